```python
import jax
import jax.numpy as jnp
from jax import lax
import numpy as np

D_MODEL = 1024
BATCH = 8
SEQ = 8192
DEPTH = 2

CTX_LEN = 256
GRID_W = 64
ATTN_HEADS = 8
ATTN_KV_HEADS = 2
HEAD_DIM = 64
WINDOW = 128
WBLOCK = 128
ROPE_THETA = 10000.0
HG_HEADS = 4
HG_DK = 128
HG_DV = 128
HG_CHUNK = 64
N_EXPERTS = 16
EXPERT_FF = 1024
CAPACITY_FACTOR = 2
N_MOD = 6
LN_EPS = 1e-6
DEEPNORM_ALPHA = (2 * DEPTH) ** 0.25
DEEPNORM_BETA = (8 * DEPTH) ** -0.25
ATTN_Q_DIM = ATTN_HEADS * HEAD_DIM
ATTN_KV_DIM = ATTN_KV_HEADS * HEAD_DIM
HG_K_DIM = HG_HEADS * HG_DK
HG_V_DIM = HG_HEADS * HG_DV
IN_SIZES = (ATTN_Q_DIM, ATTN_KV_DIM, ATTN_KV_DIM, HG_K_DIM, HG_K_DIM, HG_K_DIM, HG_V_DIM, HG_V_DIM, D_MODEL, D_MODEL)
IN_DIM = sum(IN_SIZES)

kernel_name = 'hybrid_swa_hgrn2_ecmoe_diffusion_block'


def layer_norm(x):
    xf = x.astype(jnp.float32)
    mu = jnp.mean(xf, axis=-1, keepdims=True)
    var = jnp.mean(jnp.square(xf - mu), axis=-1, keepdims=True)
    return ((xf - mu) * lax.rsqrt(var + LN_EPS)).astype(x.dtype)


def post_norm(x, g, b):
    return layer_norm(x) * g + b


def axial_rope(x):
    n_tok = x.shape[1]
    rows = n_tok // GRID_W
    row = jnp.broadcast_to(jnp.arange(rows)[:, None], (rows, GRID_W)).reshape(-1)
    col = jnp.broadcast_to(jnp.arange(GRID_W)[None, :], (rows, GRID_W)).reshape(-1)
    n_freq = HEAD_DIM // 4
    inv_freq = ROPE_THETA ** (-jnp.arange(n_freq, dtype=jnp.float32) / n_freq)

    def rotate(xh, pos):
        ang = pos.astype(jnp.float32)[:, None] * inv_freq[None, :]
        cos = jnp.cos(ang)[None, :, None, :].astype(x.dtype)
        sin = jnp.sin(ang)[None, :, None, :].astype(x.dtype)
        x1, x2 = jnp.split(xh, 2, axis=-1)
        return jnp.concatenate([x1 * cos - x2 * sin, x2 * cos + x1 * sin], axis=-1)

    x_row, x_col = jnp.split(x, 2, axis=-1)
    return jnp.concatenate([rotate(x_row, row), rotate(x_col, col)], axis=-1)


def window_attention(q, k, v, kc, vc, sink):
    B, L, H, dh = q.shape
    G = k.shape[2]
    R = H // G
    nb = L // WBLOCK
    scale = dh ** -0.5
    qb = q.reshape(B, nb, WBLOCK, G, R, dh)

    def band(t):
        tb = t.reshape(B, nb, WBLOCK, G, dh)
        tp = jnp.pad(tb, ((0, 0), (1, 1), (0, 0), (0, 0), (0, 0)))
        return jnp.concatenate([tp[:, :-2], tp[:, 1:-1], tp[:, 2:]], axis=2)

    kw, vw = band(k), band(v)
    ipos = jnp.arange(nb)[:, None] * WBLOCK + jnp.arange(WBLOCK)[None, :]
    jpos = jnp.arange(nb)[:, None] * WBLOCK - WBLOCK + jnp.arange(3 * WBLOCK)[None, :]
    rel = jpos[:, None, :] - ipos[:, :, None]
    valid = (jnp.abs(rel) <= WINDOW) & (jpos[:, None, :] >= 0) & (jpos[:, None, :] < L)
    s_win = jnp.einsum('bntgrd,bnsgd->bngrts', qb, kw).astype(jnp.float32) * scale
    s_win = jnp.where(valid[None, :, None, None], s_win, -jnp.inf)
    s_ctx = jnp.einsum('bntgrd,bcgd->bngrtc', qb, kc).astype(jnp.float32) * scale
    s_sink = sink.astype(jnp.float32).reshape(1, 1, G, R, 1, 1)
    m = jnp.maximum(jnp.maximum(s_ctx.max(-1, keepdims=True), s_win.max(-1, keepdims=True)), s_sink)
    e_ctx = jnp.exp(s_ctx - m)
    e_win = jnp.exp(s_win - m)
    denom = jnp.exp(s_sink - m) + e_ctx.sum(-1, keepdims=True) + e_win.sum(-1, keepdims=True)
    o = jnp.einsum('bngrtc,bcgd->bngrtd', e_ctx, vc) + jnp.einsum('bngrts,bnsgd->bngrtd', e_win, vw)
    o = (o / denom).astype(v.dtype)
    return o.transpose(0, 1, 4, 2, 3, 5).reshape(B, L, H * dh)


def context_attention(q, k, v, sink):
    B, Lc, H, dh = q.shape
    G = k.shape[2]
    R = H // G
    qg = q.reshape(B, Lc, G, R, dh)
    s = jnp.einsum('btgrd,bsgd->bgrts', qg, k).astype(jnp.float32) * (dh ** -0.5)
    s_sink = jnp.broadcast_to(sink.astype(jnp.float32).reshape(1, G, R, 1, 1), s.shape[:-1] + (1,))
    p = jax.nn.softmax(jnp.concatenate([s_sink, s], axis=-1), axis=-1)
    o = jnp.einsum('bgrts,bsgd->btgrd', p[..., 1:].astype(v.dtype), v)
    return o.reshape(B, Lc, H * dh)


def hgrn_lower_bounds(logits):
    cum = jnp.cumsum(jax.nn.softmax(logits.astype(jnp.float32), axis=0), axis=0)
    return cum - cum[0:1]


def forget_gate(z, lb):
    zf = z.astype(jnp.float32)
    lbh = lb.reshape(HG_HEADS, HG_DK)
    log_f = jnp.logaddexp(jnp.log(lbh), jnp.log1p(-lbh) + jax.nn.log_sigmoid(zf))
    k = (1.0 - lbh) * jax.nn.sigmoid(-zf)
    return log_f, k


def hgrn_scan(q, k, v, log_f, s0):
    B, L, H, dk = q.shape
    dv = v.shape[-1]
    nc = L // HG_CHUNK

    def chunks(t):
        return t.reshape(B, nc, HG_CHUNK, H, t.shape[-1]).transpose(1, 0, 3, 2, 4)

    tri = jnp.tril(jnp.ones((HG_CHUNK, HG_CHUNK), dtype=bool))

    def step(S, inp):
        qc, kc, vc, gc = inp
        cum = jnp.cumsum(gc, axis=2)
        diff = cum[:, :, :, None, :] - cum[:, :, None, :, :]
        decay = jnp.exp(jnp.where(tri[:, :, None], diff, -jnp.inf))
        scores = jnp.einsum('bhtk,bhsk,bhtsk->bhts', qc.astype(jnp.float32), kc, decay)
        o = jnp.einsum('bhtk,bhkv->bhtv', qc * jnp.exp(cum), S) + jnp.einsum('bhts,bhsv->bhtv', scores, vc)
        tot = cum[:, :, -1:, :]
        S_new = jnp.exp(tot[:, :, 0, :, None]) * S + jnp.einsum('bhsk,bhsv->bhkv', kc * jnp.exp(tot - cum), vc)
        return S_new, o

    s_fin, o = lax.scan(step, s0, (chunks(q), chunks(k), chunks(v), chunks(log_f)))
    o = o.transpose(1, 0, 3, 2, 4).reshape(B, L, H, dv)
    return o, s_fin


def hgrn_bidir(q, v, z_fw, z_bw, qc, vc, zc_fw, zc_bw, lb_fw, lb_bw):
    B = q.shape[0]
    s0 = jnp.zeros((B, HG_HEADS, HG_DK, HG_DV), jnp.float32)
    flip = lambda t: jnp.flip(t, axis=1)
    log_f, k = forget_gate(zc_fw, lb_fw)
    oc_fw, s_fw = hgrn_scan(qc, k, vc, log_f, s0)
    log_f, k = forget_gate(z_fw, lb_fw)
    o_fw, _ = hgrn_scan(q, k, v, log_f, s_fw)
    log_f, k = forget_gate(flip(zc_bw), lb_bw)
    oc_bw, s_bw = hgrn_scan(flip(qc), k, flip(vc), log_f, s0)
    log_f, k = forget_gate(flip(z_bw), lb_bw)
    o_bw, _ = hgrn_scan(flip(q), k, flip(v), log_f, s_bw)
    return o_fw + flip(o_bw), oc_fw + flip(oc_bw)


def hgrn_readout(o, g, norm_g):
    B, L = o.shape[:2]
    of = o.astype(jnp.float32)
    of = of * lax.rsqrt(jnp.mean(jnp.square(of), axis=-1, keepdims=True) + LN_EPS)
    return of.reshape(B, L, HG_V_DIM).astype(g.dtype) * norm_g * jax.nn.silu(g)


def token_mixer(u, uc, w_in, sink, lb_fw, lb_bw, norm_g, w_br_a, w_br_h, w_out, need_ctx):
    B, L, _ = u.shape
    Lc = uc.shape[1]
    idx = np.cumsum(IN_SIZES)[:-1].tolist()
    qa, ka, va, qh, zf, zb, ih, gh, ga, gb = jnp.split(u @ w_in, idx, axis=-1)
    cqa, cka, cva, cqh, czf, czb, cih, cgh, cga, cgb = jnp.split(uc @ w_in, idx, axis=-1)

    def heads(t, h):
        return t.reshape(B, t.shape[1], h, -1)

    kc_a, vc_a = heads(cka, ATTN_KV_HEADS), heads(cva, ATTN_KV_HEADS)
    attn = window_attention(axial_rope(heads(qa, ATTN_HEADS)), axial_rope(heads(ka, ATTN_KV_HEADS)),
                            heads(va, ATTN_KV_HEADS), kc_a, vc_a, sink)
    o_h, oc_h = hgrn_bidir(heads(jax.nn.silu(qh), HG_HEADS), heads(ih, HG_HEADS),
                           heads(zf, HG_HEADS), heads(zb, HG_HEADS),
                           heads(jax.nn.silu(cqh), HG_HEADS), heads(cih, HG_HEADS),
                           heads(czf, HG_HEADS), heads(czb, HG_HEADS), lb_fw, lb_bw)
    hg = hgrn_readout(o_h, gh, norm_g)
    y = (jax.nn.sigmoid(ga) * (attn @ w_br_a) + jax.nn.sigmoid(gb) * (hg @ w_br_h)) @ w_out
    if not need_ctx:
        return y, None
    attn_c = context_attention(heads(cqa, ATTN_HEADS), kc_a, vc_a, sink)
    hg_c = hgrn_readout(oc_h, cgh, norm_g)
    yc = (jax.nn.sigmoid(cga) * (attn_c @ w_br_a) + jax.nn.sigmoid(cgb) * (hg_c @ w_br_h)) @ w_out
    return y, yc


def expert_choice_ffn(u, w_router, w_gate, w_up, w_down):
    B, n, D = u.shape
    cap = CAPACITY_FACTOR * n // N_EXPERTS
    aff = jax.nn.softmax((u @ w_router).astype(jnp.float32), axis=-1)
    top_w, top_idx = lax.top_k(jnp.swapaxes(aff, 1, 2), cap)
    xs = jax.vmap(lambda ub, ib: ub[ib])(u, top_idx)
    h = jax.nn.silu(jnp.einsum('becd,edf->becf', xs, w_gate)) * jnp.einsum('becd,edf->becf', xs, w_up)
    y = jnp.einsum('becf,efd->becd', h, w_down) * top_w[..., None].astype(u.dtype)
    return jax.vmap(lambda yb, ib: jnp.zeros((n, D), yb.dtype).at[ib.reshape(-1)].add(yb.reshape(-1, D)))(y, top_idx)


def setup_inputs(seed: int = 0) -> dict:
    key = jax.random.key(seed)
    ks = jax.random.split(key, 24)
    f32 = jnp.float32

    def nrm(k, shape, scale):
        return jax.random.normal(k, shape, f32) * scale

    D = D_MODEL
    return {
        'x': nrm(ks[0], (BATCH, SEQ, D), 1.0),
        'c': nrm(ks[1], (BATCH, D), 1.0),
        'ctx': nrm(ks[2], (BATCH, CTX_LEN, D), 1.0),
        'c_ctx': nrm(ks[3], (D,), 1.0),
        'w_mod': nrm(ks[4], (DEPTH, D, N_MOD * D), D ** -0.5),
        'b_mod': nrm(ks[5], (DEPTH, N_MOD * D), 0.02),
        'w_in': nrm(ks[6], (DEPTH, D, IN_DIM), D ** -0.5),
        'attn_sink': nrm(ks[7], (DEPTH, ATTN_HEADS), 1.0),
        'hgrn_lb_fw': nrm(ks[8], (DEPTH, HG_K_DIM), 0.5),
        'hgrn_lb_bw': nrm(ks[9], (DEPTH, HG_K_DIM), 0.5),
        'hgrn_norm_g': 1.0 + nrm(ks[10], (DEPTH, HG_V_DIM), 0.02),
        'w_branch_attn': nrm(ks[11], (DEPTH, ATTN_Q_DIM, D), DEEPNORM_BETA * ATTN_Q_DIM ** -0.5),
        'w_branch_hgrn': nrm(ks[12], (DEPTH, HG_V_DIM, D), DEEPNORM_BETA * HG_V_DIM ** -0.5),
        'w_out': nrm(ks[13], (DEPTH, D, D), DEEPNORM_BETA * D ** -0.5),
        'w_router': nrm(ks[14], (DEPTH, D, N_EXPERTS), D ** -0.5),
        'w_gate': nrm(ks[15], (DEPTH, N_EXPERTS, D, EXPERT_FF), D ** -0.5),
        'w_up': nrm(ks[16], (DEPTH, N_EXPERTS, D, EXPERT_FF), D ** -0.5),
        'w_down': nrm(ks[17], (DEPTH, N_EXPERTS, EXPERT_FF, D), DEEPNORM_BETA * EXPERT_FF ** -0.5),
        'ln_g': 1.0 + nrm(ks[18], (DEPTH, 2, D), 0.02),
        'ln_b': nrm(ks[19], (DEPTH, 2, D), 0.02),
    }


def reference(x, c, ctx, c_ctx, w_mod, b_mod, w_in, attn_sink, hgrn_lb_fw, hgrn_lb_bw, hgrn_norm_g,
              w_branch_attn, w_branch_hgrn, w_out, w_router, w_gate, w_up, w_down, ln_g, ln_b):
    lb_fw_all = hgrn_lower_bounds(hgrn_lb_fw)
    lb_bw_all = hgrn_lower_bounds(hgrn_lb_bw)
    xc = ctx
    for l in range(DEPTH):
        need_ctx = l < DEPTH - 1
        mod = jax.nn.silu(c) @ w_mod[l] + b_mod[l]
        modc = jax.nn.silu(c_ctx) @ w_mod[l] + b_mod[l]
        sh1, sc1, g1, sh2, sc2, g2 = jnp.split(mod[:, None, :], N_MOD, axis=-1)
        sh1c, sc1c, g1c, sh2c, sc2c, g2c = jnp.split(modc, N_MOD)
        u = layer_norm(x) * (1.0 + sc1) + sh1
        uc = layer_norm(xc) * (1.0 + sc1c) + sh1c
        y, yc = token_mixer(u, uc, w_in[l], attn_sink[l], lb_fw_all[l], lb_bw_all[l], hgrn_norm_g[l],
                            w_branch_attn[l], w_branch_hgrn[l], w_out[l], need_ctx)
        x = post_norm(DEEPNORM_ALPHA * x + g1 * y, ln_g[l, 0], ln_b[l, 0])
        u = layer_norm(x) * (1.0 + sc2) + sh2
        x = post_norm(DEEPNORM_ALPHA * x + g2 * expert_choice_ffn(u, w_router[l], w_gate[l], w_up[l], w_down[l]),
                      ln_g[l, 1], ln_b[l, 1])
        if need_ctx:
            xc = post_norm(DEEPNORM_ALPHA * xc + g1c * yc, ln_g[l, 0], ln_b[l, 0])
            uc = layer_norm(xc) * (1.0 + sc2c) + sh2c
            xc = post_norm(DEEPNORM_ALPHA * xc + g2c * expert_choice_ffn(uc, w_router[l], w_gate[l], w_up[l], w_down[l]),
                           ln_g[l, 1], ln_b[l, 1])
    return x
```

```python
import functools

import jax
import jax.numpy as jnp
from jax import lax
from jax.experimental import pallas as pl
from jax.experimental.pallas import tpu as pltpu

F32 = jnp.float32
BF16 = jnp.bfloat16

GRID_W = 64
ATTN_HEADS = 8
ATTN_KV_HEADS = 2
HEAD_DIM = 64
WINDOW = 128
WBLOCK = 128
ROPE_THETA = 10000.0
HG_HEADS = 4
HG_DK = 128
HG_DV = 128
N_EXPERTS = 16
CAPACITY_FACTOR = 2
N_MOD = 6
LN_EPS = 1e-6
ATTN_Q_DIM = ATTN_HEADS * HEAD_DIM
ATTN_KV_DIM = ATTN_KV_HEADS * HEAD_DIM
HG_DIM = HG_HEADS * HG_DK

TOKEN_TILE = 256
SLOT_ALIGN = 16
VMEM_LIMIT_BYTES = 56 << 20


def _cparams(*sem):
    return pltpu.CompilerParams(dimension_semantics=sem, vmem_limit_bytes=VMEM_LIMIT_BYTES)


def _dot(a, b):
    return jnp.dot(a, b, preferred_element_type=F32)


def _dot_nt(a, b):
    return lax.dot_general(a, b, (((1,), (1,)), ((), ())), preferred_element_type=F32)


def _dot_tn(a, b):
    return lax.dot_general(a, b, (((0,), (0,)), ((), ())), preferred_element_type=F32)


def _split2(a):
    hi = a.astype(BF16)
    lo = (a - hi.astype(F32)).astype(BF16)
    return hi, lo


def _split3(a):
    hi = a.astype(BF16)
    r = a - hi.astype(F32)
    mid = r.astype(BF16)
    lo = (r - mid.astype(F32)).astype(BF16)
    return hi, mid, lo


def _layer_norm(x):
    mu = jnp.mean(x, axis=-1, keepdims=True)
    xc = x - mu
    var = jnp.mean(xc * xc, axis=-1, keepdims=True)
    return xc * lax.rsqrt(var + LN_EPS)


def _sigmoid(x):
    return 1.0 / (1.0 + jnp.exp(-x))


def _silu(x):
    return x * _sigmoid(x)


def _mod_kernel(c_ref, w_ref, b_ref, o_ref):
    a = _silu(c_ref[...])
    ah, al = _split2(a)
    wh, wl = _split2(w_ref[...])
    o_ref[...] = _dot(ah, wh) + _dot(ah, wl) + _dot(al, wh) + b_ref[...]


def _modulation(c_all, w_mod, b_mod, layer):
    rows, d = c_all.shape
    cols = w_mod.shape[-1]
    tn = 1024
    return pl.pallas_call(
        _mod_kernel,
        grid=(cols // tn,),
        in_specs=[
            pl.BlockSpec((rows, d), lambda j: (0, 0)),
            pl.BlockSpec((None, d, tn), lambda j: (layer, 0, j)),
            pl.BlockSpec((None, 1, tn), lambda j: (layer, 0, j)),
        ],
        out_specs=pl.BlockSpec((rows, tn), lambda j: (0, j)),
        out_shape=jax.ShapeDtypeStruct((rows, cols), F32),
        compiler_params=_cparams("arbitrary"),
        name="modulation",
    )(c_all, w_mod, b_mod)


def _rope(x, cos, sin):
    w = x.shape[1]
    lane = lax.broadcasted_iota(jnp.int32, x.shape, 1)
    partner = jnp.where((lane & 16) == 0, pltpu.roll(x, w - 16, 1), pltpu.roll(x, 16, 1))
    return x * cos + partner * sin


def _inproj_kernel(*refs, rope):
    if rope:
        x_ref, sc_ref, sh_ref, w_ref, cos_ref, sin_ref, q_ref, kv_ref, hg_ref, gab_ref = refs
    else:
        x_ref, sc_ref, sh_ref, w_ref, q_ref, kv_ref, hg_ref, gab_ref = refs
    u = _layer_norm(x_ref[0]) * (1.0 + sc_ref[0]) + sh_ref[0]
    ub = u.astype(BF16)
    kv0 = ATTN_Q_DIM
    v0 = kv0 + ATTN_KV_DIM
    hg0 = v0 + ATTN_KV_DIM
    gab0 = hg0 + 5 * HG_DIM
    q = _dot(ub, w_ref[:, 0:kv0])
    k = _dot(ub, w_ref[:, kv0:v0])
    if rope:
        q = _rope(q, cos_ref[...], sin_ref[...])
        k = _rope(k, cos_ref[:, 0:ATTN_KV_DIM], sin_ref[:, 0:ATTN_KV_DIM])
    q_ref[0] = q
    kv_ref[0, :, 0:ATTN_KV_DIM] = k
    kv_ref[0, :, ATTN_KV_DIM:] = _dot(ub, w_ref[:, v0:hg0])
    hg_ref[0] = _dot(ub, w_ref[:, hg0:gab0])
    gab_ref[0] = _dot(ub, w_ref[:, gab0:])


def _in_projection(x, sc, sh, w_in, layer, rope_tabs):
    b, l, d = x.shape
    tm = TOKEN_TILE
    per_batch = sc.shape[0] > 1
    mod_map = (lambda bi, i: (bi, 0, 0)) if per_batch else (lambda bi, i: (0, 0, 0))
    in_dim = w_in.shape[-1]
    in_specs = [
        pl.BlockSpec((1, tm, d), lambda bi, i: (bi, i, 0)),
        pl.BlockSpec((1, 1, d), mod_map),
        pl.BlockSpec((1, 1, d), mod_map),
        pl.BlockSpec((None, d, in_dim), lambda bi, i: (layer, 0, 0), pipeline_mode=pl.Buffered(1)),
    ]
    args = [x, sc, sh, w_in]
    if rope_tabs is not None:
        in_specs += [pl.BlockSpec((tm, ATTN_Q_DIM), lambda bi, i: (i, 0))] * 2
        args += list(rope_tabs)
    widths = (ATTN_Q_DIM, 2 * ATTN_KV_DIM, 5 * HG_DIM, 2 * d)
    return pl.pallas_call(
        functools.partial(_inproj_kernel, rope=rope_tabs is not None),
        grid=(b, l // tm),
        in_specs=in_specs,
        out_specs=[pl.BlockSpec((1, tm, wd), lambda bi, i: (bi, i, 0)) for wd in widths],
        out_shape=[jax.ShapeDtypeStruct((b, l, wd), F32) for wd in widths],
        compiler_params=_cparams("parallel", "arbitrary"),
        name="in_projection",
    )(*args)


def _attn_kernel(*refs, layer, nb, windowed):
    if windowed:
        sink_ref, q_ref, kvp_ref, kvc_ref, kvn_ref, ckv_ref, o_ref = refs
        kvs = [ckv_ref[0], kvp_ref[0], kvc_ref[0], kvn_ref[0]]
    else:
        sink_ref, q_ref, ckv_ref, o_ref = refs
        kvs = [ckv_ref[0]]
    q = q_ref[0]
    tq = q.shape[0]
    lc = kvs[0].shape[0]
    rep = ATTN_HEADS // ATTN_KV_HEADS
    scale = HEAD_DIM ** -0.5
    if windowed:
        n = pl.program_id(1)
        span = 3 * WBLOCK
        ti = lax.broadcasted_iota(jnp.int32, (tq, span), 0)
        sj = lax.broadcasted_iota(jnp.int32, (tq, span), 1)
        rel = sj - WBLOCK - ti
        ok = (rel <= WINDOW) & (rel >= -WINDOW)
        ok = ok & ((sj >= WBLOCK) | (n > 0)) & ((sj < 2 * WBLOCK) | (n < nb - 1))
        mask = jnp.concatenate([jnp.ones((tq, lc), jnp.bool_), ok], axis=1)
        mask = jnp.concatenate([mask] * rep, axis=0)
    outs = []
    for g in range(ATTN_KV_HEADS):
        kk = jnp.concatenate([t[:, g * HEAD_DIM:(g + 1) * HEAD_DIM] for t in kvs], axis=0).astype(BF16)
        v0 = ATTN_KV_DIM + g * HEAD_DIM
        vv = jnp.concatenate([t[:, v0:v0 + HEAD_DIM] for t in kvs], axis=0).astype(BF16)
        heads = range(g * rep, (g + 1) * rep)
        qg = jnp.concatenate([q[:, h * HEAD_DIM:(h + 1) * HEAD_DIM] for h in heads], axis=0).astype(BF16)
        s = _dot_nt(qg, kk) * scale
        if windowed:
            s = jnp.where(mask, s, -jnp.inf)
        sink = jnp.concatenate([jnp.full((tq, 1), sink_ref[layer, h], F32) for h in heads], axis=0)
        m = jnp.maximum(jnp.max(s, axis=1, keepdims=True), sink)
        e = jnp.exp(s - m)
        den = jnp.exp(sink - m) + jnp.sum(e, axis=1, keepdims=True)
        o = _dot(e.astype(BF16), vv) / den
        outs += [o[r * tq:(r + 1) * tq] for r in range(rep)]
    o_ref[0] = jnp.concatenate(outs, axis=1).astype(BF16)


def _attention(q, kv, ckv, sink, layer):
    b, l, _ = q.shape
    lc = ckv.shape[1]
    nb = l // WBLOCK
    windowed = kv is not None
    smem = pl.BlockSpec(memory_space=pltpu.SMEM)
    qspec = pl.BlockSpec((1, WBLOCK, ATTN_Q_DIM), lambda bi, n: (bi, n, 0))
    cspec = pl.BlockSpec((1, lc, 2 * ATTN_KV_DIM), lambda bi, n: (bi, 0, 0))
    if windowed:
        kvw = 2 * ATTN_KV_DIM
        in_specs = [
            smem, qspec,
            pl.BlockSpec((1, WBLOCK, kvw), lambda bi, n: (bi, jnp.maximum(n - 1, 0), 0)),
            pl.BlockSpec((1, WBLOCK, kvw), lambda bi, n: (bi, n, 0)),
            pl.BlockSpec((1, WBLOCK, kvw), lambda bi, n: (bi, jnp.minimum(n + 1, nb - 1), 0)),
            cspec,
        ]
        args = (sink, q, kv, kv, kv, ckv)
    else:
        in_specs = [smem, qspec, cspec]
        args = (sink, q, ckv)
    return pl.pallas_call(
        functools.partial(_attn_kernel, layer=layer, nb=nb, windowed=windowed),
        grid=(b, nb),
        in_specs=in_specs,
        out_specs=pl.BlockSpec((1, WBLOCK, ATTN_Q_DIM), lambda bi, n: (bi, n, 0)),
        out_shape=jax.ShapeDtypeStruct((b, l, ATTN_Q_DIM), BF16),
        compiler_params=_cparams("parallel", "arbitrary"),
        name="window_attention" if windowed else "context_attention",
    )(*args)


def _block_ref_rows(cum, blk, r):
    c, w = cum.shape
    if blk % 8 == 0:
        x = cum.reshape(c // blk, blk, w)
        return jnp.broadcast_to(x[:, r:r + 1, :], x.shape).reshape(c, w)
    pos = lax.broadcasted_iota(jnp.int32, cum.shape, 0) & (blk - 1)
    out = cum
    for p in range(blk):
        delta = r - p
        if delta == 0:
            continue
        out = jnp.where(pos == p, pltpu.roll(cum, (-delta) % c, 0), out)
    return out


def _lower_bound(lb_ref, layer):
    logits = lb_ref[...]
    m = jnp.max(logits, axis=0, keepdims=True)
    ex = jnp.exp(logits - m)
    p = ex / jnp.sum(ex, axis=0, keepdims=True)
    lb = jnp.zeros_like(p[0:1])
    for j in range(1, layer + 1):
        lb = lb + p[j:j + 1]
    return lb


def _hgrn_kernel(*refs, layer, has_init):
    if has_init:
        (lbf_ref, lbb_ref, tri_ref, qf_ref, zf_ref, vf_ref, qb_ref, zb_ref, vb_ref, s0_ref,
         of_ref, ob_ref, s_ref) = refs
    else:
        (lbf_ref, lbb_ref, tri_ref, qf_ref, zf_ref, vf_ref, qb_ref, zb_ref, vb_ref,
         of_ref, ob_ref, s_ref) = refs

    @pl.when(pl.program_id(1) == 0)
    def _():
        if has_init:
            s_ref[...] = s0_ref[...]
        else:
            s_ref[...] = jnp.zeros(s_ref.shape, F32)

    c = qf_ref.shape[1]
    row = lax.broadcasted_iota(jnp.int32, (c, HG_DK), 0)
    ti = lax.broadcasted_iota(jnp.int32, (c, c), 0)
    sj = lax.broadcasted_iota(jnp.int32, (c, c), 1)
    dirs = ((qf_ref, zf_ref, vf_ref, of_ref, lbf_ref), (qb_ref, zb_ref, vb_ref, ob_ref, lbb_ref))
    for d, (q_ref, z_ref, v_ref, o_ref, lb_ref) in enumerate(dirs):
        backward = d == 1
        lb_all = _lower_bound(lb_ref, layer)
        tri = tri_ref[d]
        for h in range(HG_HEADS):
            sl = slice(h * HG_DK, (h + 1) * HG_DK)
            lb = lb_all[:, sl]
            z = z_ref[0, :, sl]
            q = _silu(q_ref[0, :, sl])
            v = v_ref[0, :, sl]
            vb = v.astype(BF16)
            log_sig = jnp.minimum(z, 0.0) - jnp.log1p(jnp.exp(-jnp.abs(z)))
            a = jnp.log(lb)
            bb = jnp.log1p(-lb) + log_sig
            g = jnp.maximum(a, bb) + jnp.log1p(jnp.exp(-jnp.abs(a - bb)))
            kk = (1.0 - lb) * _sigmoid(-z)
            g3 = _split3(g)
            cum = _dot(tri, g3[0]) + _dot(tri, g3[1]) + _dot(tri, g3[2])
            st = s_ref[0, d, h]
            o = _dot_nt((q * jnp.exp(cum)).astype(BF16), st.astype(BF16))
            o = o + jnp.sum(q * kk, axis=1, keepdims=True) * v
            scores = jnp.zeros((c, c), F32)
            hs = c // 2
            while hs >= 1:
                blk = 2 * hs
                dd = cum - _block_ref_rows(cum, blk, hs if backward else hs - 1)
                second = (row & (blk - 1)) >= hs
                q_half = jnp.logical_not(second) if backward else second
                e = jnp.exp(jnp.where(q_half, dd, -dd))
                qt = jnp.where(q_half, q * e, 0.0).astype(BF16)
                kt = jnp.where(q_half, 0.0, kk * e).astype(BF16)
                sc = _dot_nt(qt, kt)
                if blk < c:
                    sc = jnp.where((ti & -blk) == (sj & -blk), sc, 0.0)
                scores = scores + sc
                hs //= 2
            o = o + _dot(scores.astype(BF16), vb)
            o_ref[0, :, sl] = o
            tot = cum[0:1] if backward else cum[c - 1:c]
            kdec = kk * jnp.exp(tot - cum)
            s_ref[0, d, h] = st * jnp.exp(tot) + _dot_tn(vb, kdec.astype(BF16))


def _hgrn_scan(hgp, lb_fw, lb_bw, tri, s0, layer):
    b, l, _ = hgp.shape
    tt = TOKEN_TILE
    ns = l // tt
    has_init = s0 is not None
    full2 = lambda bi, i: (0, 0)
    fwd = lambda j: pl.BlockSpec((1, tt, HG_DIM), lambda bi, i: (bi, i, j))
    bwd = lambda j: pl.BlockSpec((1, tt, HG_DIM), lambda bi, i: (bi, ns - 1 - i, j))
    sspec = pl.BlockSpec((1, 2, HG_HEADS, HG_DV, HG_DK), lambda bi, i: (bi, 0, 0, 0, 0))
    in_specs = [
        pl.BlockSpec(lb_fw.shape, full2), pl.BlockSpec(lb_bw.shape, full2),
        pl.BlockSpec(tri.shape, lambda bi, i: (0, 0, 0)),
        fwd(0), fwd(1), fwd(3), bwd(0), bwd(2), bwd(3),
    ]
    args = [lb_fw, lb_bw, tri, hgp, hgp, hgp, hgp, hgp, hgp]
    if has_init:
        in_specs.append(sspec)
        args.append(s0)
    return pl.pallas_call(
        functools.partial(_hgrn_kernel, layer=layer, has_init=has_init),
        grid=(b, ns),
        in_specs=in_specs,
        out_specs=[
            pl.BlockSpec((1, tt, HG_DIM), lambda bi, i: (bi, i, 0)),
            pl.BlockSpec((1, tt, HG_DIM), lambda bi, i: (bi, ns - 1 - i, 0)),
            sspec,
        ],
        out_shape=[
            jax.ShapeDtypeStruct((b, l, HG_DIM), F32),
            jax.ShapeDtypeStruct((b, l, HG_DIM), F32),
            jax.ShapeDtypeStruct((b, 2, HG_HEADS, HG_DV, HG_DK), F32),
        ],
        compiler_params=_cparams("parallel", "arbitrary"),
        name="hgrn_scan",
    )(*args)


def _mixout_kernel(attn_ref, of_ref, ob_ref, gh_ref, ga_ref, gb_ref, x_ref, g1_ref, sc2_ref, sh2_ref,
                   ng_ref, wa_ref, wh_ref, wo_ref, lng_ref, lnb_ref, wrh_ref, wrl_ref,
                   xo_ref, u2_ref, aff_ref, *, alpha):
    o = of_ref[0] + ob_ref[0]
    parts = []
    for h in range(HG_HEADS):
        oh = o[:, h * HG_DV:(h + 1) * HG_DV]
        parts.append(oh * lax.rsqrt(jnp.mean(oh * oh, axis=-1, keepdims=True) + LN_EPS))
    hg = jnp.concatenate(parts, axis=1) * ng_ref[...] * _silu(gh_ref[0])
    ya = _dot(attn_ref[0], wa_ref[...])
    yh = _dot(hg.astype(BF16), wh_ref[...])
    mix = _sigmoid(ga_ref[0]) * ya + _sigmoid(gb_ref[0]) * yh
    y = _dot(mix.astype(BF16), wo_ref[...])
    xn = _layer_norm(alpha * x_ref[0] + g1_ref[0] * y) * lng_ref[...] + lnb_ref[...]
    xo_ref[0] = xn
    u2 = _layer_norm(xn) * (1.0 + sc2_ref[0]) + sh2_ref[0]
    u2_ref[0] = u2.astype(BF16)
    uh, ul = _split2(u2)
    lg = _dot_nt(wrh_ref[...], uh) + _dot_nt(wrh_ref[...], ul) + _dot_nt(wrl_ref[...], uh)
    ex = jnp.exp(lg - jnp.max(lg, axis=0, keepdims=True))
    aff_ref[0] = ex / jnp.sum(ex, axis=0, keepdims=True)


def _mixer_output(attn, o_f, o_b, hgp, gab, x, g1, sc2, sh2, norm_g, w_a, w_h, w_o, ln_g, ln_b,
                  wr_hi, wr_lo, layer, alpha):
    b, l, d = x.shape
    tm = TOKEN_TILE
    per_batch = g1.shape[0] > 1
    mod_map = (lambda bi, i: (bi, 0, 0)) if per_batch else (lambda bi, i: (0, 0, 0))
    tok = lambda w, j=0: pl.BlockSpec((1, tm, w), lambda bi, i: (bi, i, j))
    lay2 = lambda r, c: pl.BlockSpec((None, r, c), lambda bi, i: (layer, 0, 0))
    in_specs = [
        tok(ATTN_Q_DIM), tok(HG_DIM), tok(HG_DIM), tok(HG_DIM, 4), tok(d, 0), tok(d, 1), tok(d),
        pl.BlockSpec((1, 1, d), mod_map), pl.BlockSpec((1, 1, d), mod_map), pl.BlockSpec((1, 1, d), mod_map),
        lay2(1, HG_DIM), lay2(ATTN_Q_DIM, d), lay2(HG_DIM, d), lay2(d, d),
        pl.BlockSpec((None, None, 1, d), lambda bi, i: (layer, 0, 0, 0)),
        pl.BlockSpec((None, None, 1, d), lambda bi, i: (layer, 0, 0, 0)),
        lay2(N_EXPERTS, d), lay2(N_EXPERTS, d),
    ]
    return pl.pallas_call(
        functools.partial(_mixout_kernel, alpha=alpha),
        grid=(b, l // tm),
        in_specs=in_specs,
        out_specs=[tok(d), tok(d), pl.BlockSpec((1, N_EXPERTS, tm), lambda bi, i: (bi, 0, i))],
        out_shape=[
            jax.ShapeDtypeStruct((b, l, d), F32),
            jax.ShapeDtypeStruct((b, l, d), BF16),
            jax.ShapeDtypeStruct((b, N_EXPERTS, l), F32),
        ],
        compiler_params=_cparams("parallel", "arbitrary"),
        name="mixer_output",
    )(attn, o_f, o_b, hgp, gab, gab, x, g1, sc2, sh2, norm_g, w_a, w_h, w_o, ln_g, ln_b, wr_hi, wr_lo)


def _select_kernel(aff_ref, triu_ref, rk_ref, st_ref, *, cap):
    a = aff_ref[0]
    ne, n = a.shape
    ts = triu_ref.shape[0]
    bits = lax.bitcast_convert_type(a, jnp.int32)
    capf = jnp.float32(cap)

    def count_ge(v):
        return jnp.sum(jnp.where(bits >= v, 1.0, 0.0), axis=1, keepdims=True)

    def body(_, carry):
        lo, hi = carry
        mid = lo + ((hi - lo) >> 1)
        ok = count_ge(mid) >= capf
        return jnp.where(ok, mid, lo), jnp.where(ok, hi, mid)

    lo0 = jnp.zeros((ne, 1), jnp.int32)
    hi0 = jnp.full((ne, 1), 0x7F800000, jnp.int32)
    thr, _ = lax.fori_loop(0, 31, body, (lo0, hi0))
    gt = bits > thr
    eq = bits == thr
    need = capf - jnp.sum(jnp.where(gt, 1.0, 0.0), axis=1, keepdims=True)
    triu = triu_ref[...]
    lane = lax.broadcasted_iota(jnp.int32, st_ref.shape[1:], 1)
    eq_carry = jnp.zeros((ne, 1), F32)
    sel_carry = jnp.zeros((ne, 1), F32)
    starts = jnp.zeros(st_ref.shape[1:], F32)
    for t in range(n // ts):
        sl = slice(t * ts, (t + 1) * ts)
        eq_t = eq[:, sl]
        eq_pref = _dot(jnp.where(eq_t, 1.0, 0.0).astype(BF16), triu) + eq_carry
        sel_t = gt[:, sl] | (eq_t & (eq_pref <= need))
        sel_f = jnp.where(sel_t, 1.0, 0.0)
        sel_pref = _dot(sel_f.astype(BF16), triu) + sel_carry
        rk_ref[0, :, sl] = jnp.where(sel_t, sel_pref - 1.0, -1.0)
        starts = jnp.where(lane == t, sel_carry, starts)
        eq_carry = eq_pref[:, ts - 1:ts]
        sel_carry = sel_pref[:, ts - 1:ts]
    st_ref[0] = starts


def _select(aff, triu, cap):
    b, ne, n = aff.shape
    return pl.pallas_call(
        functools.partial(_select_kernel, cap=cap),
        grid=(b,),
        in_specs=[
            pl.BlockSpec((1, ne, n), lambda bi: (bi, 0, 0)),
            pl.BlockSpec(triu.shape, lambda bi: (0, 0)),
        ],
        out_specs=[
            pl.BlockSpec((1, ne, n), lambda bi: (bi, 0, 0)),
            pl.BlockSpec((1, ne, 128), lambda bi: (bi, 0, 0)),
        ],
        out_shape=[
            jax.ShapeDtypeStruct((b, ne, n), F32),
            jax.ShapeDtypeStruct((b, ne, 128), F32),
        ],
        compiler_params=_cparams("parallel"),
        name="expert_choice_select",
    )(aff, triu)


def _slot_window(cap):
    return min(cap, TOKEN_TILE + SLOT_ALIGN)


def _window_start(s0, cap):
    w = _slot_window(cap)
    return jnp.minimum((s0 // SLOT_ALIGN) * SLOT_ALIGN, cap - w)


def _ffn_kernel(st_ref, u_ref, rk_ref, aff_ref, wg_ref, wu_ref, wd_ref, y_ref, xs_ref, ws_ref,
                *, cap, nt, tiles):
    b, e, k = pl.program_id(0), pl.program_id(1), pl.program_id(2)
    w = _slot_window(cap)

    @pl.when(k == 0)
    def _():
        xs_ref[...] = jnp.zeros(xs_ref.shape, F32)
        ws_ref[...] = jnp.zeros(ws_ref.shape, F32)

    for j in range(tiles):
        t = k * tiles + j
        a = pl.multiple_of(_window_start(st_ref[(b * N_EXPERTS + e) * nt + t], cap), SLOT_ALIGN)
        sl = slice(j * TOKEN_TILE, (j + 1) * TOKEN_TILE)
        slot = (a + lax.broadcasted_iota(jnp.int32, (w, 1), 0)).astype(F32)
        hit = slot == rk_ref[0, 0, :, sl]
        xs_ref[pl.ds(a, w), :] += _dot(jnp.where(hit, 1.0, 0.0).astype(BF16), u_ref[0, sl, :])
        ws_ref[pl.ds(a, w), :] += jnp.sum(jnp.where(hit, aff_ref[0, 0, :, sl], 0.0), axis=1, keepdims=True)

    @pl.when(k == pl.num_programs(2) - 1)
    def _():
        rc = min(cap, 256)
        for c in range(cap // rc):
            rows = slice(c * rc, (c + 1) * rc)
            xb = xs_ref[rows, :].astype(BF16)
            h = _silu(_dot(xb, wg_ref[0])) * _dot(xb, wu_ref[0])
            y_ref[0, 0, rows, :] = (_dot(h.astype(BF16), wd_ref[0]) * ws_ref[rows, :]).astype(BF16)


def _expert_ffn(starts, u2, rk4, aff4, w_gate, w_up, w_down, layer, cap):
    b, n, d = u2.shape
    ff = w_gate.shape[-1]
    nt = n // TOKEN_TILE
    tk = min(n, 1024)
    tiles = tk // TOKEN_TILE
    wspec = lambda r, c: pl.BlockSpec((None, 1, r, c), lambda bi, e, k, st: (layer, e, 0, 0))
    grid_spec = pltpu.PrefetchScalarGridSpec(
        num_scalar_prefetch=1,
        grid=(b, N_EXPERTS, n // tk),
        in_specs=[
            pl.BlockSpec((1, tk, d), lambda bi, e, k, st: (bi, k, 0)),
            pl.BlockSpec((1, 1, 1, tk), lambda bi, e, k, st: (bi, e, 0, k)),
            pl.BlockSpec((1, 1, 1, tk), lambda bi, e, k, st: (bi, e, 0, k)),
            wspec(d, ff), wspec(d, ff), wspec(ff, d),
        ],
        out_specs=pl.BlockSpec((1, 1, cap, d), lambda bi, e, k, st: (bi, e, 0, 0)),
        scratch_shapes=[pltpu.VMEM((cap, d), F32), pltpu.VMEM((cap, 1), F32)],
    )
    return pl.pallas_call(
        functools.partial(_ffn_kernel, cap=cap, nt=nt, tiles=tiles),
        grid_spec=grid_spec,
        out_shape=jax.ShapeDtypeStruct((b, N_EXPERTS, cap, d), BF16),
        compiler_params=_cparams("parallel", "arbitrary", "arbitrary"),
        name="expert_ffn",
    )(starts, u2, rk4, aff4, w_gate, w_up, w_down)


def _combine_kernel(*refs, cap, nt, alpha):
    st_ref, rk_ref, x_ref, g2_ref, lng_ref, lnb_ref = refs[:6]
    y_refs = refs[6:6 + N_EXPERTS]
    o_ref = refs[6 + N_EXPERTS]
    b, t = pl.program_id(0), pl.program_id(1)
    w = _slot_window(cap)
    acc = jnp.zeros(x_ref.shape[1:], F32)
    for e in range(N_EXPERTS):
        a = _window_start(st_ref[(b * N_EXPERTS + e) * nt + t], cap)
        slot = (a + lax.broadcasted_iota(jnp.int32, (w, 1), 0)).astype(F32)
        hit = slot == rk_ref[0, e:e + 1, :]
        acc = acc + _dot_tn(jnp.where(hit, 1.0, 0.0).astype(BF16), y_refs[e][...])
    xr = alpha * x_ref[0] + g2_ref[0] * acc
    o_ref[0] = _layer_norm(xr) * lng_ref[...] + lnb_ref[...]


def _combine(starts, rk, x, g2, ln_g, ln_b, y, layer, cap, alpha):
    b, n, d = x.shape
    tt = TOKEN_TILE
    nt = n // tt
    w = _slot_window(cap)
    per_batch = g2.shape[0] > 1
    mod_map = (lambda bi, t, st: (bi, 0, 0)) if per_batch else (lambda bi, t, st: (0, 0, 0))

    def y_spec(e):
        def index_map(bi, t, st):
            be = bi * N_EXPERTS + e
            return pl.multiple_of(be * cap + _window_start(st[be * nt + t], cap), SLOT_ALIGN), 0
        return pl.BlockSpec((pl.Element(w), pl.Element(d)), index_map)

    y = y.reshape(b * N_EXPERTS * cap, d)

    grid_spec = pltpu.PrefetchScalarGridSpec(
        num_scalar_prefetch=1,
        grid=(b, nt),
        in_specs=[
            pl.BlockSpec((1, N_EXPERTS, tt), lambda bi, t, st: (bi, 0, t)),
            pl.BlockSpec((1, tt, d), lambda bi, t, st: (bi, t, 0)),
            pl.BlockSpec((1, 1, d), mod_map),
            pl.BlockSpec((None, None, 1, d), lambda bi, t, st: (layer, 1, 0, 0)),
            pl.BlockSpec((None, None, 1, d), lambda bi, t, st: (layer, 1, 0, 0)),
        ] + [y_spec(e) for e in range(N_EXPERTS)],
        out_specs=pl.BlockSpec((1, tt, d), lambda bi, t, st: (bi, t, 0)),
    )
    return pl.pallas_call(
        functools.partial(_combine_kernel, cap=cap, nt=nt, alpha=alpha),
        grid_spec=grid_spec,
        out_shape=jax.ShapeDtypeStruct((b, n, d), F32),
        compiler_params=_cparams("parallel", "arbitrary"),
        name="moe_combine",
    )(starts, rk, x, g2, ln_g, ln_b, *([y] * N_EXPERTS))


def _moe(x, u2, aff, g2, ln_g, ln_b, triu, w_gate, w_up, w_down, layer, alpha):
    b, n, _ = x.shape
    cap = CAPACITY_FACTOR * n // N_EXPERTS
    nt = n // TOKEN_TILE
    rk, st = _select(aff, triu, cap)
    starts = st[:, :, :nt].astype(jnp.int32).reshape(-1)
    rk4 = rk.reshape(b, N_EXPERTS, 1, n)
    aff4 = aff.reshape(b, N_EXPERTS, 1, n)
    y = _expert_ffn(starts, u2, rk4, aff4, w_gate, w_up, w_down, layer, cap)
    return _combine(starts, rk, x, g2, ln_g, ln_b, y, layer, cap, alpha)


def _rope_tables(l):
    pos = jnp.arange(l)
    nf = HEAD_DIM // 4
    inv = ROPE_THETA ** (-jnp.arange(nf, dtype=F32) / nf)
    ang_r = (pos // GRID_W).astype(F32)[:, None] * inv[None, :]
    ang_c = (pos % GRID_W).astype(F32)[:, None] * inv[None, :]
    cos = jnp.concatenate([jnp.cos(ang_r)] * 2 + [jnp.cos(ang_c)] * 2, axis=1)
    sin = jnp.concatenate([-jnp.sin(ang_r), jnp.sin(ang_r), -jnp.sin(ang_c), jnp.sin(ang_c)], axis=1)
    return jnp.tile(cos, (1, ATTN_HEADS)), jnp.tile(sin, (1, ATTN_HEADS))


def kernel(x, c, ctx, c_ctx, w_mod, b_mod, w_in, attn_sink, hgrn_lb_fw, hgrn_lb_bw, hgrn_norm_g,
           w_branch_attn, w_branch_hgrn, w_out, w_router, w_gate, w_up, w_down, ln_g, ln_b):
    depth, d = w_mod.shape[0], x.shape[-1]
    b = x.shape[0]
    assert x.shape[1] % (4 * TOKEN_TILE) == 0 and ctx.shape[1] == TOKEN_TILE
    alpha = (2 * depth) ** 0.25
    mod_rows = -(-(b + 1) // 8) * 8
    c_all = jnp.concatenate([c, c_ctx[None, :], jnp.zeros((mod_rows - b - 1, d), F32)], axis=0)
    b_mod3 = b_mod[:, None, :]
    norm_g3 = hgrn_norm_g[:, None, :]
    ln_g4, ln_b4 = ln_g[:, :, None, :], ln_b[:, :, None, :]
    w_in_b = w_in.astype(BF16)
    w_a_b, w_h_b, w_o_b = w_branch_attn.astype(BF16), w_branch_hgrn.astype(BF16), w_out.astype(BF16)
    w_g_b, w_u_b, w_d_b = w_gate.astype(BF16), w_up.astype(BF16), w_down.astype(BF16)
    wr_t = jnp.swapaxes(w_router, 1, 2)
    wr_hi = wr_t.astype(BF16)
    wr_lo = (wr_t - wr_hi.astype(F32)).astype(BF16)
    rope_tabs = _rope_tables(x.shape[1])
    tt = TOKEN_TILE
    r_i = lax.broadcasted_iota(jnp.int32, (tt, tt), 0)
    c_i = lax.broadcasted_iota(jnp.int32, (tt, tt), 1)
    tri = jnp.stack([(c_i <= r_i), (c_i >= r_i)]).astype(BF16)
    triu = (r_i <= c_i).astype(BF16)

    xc = ctx
    for l in range(depth):
        need_ctx = l < depth - 1
        mod = _modulation(c_all, w_mod, b_mod3, l)
        sh1, sc1, g1, sh2, sc2, g2 = [mod[:b, i * d:(i + 1) * d][:, None, :] for i in range(N_MOD)]
        sh1c, sc1c, g1c, sh2c, sc2c, g2c = [mod[b:b + 1, i * d:(i + 1) * d][:, None, :] for i in range(N_MOD)]
        cq, ckv, chg, cgab = _in_projection(xc, sc1c, sh1c, w_in_b, l, None)
        oc_f, oc_b, s_ctx = _hgrn_scan(chg, hgrn_lb_fw, hgrn_lb_bw, tri, None, l)
        q, kv, hgp, gab = _in_projection(x, sc1, sh1, w_in_b, l, rope_tabs)
        o_f, o_b, _ = _hgrn_scan(hgp, hgrn_lb_fw, hgrn_lb_bw, tri, s_ctx, l)
        attn = _attention(q, kv, ckv, attn_sink, l)
        x1, u2, aff = _mixer_output(attn, o_f, o_b, hgp, gab, x, g1, sc2, sh2, norm_g3, w_a_b, w_h_b, w_o_b,
                                    ln_g4, ln_b4, wr_hi, wr_lo, l, alpha)
        x = _moe(x1, u2, aff, g2, ln_g4, ln_b4, triu, w_g_b, w_u_b, w_d_b, l, alpha)
        if need_ctx:
            attn_c = _attention(cq, None, ckv, attn_sink, l)
            xc1, uc2, affc = _mixer_output(attn_c, oc_f, oc_b, chg, cgab, xc, g1c, sc2c, sh2c, norm_g3,
                                           w_a_b, w_h_b, w_o_b, ln_g4, ln_b4, wr_hi, wr_lo, l, alpha)
            xc = _moe(xc1, uc2, affc, g2c, ln_g4, ln_b4, triu, w_g_b, w_u_b, w_d_b, l, alpha)
    return x
```

```python
import functools

import jax
import jax.numpy as jnp
from jax import lax
from jax.experimental import pallas as pl
from jax.experimental.pallas import tpu as pltpu

F32 = jnp.float32
BF16 = jnp.bfloat16

GRID_W = 64
ATTN_HEADS = 8
ATTN_KV_HEADS = 2
HEAD_DIM = 64
WINDOW = 128
WBLOCK = 128
ROPE_THETA = 10000.0
HG_HEADS = 4
HG_DK = 128
HG_DV = 128
N_EXPERTS = 16
CAPACITY_FACTOR = 2
N_MOD = 6
LN_EPS = 1e-6
ATTN_Q_DIM = ATTN_HEADS * HEAD_DIM
ATTN_KV_DIM = ATTN_KV_HEADS * HEAD_DIM
HG_DIM = HG_HEADS * HG_DK

TOKEN_TILE = 256
HG_CHUNK = 128
ROUTE_TILE = 128
SLOT_ALIGN = 16
VMEM_LIMIT_BYTES = 56 << 20


def _cparams(*sem):
    return pltpu.CompilerParams(dimension_semantics=sem, vmem_limit_bytes=VMEM_LIMIT_BYTES)


def _dot(a, b):
    return jnp.dot(a, b, preferred_element_type=F32)


def _dot_nt(a, b):
    return lax.dot_general(a, b, (((1,), (1,)), ((), ())), preferred_element_type=F32)


def _dot_tn(a, b):
    return lax.dot_general(a, b, (((0,), (0,)), ((), ())), preferred_element_type=F32)


def _split2(a):
    hi = a.astype(BF16)
    lo = (a - hi.astype(F32)).astype(BF16)
    return hi, lo


def _split3(a):
    hi = a.astype(BF16)
    r = a - hi.astype(F32)
    mid = r.astype(BF16)
    lo = (r - mid.astype(F32)).astype(BF16)
    return hi, mid, lo


def _layer_norm(x):
    mu = jnp.mean(x, axis=-1, keepdims=True)
    xc = x - mu
    var = jnp.mean(xc * xc, axis=-1, keepdims=True)
    return xc * lax.rsqrt(var + LN_EPS)


def _sigmoid(x):
    return 1.0 / (1.0 + jnp.exp(-x))


def _silu(x):
    return x * _sigmoid(x)


def _mod_kernel(c_ref, w_ref, b_ref, o_ref):
    a = _silu(c_ref[...])
    ah, al = _split2(a)
    wh, wl = _split2(w_ref[...])
    o_ref[...] = _dot(ah, wh) + _dot(ah, wl) + _dot(al, wh) + b_ref[...]


def _modulation(c_all, w_mod, b_mod, layer):
    rows, d = c_all.shape
    cols = w_mod.shape[-1]
    tn = 1024
    return pl.pallas_call(
        _mod_kernel,
        grid=(cols // tn,),
        in_specs=[
            pl.BlockSpec((rows, d), lambda j: (0, 0)),
            pl.BlockSpec((None, d, tn), lambda j: (layer, 0, j)),
            pl.BlockSpec((None, 1, tn), lambda j: (layer, 0, j)),
        ],
        out_specs=pl.BlockSpec((rows, tn), lambda j: (0, j)),
        out_shape=jax.ShapeDtypeStruct((rows, cols), F32),
        compiler_params=_cparams("arbitrary"),
        name="modulation",
    )(c_all, w_mod, b_mod)


def _rope(x, cos, sin):
    w = x.shape[1]
    lane = lax.broadcasted_iota(jnp.int32, x.shape, 1)
    partner = jnp.where((lane & 16) == 0, pltpu.roll(x, w - 16, 1), pltpu.roll(x, 16, 1))
    return x * cos + partner * sin


def _inproj_kernel(*refs, rope):
    if rope:
        x_ref, sc_ref, sh_ref, w_ref, cos_ref, sin_ref, q_ref, kv_ref, hg_ref, gab_ref = refs
    else:
        x_ref, sc_ref, sh_ref, w_ref, q_ref, kv_ref, hg_ref, gab_ref = refs
    u = _layer_norm(x_ref[0]) * (1.0 + sc_ref[0]) + sh_ref[0]
    ub = u.astype(BF16)
    kv0 = ATTN_Q_DIM
    v0 = kv0 + ATTN_KV_DIM
    hg0 = v0 + ATTN_KV_DIM
    gab0 = hg0 + 5 * HG_DIM
    q = _dot(ub, w_ref[:, 0:kv0])
    k = _dot(ub, w_ref[:, kv0:v0])
    if rope:
        q = _rope(q, cos_ref[...], sin_ref[...])
        k = _rope(k, cos_ref[:, 0:ATTN_KV_DIM], sin_ref[:, 0:ATTN_KV_DIM])
    q_ref[0] = q
    kv_ref[0, :, 0:ATTN_KV_DIM] = k
    kv_ref[0, :, ATTN_KV_DIM:] = _dot(ub, w_ref[:, v0:hg0])
    hg_ref[0] = _dot(ub, w_ref[:, hg0:gab0])
    gab_ref[0] = _dot(ub, w_ref[:, gab0:])


def _in_projection(x, sc, sh, w_in, layer, rope_tabs):
    b, l, d = x.shape
    tm = TOKEN_TILE
    per_batch = sc.shape[0] > 1
    mod_map = (lambda bi, i: (bi, 0, 0)) if per_batch else (lambda bi, i: (0, 0, 0))
    in_dim = w_in.shape[-1]
    in_specs = [
        pl.BlockSpec((1, tm, d), lambda bi, i: (bi, i, 0)),
        pl.BlockSpec((1, 1, d), mod_map),
        pl.BlockSpec((1, 1, d), mod_map),
        pl.BlockSpec((None, d, in_dim), lambda bi, i: (layer, 0, 0), pipeline_mode=pl.Buffered(1)),
    ]
    args = [x, sc, sh, w_in]
    if rope_tabs is not None:
        in_specs += [pl.BlockSpec((tm, ATTN_Q_DIM), lambda bi, i: (i, 0))] * 2
        args += list(rope_tabs)
    widths = (ATTN_Q_DIM, 2 * ATTN_KV_DIM, 5 * HG_DIM, 2 * d)
    return pl.pallas_call(
        functools.partial(_inproj_kernel, rope=rope_tabs is not None),
        grid=(b, l // tm),
        in_specs=in_specs,
        out_specs=[pl.BlockSpec((1, tm, wd), lambda bi, i: (bi, i, 0)) for wd in widths],
        out_shape=[jax.ShapeDtypeStruct((b, l, wd), F32) for wd in widths],
        compiler_params=_cparams("parallel", "arbitrary"),
        name="in_projection",
    )(*args)


def _attn_kernel(*refs, layer, nb, windowed):
    if windowed:
        sink_ref, q_ref, kvp_ref, kvc_ref, kvn_ref, ckv_ref, o_ref = refs
        kvs = [ckv_ref[0], kvp_ref[0], kvc_ref[0], kvn_ref[0]]
    else:
        sink_ref, q_ref, ckv_ref, o_ref = refs
        kvs = [ckv_ref[0]]
    q = q_ref[0]
    tq = q.shape[0]
    lc = kvs[0].shape[0]
    rep = ATTN_HEADS // ATTN_KV_HEADS
    scale = HEAD_DIM ** -0.5
    if windowed:
        n = pl.program_id(1)
        span = 3 * WBLOCK
        ti = lax.broadcasted_iota(jnp.int32, (tq, span), 0)
        sj = lax.broadcasted_iota(jnp.int32, (tq, span), 1)
        rel = sj - WBLOCK - ti
        ok = (rel <= WINDOW) & (rel >= -WINDOW)
        ok = ok & ((sj >= WBLOCK) | (n > 0)) & ((sj < 2 * WBLOCK) | (n < nb - 1))
        mask = jnp.concatenate([jnp.ones((tq, lc), jnp.bool_), ok], axis=1)
        mask = jnp.concatenate([mask] * rep, axis=0)
    outs = []
    for g in range(ATTN_KV_HEADS):
        kk = jnp.concatenate([t[:, g * HEAD_DIM:(g + 1) * HEAD_DIM] for t in kvs], axis=0).astype(BF16)
        v0 = ATTN_KV_DIM + g * HEAD_DIM
        vv = jnp.concatenate([t[:, v0:v0 + HEAD_DIM] for t in kvs], axis=0).astype(BF16)
        heads = range(g * rep, (g + 1) * rep)
        qg = jnp.concatenate([q[:, h * HEAD_DIM:(h + 1) * HEAD_DIM] for h in heads], axis=0).astype(BF16)
        s = _dot_nt(qg, kk) * scale
        if windowed:
            s = jnp.where(mask, s, -jnp.inf)
        sink = jnp.concatenate([jnp.full((tq, 1), sink_ref[layer, h], F32) for h in heads], axis=0)
        m = jnp.maximum(jnp.max(s, axis=1, keepdims=True), sink)
        e = jnp.exp(s - m)
        den = jnp.exp(sink - m) + jnp.sum(e, axis=1, keepdims=True)
        o = _dot(e.astype(BF16), vv) / den
        outs += [o[r * tq:(r + 1) * tq] for r in range(rep)]
    o_ref[0] = jnp.concatenate(outs, axis=1).astype(BF16)


def _attention(q, kv, ckv, sink, layer):
    b, l, _ = q.shape
    lc = ckv.shape[1]
    nb = l // WBLOCK
    windowed = kv is not None
    smem = pl.BlockSpec(memory_space=pltpu.SMEM)
    qspec = pl.BlockSpec((1, WBLOCK, ATTN_Q_DIM), lambda bi, n: (bi, n, 0))
    cspec = pl.BlockSpec((1, lc, 2 * ATTN_KV_DIM), lambda bi, n: (bi, 0, 0))
    if windowed:
        kvw = 2 * ATTN_KV_DIM
        in_specs = [
            smem, qspec,
            pl.BlockSpec((1, WBLOCK, kvw), lambda bi, n: (bi, jnp.maximum(n - 1, 0), 0)),
            pl.BlockSpec((1, WBLOCK, kvw), lambda bi, n: (bi, n, 0)),
            pl.BlockSpec((1, WBLOCK, kvw), lambda bi, n: (bi, jnp.minimum(n + 1, nb - 1), 0)),
            cspec,
        ]
        args = (sink, q, kv, kv, kv, ckv)
    else:
        in_specs = [smem, qspec, cspec]
        args = (sink, q, ckv)
    return pl.pallas_call(
        functools.partial(_attn_kernel, layer=layer, nb=nb, windowed=windowed),
        grid=(b, nb),
        in_specs=in_specs,
        out_specs=pl.BlockSpec((1, WBLOCK, ATTN_Q_DIM), lambda bi, n: (bi, n, 0)),
        out_shape=jax.ShapeDtypeStruct((b, l, ATTN_Q_DIM), BF16),
        compiler_params=_cparams("parallel", "arbitrary"),
        name="window_attention" if windowed else "context_attention",
    )(*args)


def _block_ref_rows(cum, blk, r):
    c, w = cum.shape
    if blk % 8 == 0:
        x = cum.reshape(c // blk, blk, w)
        return jnp.broadcast_to(x[:, r:r + 1, :], x.shape).reshape(c, w)
    pos = lax.broadcasted_iota(jnp.int32, cum.shape, 0) & (blk - 1)
    out = cum
    for p in range(blk):
        delta = r - p
        if delta == 0:
            continue
        out = jnp.where(pos == p, pltpu.roll(cum, (-delta) % c, 0), out)
    return out


def _lower_bound(lb_ref, layer):
    logits = lb_ref[...]
    m = jnp.max(logits, axis=0, keepdims=True)
    ex = jnp.exp(logits - m)
    p = ex / jnp.sum(ex, axis=0, keepdims=True)
    lb = jnp.zeros_like(p[0:1])
    for j in range(1, layer + 1):
        lb = lb + p[j:j + 1]
    return lb


def _hgrn_kernel(*refs, layer, has_init):
    if has_init:
        (lbf_ref, lbb_ref, tri_ref, lm_ref, qf_ref, zf_ref, vf_ref, qb_ref, zb_ref, vb_ref, s0_ref,
         of_ref, ob_ref, s_ref) = refs
    else:
        (lbf_ref, lbb_ref, tri_ref, lm_ref, qf_ref, zf_ref, vf_ref, qb_ref, zb_ref, vb_ref,
         of_ref, ob_ref, s_ref) = refs

    @pl.when(pl.program_id(1) == 0)
    def _():
        if has_init:
            s_ref[...] = s0_ref[...]
        else:
            s_ref[...] = jnp.zeros(s_ref.shape, F32)

    tt = qf_ref.shape[1]
    c = tri_ref.shape[1]
    row = lax.broadcasted_iota(jnp.int32, (c, HG_DK), 0)
    levels = []
    hs = c // 2
    while hs >= 1:
        levels.append(hs)
        hs //= 2
    dirs = ((qf_ref, zf_ref, vf_ref, of_ref, lbf_ref), (qb_ref, zb_ref, vb_ref, ob_ref, lbb_ref))
    for d, (q_ref, z_ref, v_ref, o_ref, lb_ref) in enumerate(dirs):
        backward = d == 1
        lb_all = _lower_bound(lb_ref, layer)
        tri = tri_ref[d]
        q_halves = [((row & (2 * hs - 1)) >= hs) != backward for hs in levels]
        for h in range(HG_HEADS):
            sl = slice(h * HG_DK, (h + 1) * HG_DK)
            lb = lb_all[:, sl]
            z = z_ref[0, :, sl]
            q_all = _silu(q_ref[0, :, sl])
            v_all = v_ref[0, :, sl]
            log_sig = jnp.minimum(z, 0.0) - jnp.log1p(jnp.exp(-jnp.abs(z)))
            a = jnp.log(lb)
            bb = jnp.log1p(-lb) + log_sig
            g_all = jnp.maximum(a, bb) + jnp.log1p(jnp.exp(-jnp.abs(a - bb)))
            k_all = (1.0 - lb) * _sigmoid(-z)
            st = s_ref[0, d, h]
            chunks = range(tt // c)
            for ci in (reversed(chunks) if backward else chunks):
                rows = slice(ci * c, (ci + 1) * c)
                q, kk, v = q_all[rows], k_all[rows], v_all[rows]
                qb, kb, vb = q.astype(BF16), kk.astype(BF16), v.astype(BF16)
                g3 = _split3(g_all[rows])
                cum = _dot(tri, g3[0]) + _dot(tri, g3[1]) + _dot(tri, g3[2])
                o = _dot_nt((q * jnp.exp(cum)).astype(BF16), st.astype(BF16))
                o = o + jnp.sum(q * kk, axis=1, keepdims=True) * v
                scores = jnp.zeros((c, c), F32)
                for li, hs in enumerate(levels):
                    blk = 2 * hs
                    dd = cum - _block_ref_rows(cum, blk, hs if backward else hs - 1)
                    e = jnp.exp(-jnp.abs(dd)).astype(BF16)
                    x = jnp.where(q_halves[li], qb, kb) * e
                    scores = scores + _dot_nt(x, x) * lm_ref[d, li]
                o = o + _dot(scores.astype(BF16), vb)
                o_ref[0, rows, sl] = o
                tot = cum[0:1] if backward else cum[c - 1:c]
                kdec = kk * jnp.exp(tot - cum)
                st = st * jnp.exp(tot) + _dot_tn(vb, kdec.astype(BF16))
            s_ref[0, d, h] = st


def _hgrn_scan(hgp, lb_fw, lb_bw, tri, lmask, s0, layer):
    b, l, _ = hgp.shape
    tt = TOKEN_TILE
    ns = l // tt
    has_init = s0 is not None
    full2 = lambda bi, i: (0, 0)
    fwd = lambda j: pl.BlockSpec((1, tt, HG_DIM), lambda bi, i: (bi, i, j))
    bwd = lambda j: pl.BlockSpec((1, tt, HG_DIM), lambda bi, i: (bi, ns - 1 - i, j))
    sspec = pl.BlockSpec((1, 2, HG_HEADS, HG_DV, HG_DK), lambda bi, i: (bi, 0, 0, 0, 0))
    in_specs = [
        pl.BlockSpec(lb_fw.shape, full2), pl.BlockSpec(lb_bw.shape, full2),
        pl.BlockSpec(tri.shape, lambda bi, i: (0, 0, 0)),
        pl.BlockSpec(lmask.shape, lambda bi, i: (0, 0, 0, 0)),
        fwd(0), fwd(1), fwd(3), bwd(0), bwd(2), bwd(3),
    ]
    args = [lb_fw, lb_bw, tri, lmask, hgp, hgp, hgp, hgp, hgp, hgp]
    if has_init:
        in_specs.append(sspec)
        args.append(s0)
    return pl.pallas_call(
        functools.partial(_hgrn_kernel, layer=layer, has_init=has_init),
        grid=(b, ns),
        in_specs=in_specs,
        out_specs=[
            pl.BlockSpec((1, tt, HG_DIM), lambda bi, i: (bi, i, 0)),
            pl.BlockSpec((1, tt, HG_DIM), lambda bi, i: (bi, ns - 1 - i, 0)),
            sspec,
        ],
        out_shape=[
            jax.ShapeDtypeStruct((b, l, HG_DIM), F32),
            jax.ShapeDtypeStruct((b, l, HG_DIM), F32),
            jax.ShapeDtypeStruct((b, 2, HG_HEADS, HG_DV, HG_DK), F32),
        ],
        compiler_params=_cparams("parallel", "arbitrary"),
        name="hgrn_scan",
    )(*args)


def _hgrn_constants(c):
    t = lax.broadcasted_iota(jnp.int32, (c, c), 0)
    s = lax.broadcasted_iota(jnp.int32, (c, c), 1)
    tri = jnp.stack([s <= t, s >= t]).astype(BF16)
    masks = []
    for backward in (False, True):
        per_level = []
        hs = c // 2
        while hs >= 1:
            blk = 2 * hs
            same = (t // blk) == (s // blk)
            t_late, s_late = (t % blk) >= hs, (s % blk) >= hs
            pair = (jnp.logical_not(t_late) & s_late) if backward else (t_late & jnp.logical_not(s_late))
            per_level.append(same & pair)
            hs //= 2
        masks.append(jnp.stack(per_level))
    return tri, jnp.stack(masks).astype(F32)


def _mixout_kernel(attn_ref, of_ref, ob_ref, gh_ref, ga_ref, gb_ref, x_ref, g1_ref, sc2_ref, sh2_ref,
                   ng_ref, wa_ref, wh_ref, wo_ref, lng_ref, lnb_ref, wrh_ref, wrl_ref,
                   xo_ref, u2_ref, aff_ref, *, alpha):
    o = of_ref[0] + ob_ref[0]
    parts = []
    for h in range(HG_HEADS):
        oh = o[:, h * HG_DV:(h + 1) * HG_DV]
        parts.append(oh * lax.rsqrt(jnp.mean(oh * oh, axis=-1, keepdims=True) + LN_EPS))
    hg = jnp.concatenate(parts, axis=1) * ng_ref[...] * _silu(gh_ref[0])
    ya = _dot(attn_ref[0], wa_ref[...])
    yh = _dot(hg.astype(BF16), wh_ref[...])
    mix = _sigmoid(ga_ref[0]) * ya + _sigmoid(gb_ref[0]) * yh
    y = _dot(mix.astype(BF16), wo_ref[...])
    xn = _layer_norm(alpha * x_ref[0] + g1_ref[0] * y) * lng_ref[...] + lnb_ref[...]
    xo_ref[0] = xn
    u2 = _layer_norm(xn) * (1.0 + sc2_ref[0]) + sh2_ref[0]
    u2_ref[0] = u2.astype(BF16)
    uh, ul = _split2(u2)
    lg = _dot_nt(wrh_ref[...], uh) + _dot_nt(wrh_ref[...], ul) + _dot_nt(wrl_ref[...], uh)
    ex = jnp.exp(lg - jnp.max(lg, axis=0, keepdims=True))
    aff_ref[0] = ex / jnp.sum(ex, axis=0, keepdims=True)


def _mixer_output(attn, o_f, o_b, hgp, gab, x, g1, sc2, sh2, norm_g, w_a, w_h, w_o, ln_g, ln_b,
                  wr_hi, wr_lo, layer, alpha):
    b, l, d = x.shape
    tm = TOKEN_TILE
    per_batch = g1.shape[0] > 1
    mod_map = (lambda bi, i: (bi, 0, 0)) if per_batch else (lambda bi, i: (0, 0, 0))
    tok = lambda w, j=0: pl.BlockSpec((1, tm, w), lambda bi, i: (bi, i, j))
    lay2 = lambda r, c: pl.BlockSpec((None, r, c), lambda bi, i: (layer, 0, 0))
    in_specs = [
        tok(ATTN_Q_DIM), tok(HG_DIM), tok(HG_DIM), tok(HG_DIM, 4), tok(d, 0), tok(d, 1), tok(d),
        pl.BlockSpec((1, 1, d), mod_map), pl.BlockSpec((1, 1, d), mod_map), pl.BlockSpec((1, 1, d), mod_map),
        lay2(1, HG_DIM), lay2(ATTN_Q_DIM, d), lay2(HG_DIM, d), lay2(d, d),
        pl.BlockSpec((None, None, 1, d), lambda bi, i: (layer, 0, 0, 0)),
        pl.BlockSpec((None, None, 1, d), lambda bi, i: (layer, 0, 0, 0)),
        lay2(N_EXPERTS, d), lay2(N_EXPERTS, d),
    ]
    return pl.pallas_call(
        functools.partial(_mixout_kernel, alpha=alpha),
        grid=(b, l // tm),
        in_specs=in_specs,
        out_specs=[tok(d), tok(d), pl.BlockSpec((1, N_EXPERTS, tm), lambda bi, i: (bi, 0, i))],
        out_shape=[
            jax.ShapeDtypeStruct((b, l, d), F32),
            jax.ShapeDtypeStruct((b, l, d), BF16),
            jax.ShapeDtypeStruct((b, N_EXPERTS, l), F32),
        ],
        compiler_params=_cparams("parallel", "arbitrary"),
        name="mixer_output",
    )(attn, o_f, o_b, hgp, gab, gab, x, g1, sc2, sh2, norm_g, w_a, w_h, w_o, ln_g, ln_b, wr_hi, wr_lo)


def _select_kernel(aff_ref, triu_ref, rk_ref, st_ref, *, cap):
    a = aff_ref[0]
    ne, n = a.shape
    ts = triu_ref.shape[0]
    bits = lax.bitcast_convert_type(a, jnp.int32)
    capf = jnp.float32(cap)

    def count_ge(v):
        return jnp.sum(jnp.where(bits >= v, 1.0, 0.0), axis=1, keepdims=True)

    def body(_, carry):
        lo, hi = carry
        mid = lo + ((hi - lo) >> 1)
        ok = count_ge(mid) >= capf
        return jnp.where(ok, mid, lo), jnp.where(ok, hi, mid)

    lo0 = jnp.zeros((ne, 1), jnp.int32)
    hi0 = jnp.full((ne, 1), 0x7F800000, jnp.int32)
    thr, _ = lax.fori_loop(0, 31, body, (lo0, hi0))
    gt = bits > thr
    eq = bits == thr
    need = capf - jnp.sum(jnp.where(gt, 1.0, 0.0), axis=1, keepdims=True)
    triu = triu_ref[...]
    lane = lax.broadcasted_iota(jnp.int32, st_ref.shape[1:], 1)
    eq_carry = jnp.zeros((ne, 1), F32)
    sel_carry = jnp.zeros((ne, 1), F32)
    starts = jnp.zeros(st_ref.shape[1:], F32)
    for t in range(n // ts):
        sl = slice(t * ts, (t + 1) * ts)
        eq_t = eq[:, sl]
        eq_pref = _dot(jnp.where(eq_t, 1.0, 0.0).astype(BF16), triu) + eq_carry
        sel_t = gt[:, sl] | (eq_t & (eq_pref <= need))
        sel_f = jnp.where(sel_t, 1.0, 0.0)
        sel_pref = _dot(sel_f.astype(BF16), triu) + sel_carry
        rk_ref[0, :, sl] = jnp.where(sel_t, sel_pref - 1.0, -1.0)
        starts = jnp.where(lane == t, sel_carry, starts)
        eq_carry = eq_pref[:, ts - 1:ts]
        sel_carry = sel_pref[:, ts - 1:ts]
    st_ref[0] = starts


def _select(aff, triu, cap):
    b, ne, n = aff.shape
    return pl.pallas_call(
        functools.partial(_select_kernel, cap=cap),
        grid=(b,),
        in_specs=[
            pl.BlockSpec((1, ne, n), lambda bi: (bi, 0, 0)),
            pl.BlockSpec(triu.shape, lambda bi: (0, 0)),
        ],
        out_specs=[
            pl.BlockSpec((1, ne, n), lambda bi: (bi, 0, 0)),
            pl.BlockSpec((1, ne, 128), lambda bi: (bi, 0, 0)),
        ],
        out_shape=[
            jax.ShapeDtypeStruct((b, ne, n), F32),
            jax.ShapeDtypeStruct((b, ne, 128), F32),
        ],
        compiler_params=_cparams("parallel"),
        name="expert_choice_select",
    )(aff, triu)


def _slot_window(cap, tile):
    return min(cap, tile + SLOT_ALIGN)


def _window_start(s0, cap, tile):
    return jnp.minimum((s0 // SLOT_ALIGN) * SLOT_ALIGN, cap - _slot_window(cap, tile))


def _ffn_kernel(st_ref, u_ref, rk_ref, aff_ref, wg_ref, wu_ref, wd_ref, y_ref, xs_ref,
                *, cap, ns, tiles):
    b, e, k = pl.program_id(0), pl.program_id(1), pl.program_id(2)
    w = _slot_window(cap, TOKEN_TILE)
    sub = TOKEN_TILE // ROUTE_TILE
    base = (b * N_EXPERTS + e) * (ns + 1)

    for j in range(tiles):
        t = (k * tiles + j) * sub
        s0, s1 = st_ref[base + t], st_ref[base + t + sub]
        a = pl.multiple_of(_window_start(s0, cap, TOKEN_TILE), SLOT_ALIGN)
        sl = slice(j * TOKEN_TILE, (j + 1) * TOKEN_TILE)
        slot = a + lax.broadcasted_iota(jnp.int32, (w, 1), 0)
        hit = slot.astype(F32) == rk_ref[0, 0, :, sl]
        mine = (slot >= s0) & (slot < s1)
        rows = _dot(jnp.where(hit, 1.0, 0.0).astype(BF16), u_ref[0, sl, :])
        wts = jnp.sum(jnp.where(hit, aff_ref[0, 0, :, sl], 0.0), axis=1, keepdims=True)
        vals = jnp.concatenate([rows, jnp.broadcast_to(wts, (w, 128))], axis=1)
        pltpu.store(xs_ref.at[pl.ds(a, w), :], vals, mask=jnp.broadcast_to(mine, vals.shape))

    @pl.when(k == pl.num_programs(2) - 1)
    def _():
        rc = min(cap, 256)
        d = u_ref.shape[2]
        for c in range(cap // rc):
            rows = slice(c * rc, (c + 1) * rc)
            xb = xs_ref[rows, 0:d].astype(BF16)
            h = _silu(_dot(xb, wg_ref[0])) * _dot(xb, wu_ref[0])
            y_ref[0, 0, rows, :] = (_dot(h.astype(BF16), wd_ref[0]) * xs_ref[rows, d:d + 1]).astype(BF16)


def _expert_ffn(starts, u2, rk4, aff4, w_gate, w_up, w_down, layer, cap):
    b, n, d = u2.shape
    ff = w_gate.shape[-1]
    ns = n // ROUTE_TILE
    tk = min(n, 1024)
    tiles = tk // TOKEN_TILE
    wspec = lambda r, c: pl.BlockSpec((None, 1, r, c), lambda bi, e, k, st: (layer, e, 0, 0))
    grid_spec = pltpu.PrefetchScalarGridSpec(
        num_scalar_prefetch=1,
        grid=(b, N_EXPERTS, n // tk),
        in_specs=[
            pl.BlockSpec((1, tk, d), lambda bi, e, k, st: (bi, k, 0)),
            pl.BlockSpec((1, 1, 1, tk), lambda bi, e, k, st: (bi, e, 0, k)),
            pl.BlockSpec((1, 1, 1, tk), lambda bi, e, k, st: (bi, e, 0, k)),
            wspec(d, ff), wspec(d, ff), wspec(ff, d),
        ],
        out_specs=pl.BlockSpec((1, 1, cap, d), lambda bi, e, k, st: (bi, e, 0, 0)),
        scratch_shapes=[pltpu.VMEM((cap, d + 128), F32)],
    )
    return pl.pallas_call(
        functools.partial(_ffn_kernel, cap=cap, ns=ns, tiles=tiles),
        grid_spec=grid_spec,
        out_shape=jax.ShapeDtypeStruct((b, N_EXPERTS, cap, d), BF16),
        compiler_params=_cparams("parallel", "arbitrary", "arbitrary"),
        name="expert_ffn",
    )(starts, u2, rk4, aff4, w_gate, w_up, w_down)


def _combine_kernel(*refs, cap, ns, alpha):
    st_ref, rk_ref, x_ref, g2_ref, lng_ref, lnb_ref = refs[:6]
    y_refs = refs[6:6 + N_EXPERTS]
    o_ref = refs[6 + N_EXPERTS]
    b, t = pl.program_id(0), pl.program_id(1)
    w = _slot_window(cap, ROUTE_TILE)
    acc = jnp.zeros(x_ref.shape[1:], F32)
    for e in range(N_EXPERTS):
        a = _window_start(st_ref[(b * N_EXPERTS + e) * (ns + 1) + t], cap, ROUTE_TILE)
        slot = (a + lax.broadcasted_iota(jnp.int32, (w, 1), 0)).astype(F32)
        hit = slot == rk_ref[0, e:e + 1, :]
        acc = acc + _dot_tn(jnp.where(hit, 1.0, 0.0).astype(BF16), y_refs[e][...])
    xr = alpha * x_ref[0] + g2_ref[0] * acc
    o_ref[0] = _layer_norm(xr) * lng_ref[...] + lnb_ref[...]


def _combine(starts, rk, x, g2, ln_g, ln_b, y, layer, cap, alpha):
    b, n, d = x.shape
    tt = ROUTE_TILE
    ns = n // tt
    w = _slot_window(cap, tt)
    per_batch = g2.shape[0] > 1
    mod_map = (lambda bi, t, st: (bi, 0, 0)) if per_batch else (lambda bi, t, st: (0, 0, 0))

    def y_spec(e):
        def index_map(bi, t, st):
            be = bi * N_EXPERTS + e
            a = _window_start(st[be * (ns + 1) + t], cap, tt)
            return pl.multiple_of(be * cap + a, SLOT_ALIGN), 0
        return pl.BlockSpec((pl.Element(w), pl.Element(d)), index_map)

    y = y.reshape(b * N_EXPERTS * cap, d)
    grid_spec = pltpu.PrefetchScalarGridSpec(
        num_scalar_prefetch=1,
        grid=(b, ns),
        in_specs=[
            pl.BlockSpec((1, N_EXPERTS, tt), lambda bi, t, st: (bi, 0, t)),
            pl.BlockSpec((1, tt, d), lambda bi, t, st: (bi, t, 0)),
            pl.BlockSpec((1, 1, d), mod_map),
            pl.BlockSpec((None, None, 1, d), lambda bi, t, st: (layer, 1, 0, 0)),
            pl.BlockSpec((None, None, 1, d), lambda bi, t, st: (layer, 1, 0, 0)),
        ] + [y_spec(e) for e in range(N_EXPERTS)],
        out_specs=pl.BlockSpec((1, tt, d), lambda bi, t, st: (bi, t, 0)),
    )
    return pl.pallas_call(
        functools.partial(_combine_kernel, cap=cap, ns=ns, alpha=alpha),
        grid_spec=grid_spec,
        out_shape=jax.ShapeDtypeStruct((b, n, d), F32),
        compiler_params=_cparams("parallel", "arbitrary"),
        name="moe_combine",
    )(starts, rk, x, g2, ln_g, ln_b, *([y] * N_EXPERTS))


def _moe(x, u2, aff, g2, ln_g, ln_b, triu, w_gate, w_up, w_down, layer, alpha):
    b, n, _ = x.shape
    cap = CAPACITY_FACTOR * n // N_EXPERTS
    ns = n // ROUTE_TILE
    rk, st = _select(aff, triu, cap)
    starts = jnp.concatenate([st[:, :, :ns], jnp.full((b, N_EXPERTS, 1), cap, F32)], axis=2)
    starts = starts.astype(jnp.int32).reshape(-1)
    rk4 = rk.reshape(b, N_EXPERTS, 1, n)
    aff4 = aff.reshape(b, N_EXPERTS, 1, n)
    y = _expert_ffn(starts, u2, rk4, aff4, w_gate, w_up, w_down, layer, cap)
    return _combine(starts, rk, x, g2, ln_g, ln_b, y, layer, cap, alpha)


def _rope_tables(l):
    pos = jnp.arange(l)
    nf = HEAD_DIM // 4
    inv = ROPE_THETA ** (-jnp.arange(nf, dtype=F32) / nf)
    ang_r = (pos // GRID_W).astype(F32)[:, None] * inv[None, :]
    ang_c = (pos % GRID_W).astype(F32)[:, None] * inv[None, :]
    cos = jnp.concatenate([jnp.cos(ang_r)] * 2 + [jnp.cos(ang_c)] * 2, axis=1)
    sin = jnp.concatenate([-jnp.sin(ang_r), jnp.sin(ang_r), -jnp.sin(ang_c), jnp.sin(ang_c)], axis=1)
    return jnp.tile(cos, (1, ATTN_HEADS)), jnp.tile(sin, (1, ATTN_HEADS))


def kernel(x, c, ctx, c_ctx, w_mod, b_mod, w_in, attn_sink, hgrn_lb_fw, hgrn_lb_bw, hgrn_norm_g,
           w_branch_attn, w_branch_hgrn, w_out, w_router, w_gate, w_up, w_down, ln_g, ln_b):
    depth, d = w_mod.shape[0], x.shape[-1]
    b = x.shape[0]
    assert x.shape[1] % (4 * TOKEN_TILE) == 0 and ctx.shape[1] == TOKEN_TILE
    alpha = (2 * depth) ** 0.25
    mod_rows = -(-(b + 1) // 8) * 8
    c_all = jnp.concatenate([c, c_ctx[None, :], jnp.zeros((mod_rows - b - 1, d), F32)], axis=0)
    b_mod3 = b_mod[:, None, :]
    norm_g3 = hgrn_norm_g[:, None, :]
    ln_g4, ln_b4 = ln_g[:, :, None, :], ln_b[:, :, None, :]
    w_in_b = w_in.astype(BF16)
    w_a_b, w_h_b, w_o_b = w_branch_attn.astype(BF16), w_branch_hgrn.astype(BF16), w_out.astype(BF16)
    w_g_b, w_u_b, w_d_b = w_gate.astype(BF16), w_up.astype(BF16), w_down.astype(BF16)
    wr_t = jnp.swapaxes(w_router, 1, 2)
    wr_hi = wr_t.astype(BF16)
    wr_lo = (wr_t - wr_hi.astype(F32)).astype(BF16)
    rope_tabs = _rope_tables(x.shape[1])
    tri, lmask = _hgrn_constants(HG_CHUNK)
    r_i = lax.broadcasted_iota(jnp.int32, (ROUTE_TILE, ROUTE_TILE), 0)
    c_i = lax.broadcasted_iota(jnp.int32, (ROUTE_TILE, ROUTE_TILE), 1)
    triu = (r_i <= c_i).astype(BF16)

    xc = ctx
    for l in range(depth):
        need_ctx = l < depth - 1
        mod = _modulation(c_all, w_mod, b_mod3, l)
        sh1, sc1, g1, sh2, sc2, g2 = [mod[:b, i * d:(i + 1) * d][:, None, :] for i in range(N_MOD)]
        sh1c, sc1c, g1c, sh2c, sc2c, g2c = [mod[b:b + 1, i * d:(i + 1) * d][:, None, :] for i in range(N_MOD)]
        cq, ckv, chg, cgab = _in_projection(xc, sc1c, sh1c, w_in_b, l, None)
        oc_f, oc_b, s_ctx = _hgrn_scan(chg, hgrn_lb_fw, hgrn_lb_bw, tri, lmask, None, l)
        q, kv, hgp, gab = _in_projection(x, sc1, sh1, w_in_b, l, rope_tabs)
        o_f, o_b, _ = _hgrn_scan(hgp, hgrn_lb_fw, hgrn_lb_bw, tri, lmask, s_ctx, l)
        attn = _attention(q, kv, ckv, attn_sink, l)
        x1, u2, aff = _mixer_output(attn, o_f, o_b, hgp, gab, x, g1, sc2, sh2, norm_g3, w_a_b, w_h_b, w_o_b,
                                    ln_g4, ln_b4, wr_hi, wr_lo, l, alpha)
        x = _moe(x1, u2, aff, g2, ln_g4, ln_b4, triu, w_g_b, w_u_b, w_d_b, l, alpha)
        if need_ctx:
            attn_c = _attention(cq, None, ckv, attn_sink, l)
            xc1, uc2, affc = _mixer_output(attn_c, oc_f, oc_b, chg, cgab, xc, g1c, sc2c, sh2c, norm_g3,
                                           w_a_b, w_h_b, w_o_b, ln_g4, ln_b4, wr_hi, wr_lo, l, alpha)
            xc = _moe(xc1, uc2, affc, g2c, ln_g4, ln_b4, triu, w_g_b, w_u_b, w_d_b, l, alpha)
    return x
```

```python
import functools

import jax
import jax.numpy as jnp
from jax import lax
from jax.experimental import pallas as pl
from jax.experimental.pallas import tpu as pltpu

F32 = jnp.float32
BF16 = jnp.bfloat16

GRID_W = 64
ATTN_HEADS = 8
ATTN_KV_HEADS = 2
HEAD_DIM = 64
WINDOW = 128
WBLOCK = 128
ROPE_THETA = 10000.0
HG_HEADS = 4
HG_DK = 128
HG_DV = 128
N_EXPERTS = 16
CAPACITY_FACTOR = 2
N_MOD = 6
LN_EPS = 1e-6
ATTN_Q_DIM = ATTN_HEADS * HEAD_DIM
ATTN_KV_DIM = ATTN_KV_HEADS * HEAD_DIM
HG_DIM = HG_HEADS * HG_DK

TOKEN_TILE = 256
HG_CHUNK = 128
ROUTE_TILE = 128
SLOT_ALIGN = 16
VMEM_LIMIT_BYTES = 56 << 20


def _cparams(*sem):
    return pltpu.CompilerParams(dimension_semantics=sem, vmem_limit_bytes=VMEM_LIMIT_BYTES)


def _dot(a, b):
    return jnp.dot(a, b, preferred_element_type=F32)


def _dot_nt(a, b):
    return lax.dot_general(a, b, (((1,), (1,)), ((), ())), preferred_element_type=F32)


def _dot_tn(a, b):
    return lax.dot_general(a, b, (((0,), (0,)), ((), ())), preferred_element_type=F32)


def _split2(a):
    hi = a.astype(BF16)
    lo = (a - hi.astype(F32)).astype(BF16)
    return hi, lo


def _split3(a):
    hi = a.astype(BF16)
    r = a - hi.astype(F32)
    mid = r.astype(BF16)
    lo = (r - mid.astype(F32)).astype(BF16)
    return hi, mid, lo


def _layer_norm(x):
    mu = jnp.mean(x, axis=-1, keepdims=True)
    xc = x - mu
    var = jnp.mean(xc * xc, axis=-1, keepdims=True)
    return xc * lax.rsqrt(var + LN_EPS)


def _sigmoid(x):
    return 1.0 / (1.0 + jnp.exp(-x))


def _silu(x):
    return x * _sigmoid(x)


def _mod_kernel(c_ref, w_ref, b_ref, o_ref):
    a = _silu(c_ref[...])
    ah, al = _split2(a)
    wh, wl = _split2(w_ref[...])
    o_ref[...] = _dot(ah, wh) + _dot(ah, wl) + _dot(al, wh) + b_ref[...]


def _modulation(c_all, w_mod, b_mod, layer):
    rows, d = c_all.shape
    cols = w_mod.shape[-1]
    tn = 1024
    return pl.pallas_call(
        _mod_kernel,
        grid=(cols // tn,),
        in_specs=[
            pl.BlockSpec((rows, d), lambda j: (0, 0)),
            pl.BlockSpec((None, d, tn), lambda j: (layer, 0, j)),
            pl.BlockSpec((None, 1, tn), lambda j: (layer, 0, j)),
        ],
        out_specs=pl.BlockSpec((rows, tn), lambda j: (0, j)),
        out_shape=jax.ShapeDtypeStruct((rows, cols), F32),
        compiler_params=_cparams("arbitrary"),
        name="modulation",
    )(c_all, w_mod, b_mod)


def _rope(x, cos, sin):
    w = x.shape[1]
    lane = lax.broadcasted_iota(jnp.int32, x.shape, 1)
    partner = jnp.where((lane & 16) == 0, pltpu.roll(x, w - 16, 1), pltpu.roll(x, 16, 1))
    return x * cos + partner * sin


def _lower_bound(lb_ref, layer):
    logits = lb_ref[...]
    m = jnp.max(logits, axis=0, keepdims=True)
    ex = jnp.exp(logits - m)
    p = ex / jnp.sum(ex, axis=0, keepdims=True)
    lb = jnp.zeros_like(p[0:1])
    for j in range(1, layer + 1):
        lb = lb + p[j:j + 1]
    return lb


def _forget_gate(z, lb):
    log_sig = jnp.minimum(z, 0.0) - jnp.log1p(jnp.exp(-jnp.abs(z)))
    a = jnp.log(lb)
    bb = jnp.log1p(-lb) + log_sig
    log_f = jnp.maximum(a, bb) + jnp.log1p(jnp.exp(-jnp.abs(a - bb)))
    return log_f, (1.0 - lb) * _sigmoid(-z)


KV2_DIM = 2 * ATTN_KV_DIM
IN_Q0, IN_K0, IN_V0 = 0, ATTN_Q_DIM, ATTN_Q_DIM + KV2_DIM
IN_HG0 = IN_V0 + KV2_DIM
IN_GAB0 = IN_HG0 + 5 * HG_DIM
HGP_Q, HGP_GF, HGP_GB, HGP_KF, HGP_KB, HGP_V, HGP_OG = range(7)


def _inproj_kernel(*refs, rope, layer):
    if rope:
        x_ref, sc_ref, sh_ref, w_ref, lbf_ref, lbb_ref, cos_ref, sin_ref, q_ref, kv_ref, hg_ref, gab_ref = refs
    else:
        x_ref, sc_ref, sh_ref, w_ref, lbf_ref, lbb_ref, q_ref, kv_ref, hg_ref, gab_ref = refs
    u = _layer_norm(x_ref[0]) * (1.0 + sc_ref[0]) + sh_ref[0]
    ub = u.astype(BF16)
    q = _dot(ub, w_ref[:, IN_Q0:IN_K0])
    k = _dot(ub, w_ref[:, IN_K0:IN_V0])
    if rope:
        q = _rope(q, cos_ref[...], sin_ref[...])
        k = _rope(k, cos_ref[:, 0:KV2_DIM], sin_ref[:, 0:KV2_DIM])
    q_ref[0] = q
    kv_ref[0, :, 0:KV2_DIM] = k
    kv_ref[0, :, KV2_DIM:] = _dot(ub, w_ref[:, IN_V0:IN_HG0])

    def hg_cols(j):
        return _dot(ub, w_ref[:, IN_HG0 + j * HG_DIM:IN_HG0 + (j + 1) * HG_DIM])

    def put(j, val):
        hg_ref[0, :, j * HG_DIM:(j + 1) * HG_DIM] = val

    put(HGP_Q, _silu(hg_cols(0)))
    log_f, kk = _forget_gate(hg_cols(1), _lower_bound(lbf_ref, layer))
    put(HGP_GF, log_f)
    put(HGP_KF, kk)
    log_f, kk = _forget_gate(hg_cols(2), _lower_bound(lbb_ref, layer))
    put(HGP_GB, log_f)
    put(HGP_KB, kk)
    put(HGP_V, hg_cols(3))
    put(HGP_OG, _silu(hg_cols(4)))
    gab_ref[0] = _sigmoid(_dot(ub, w_ref[:, IN_GAB0:]))


def _in_projection(x, sc, sh, w_in, lb_fw, lb_bw, layer, rope_tabs):
    b, l, d = x.shape
    tm = TOKEN_TILE
    per_batch = sc.shape[0] > 1
    mod_map = (lambda bi, i: (bi, 0, 0)) if per_batch else (lambda bi, i: (0, 0, 0))
    in_dim = w_in.shape[-1]
    in_specs = [
        pl.BlockSpec((1, tm, d), lambda bi, i: (bi, i, 0)),
        pl.BlockSpec((1, 1, d), mod_map),
        pl.BlockSpec((1, 1, d), mod_map),
        pl.BlockSpec((None, d, in_dim), lambda bi, i: (layer, 0, 0), pipeline_mode=pl.Buffered(1)),
        pl.BlockSpec(lb_fw.shape, lambda bi, i: (0, 0)),
        pl.BlockSpec(lb_bw.shape, lambda bi, i: (0, 0)),
    ]
    args = [x, sc, sh, w_in, lb_fw, lb_bw]
    if rope_tabs is not None:
        in_specs += [pl.BlockSpec((tm, ATTN_Q_DIM), lambda bi, i: (i, 0))] * 2
        args += list(rope_tabs)
    widths = (ATTN_Q_DIM, 2 * KV2_DIM, 7 * HG_DIM, 2 * d)
    return pl.pallas_call(
        functools.partial(_inproj_kernel, rope=rope_tabs is not None, layer=layer),
        grid=(b, l // tm),
        in_specs=in_specs,
        out_specs=[pl.BlockSpec((1, tm, wd), lambda bi, i: (bi, i, 0)) for wd in widths],
        out_shape=[jax.ShapeDtypeStruct((b, l, wd), F32) for wd in widths],
        compiler_params=_cparams("parallel", "arbitrary"),
        name="in_projection",
    )(*args)


def _relayout_w_in(w_in):
    q0, k0, v0, r0 = 0, ATTN_Q_DIM, ATTN_Q_DIM + ATTN_KV_DIM, ATTN_Q_DIM + 2 * ATTN_KV_DIM
    parts = [w_in[..., q0:k0]]
    for c0 in (k0, v0):
        for g in range(ATTN_KV_HEADS):
            head = w_in[..., c0 + g * HEAD_DIM:c0 + (g + 1) * HEAD_DIM]
            parts += [head, head]
    parts.append(w_in[..., r0:])
    return jnp.concatenate(parts, axis=-1).astype(BF16)


def _attn_kernel(*refs, layer, nb, windowed):
    if windowed:
        sink_ref, q_ref, kvp_ref, kvc_ref, kvn_ref, ckv_ref, o_ref = refs
        kvs = [ckv_ref[0], kvp_ref[0], kvc_ref[0], kvn_ref[0]]
    else:
        sink_ref, q_ref, ckv_ref, o_ref = refs
        kvs = [ckv_ref[0]]
    q = q_ref[0]
    tq = q.shape[0]
    lc = kvs[0].shape[0]
    rep = ATTN_HEADS // ATTN_KV_HEADS
    scale = HEAD_DIM ** -0.5
    tile = 2 * HEAD_DIM
    low_half = lax.broadcasted_iota(jnp.int32, (tq, tile), 1) < HEAD_DIM
    if windowed:
        n = pl.program_id(1)
        span = 3 * WBLOCK
        ti = lax.broadcasted_iota(jnp.int32, (tq, span), 0)
        sj = lax.broadcasted_iota(jnp.int32, (tq, span), 1)
        rel = sj - WBLOCK - ti
        ok = (rel <= WINDOW) & (rel >= -WINDOW)
        ok = ok & ((sj >= WBLOCK) | (n > 0)) & ((sj < 2 * WBLOCK) | (n < nb - 1))
        mask = jnp.concatenate([jnp.ones((tq, lc), jnp.bool_), ok], axis=1)
        mask = jnp.concatenate([mask] * rep, axis=0)
    tiles = []
    for g in range(ATTN_KV_HEADS):
        kk = jnp.concatenate([t[:, g * tile:(g + 1) * tile] for t in kvs], axis=0).astype(BF16)
        vv = jnp.concatenate([t[:, KV2_DIM + g * tile:KV2_DIM + (g + 1) * tile] for t in kvs], axis=0).astype(BF16)
        heads = range(g * rep, (g + 1) * rep)
        qs = []
        for h in heads:
            q2 = q[:, (h // 2) * tile:(h // 2 + 1) * tile]
            qs.append(jnp.where(low_half == (h % 2 == 0), q2, 0.0))
        qg = jnp.concatenate(qs, axis=0).astype(BF16)
        s = _dot_nt(qg, kk) * scale
        if windowed:
            s = jnp.where(mask, s, -jnp.inf)
        sink = jnp.concatenate([jnp.full((tq, 1), sink_ref[layer, h], F32) for h in heads], axis=0)
        m = jnp.maximum(jnp.max(s, axis=1, keepdims=True), sink)
        e = jnp.exp(s - m)
        den = jnp.exp(sink - m) + jnp.sum(e, axis=1, keepdims=True)
        o = _dot(e.astype(BF16), vv) / den
        for r in range(0, rep, 2):
            tiles.append(jnp.where(low_half, o[r * tq:(r + 1) * tq], o[(r + 1) * tq:(r + 2) * tq]))
    o_ref[0] = jnp.concatenate(tiles, axis=1).astype(BF16)


def _attention(q, kv, ckv, sink, layer):
    b, l, _ = q.shape
    lc = ckv.shape[1]
    nb = l // WBLOCK
    windowed = kv is not None
    smem = pl.BlockSpec(memory_space=pltpu.SMEM)
    qspec = pl.BlockSpec((1, WBLOCK, ATTN_Q_DIM), lambda bi, n: (bi, n, 0))
    cspec = pl.BlockSpec((1, lc, 2 * KV2_DIM), lambda bi, n: (bi, 0, 0))
    if windowed:
        kvw = 2 * KV2_DIM
        in_specs = [
            smem, qspec,
            pl.BlockSpec((1, WBLOCK, kvw), lambda bi, n: (bi, jnp.maximum(n - 1, 0), 0)),
            pl.BlockSpec((1, WBLOCK, kvw), lambda bi, n: (bi, n, 0)),
            pl.BlockSpec((1, WBLOCK, kvw), lambda bi, n: (bi, jnp.minimum(n + 1, nb - 1), 0)),
            cspec,
        ]
        args = (sink, q, kv, kv, kv, ckv)
    else:
        in_specs = [smem, qspec, cspec]
        args = (sink, q, ckv)
    return pl.pallas_call(
        functools.partial(_attn_kernel, layer=layer, nb=nb, windowed=windowed),
        grid=(b, nb),
        in_specs=in_specs,
        out_specs=pl.BlockSpec((1, WBLOCK, ATTN_Q_DIM), lambda bi, n: (bi, n, 0)),
        out_shape=jax.ShapeDtypeStruct((b, l, ATTN_Q_DIM), BF16),
        compiler_params=_cparams("parallel", "arbitrary"),
        name="window_attention" if windowed else "context_attention",
    )(*args)


def _block_ref_rows(cum, blk, r):
    c, w = cum.shape
    if blk % 8 == 0:
        x = cum.reshape(c // blk, blk, w)
        return jnp.broadcast_to(x[:, r:r + 1, :], x.shape).reshape(c, w)
    pos = lax.broadcasted_iota(jnp.int32, cum.shape, 0) & (blk - 1)
    out = cum
    for p in range(blk):
        delta = r - p
        if delta == 0:
            continue
        out = jnp.where(pos == p, pltpu.roll(cum, (-delta) % c, 0), out)
    return out


def _hgrn_kernel(*refs, has_init):
    if has_init:
        (tri_ref, lm_ref, qf_ref, gf_ref, kf_ref, vf_ref, qb_ref, gb_ref, kb_ref, vb_ref, s0_ref,
         of_ref, ob_ref, s_ref) = refs
    else:
        (tri_ref, lm_ref, qf_ref, gf_ref, kf_ref, vf_ref, qb_ref, gb_ref, kb_ref, vb_ref,
         of_ref, ob_ref, s_ref) = refs

    @pl.when(pl.program_id(1) == 0)
    def _():
        if has_init:
            s_ref[...] = s0_ref[...]
        else:
            s_ref[...] = jnp.zeros(s_ref.shape, F32)

    tt = qf_ref.shape[1]
    c = tri_ref.shape[1]
    row = lax.broadcasted_iota(jnp.int32, (c, HG_DK), 0)
    levels = []
    hs = c // 2
    while hs >= 1:
        levels.append(hs)
        hs //= 2
    dirs = ((qf_ref, gf_ref, kf_ref, vf_ref, of_ref), (qb_ref, gb_ref, kb_ref, vb_ref, ob_ref))
    for d, (q_ref, g_ref, k_ref, v_ref, o_ref) in enumerate(dirs):
        backward = d == 1
        tri = tri_ref[d]
        q_halves = [((row & (2 * hs - 1)) >= hs) != backward for hs in levels]
        for h in range(HG_HEADS):
            sl = slice(h * HG_DK, (h + 1) * HG_DK)
            q_all, g_all, k_all, v_all = q_ref[0, :, sl], g_ref[0, :, sl], k_ref[0, :, sl], v_ref[0, :, sl]
            st = s_ref[0, d, h]
            chunks = range(tt // c)
            for ci in (reversed(chunks) if backward else chunks):
                rows = slice(ci * c, (ci + 1) * c)
                q, kk, v = q_all[rows], k_all[rows], v_all[rows]
                qb, kb, vb = q.astype(BF16), kk.astype(BF16), v.astype(BF16)
                g3 = _split3(g_all[rows])
                cum = _dot(tri, g3[0]) + _dot(tri, g3[1]) + _dot(tri, g3[2])
                o = _dot_nt((q * jnp.exp(cum)).astype(BF16), st.astype(BF16))
                o = o + jnp.sum(q * kk, axis=1, keepdims=True) * v
                scores = jnp.zeros((c, c), F32)
                for li, hs in enumerate(levels):
                    blk = 2 * hs
                    dd = cum - _block_ref_rows(cum, blk, hs if backward else hs - 1)
                    e = jnp.exp(-jnp.abs(dd)).astype(BF16)
                    x = jnp.where(q_halves[li], qb, kb) * e
                    scores = scores + _dot_nt(x, x) * lm_ref[d, li]
                o = o + _dot(scores.astype(BF16), vb)
                o_ref[0, rows, sl] = o
                tot = cum[0:1] if backward else cum[c - 1:c]
                kdec = kk * jnp.exp(tot - cum)
                st = st * jnp.exp(tot) + _dot_tn(vb, kdec.astype(BF16))
            s_ref[0, d, h] = st


def _hgrn_scan(hgp, tri, lmask, s0):
    b, l, _ = hgp.shape
    tt = TOKEN_TILE
    ns = l // tt
    has_init = s0 is not None
    fwd = lambda j: pl.BlockSpec((1, tt, HG_DIM), lambda bi, i: (bi, i, j))
    bwd = lambda j: pl.BlockSpec((1, tt, HG_DIM), lambda bi, i: (bi, ns - 1 - i, j))
    sspec = pl.BlockSpec((1, 2, HG_HEADS, HG_DV, HG_DK), lambda bi, i: (bi, 0, 0, 0, 0))
    in_specs = [
        pl.BlockSpec(tri.shape, lambda bi, i: (0, 0, 0)),
        pl.BlockSpec(lmask.shape, lambda bi, i: (0, 0, 0, 0)),
        fwd(HGP_Q), fwd(HGP_GF), fwd(HGP_KF), fwd(HGP_V), bwd(HGP_Q), bwd(HGP_GB), bwd(HGP_KB), bwd(HGP_V),
    ]
    args = [tri, lmask] + [hgp] * 8
    if has_init:
        in_specs.append(sspec)
        args.append(s0)
    return pl.pallas_call(
        functools.partial(_hgrn_kernel, has_init=has_init),
        grid=(b, ns),
        in_specs=in_specs,
        out_specs=[
            pl.BlockSpec((1, tt, HG_DIM), lambda bi, i: (bi, i, 0)),
            pl.BlockSpec((1, tt, HG_DIM), lambda bi, i: (bi, ns - 1 - i, 0)),
            sspec,
        ],
        out_shape=[
            jax.ShapeDtypeStruct((b, l, HG_DIM), F32),
            jax.ShapeDtypeStruct((b, l, HG_DIM), F32),
            jax.ShapeDtypeStruct((b, 2, HG_HEADS, HG_DV, HG_DK), F32),
        ],
        compiler_params=_cparams("parallel", "arbitrary"),
        name="hgrn_scan",
    )(*args)


def _hgrn_constants(c):
    t = lax.broadcasted_iota(jnp.int32, (c, c), 0)
    s = lax.broadcasted_iota(jnp.int32, (c, c), 1)
    tri = jnp.stack([s <= t, s >= t]).astype(BF16)
    masks = []
    for backward in (False, True):
        per_level = []
        hs = c // 2
        while hs >= 1:
            blk = 2 * hs
            same = (t // blk) == (s // blk)
            t_late, s_late = (t % blk) >= hs, (s % blk) >= hs
            pair = (jnp.logical_not(t_late) & s_late) if backward else (t_late & jnp.logical_not(s_late))
            per_level.append(same & pair)
            hs //= 2
        masks.append(jnp.stack(per_level))
    return tri, jnp.stack(masks).astype(F32)


def _mixout_kernel(attn_ref, of_ref, ob_ref, gh_ref, ga_ref, gb_ref, x_ref, g1_ref, sc2_ref, sh2_ref,
                   ng_ref, wa_ref, wh_ref, wo_ref, lng_ref, lnb_ref, wrh_ref, wrl_ref,
                   xo_ref, u2_ref, aff_ref, *, alpha):
    o = of_ref[0] + ob_ref[0]
    parts = []
    for h in range(HG_HEADS):
        oh = o[:, h * HG_DV:(h + 1) * HG_DV]
        parts.append(oh * lax.rsqrt(jnp.mean(oh * oh, axis=-1, keepdims=True) + LN_EPS))
    hg = jnp.concatenate(parts, axis=1) * ng_ref[...] * gh_ref[0]
    ya = _dot(attn_ref[0], wa_ref[...])
    yh = _dot(hg.astype(BF16), wh_ref[...])
    mix = ga_ref[0] * ya + gb_ref[0] * yh
    y = _dot(mix.astype(BF16), wo_ref[...])
    xn = _layer_norm(alpha * x_ref[0] + g1_ref[0] * y) * lng_ref[...] + lnb_ref[...]
    xo_ref[0] = xn
    u2 = _layer_norm(xn) * (1.0 + sc2_ref[0]) + sh2_ref[0]
    u2_ref[0] = u2.astype(BF16)
    uh, ul = _split2(u2)
    lg = _dot_nt(wrh_ref[...], uh) + _dot_nt(wrh_ref[...], ul) + _dot_nt(wrl_ref[...], uh)
    ex = jnp.exp(lg - jnp.max(lg, axis=0, keepdims=True))
    aff_ref[0] = ex / jnp.sum(ex, axis=0, keepdims=True)


def _mixer_output(attn, o_f, o_b, hgp, gab, x, g1, sc2, sh2, norm_g, w_a, w_h, w_o, ln_g, ln_b,
                  wr_hi, wr_lo, layer, alpha):
    b, l, d = x.shape
    tm = TOKEN_TILE
    per_batch = g1.shape[0] > 1
    mod_map = (lambda bi, i: (bi, 0, 0)) if per_batch else (lambda bi, i: (0, 0, 0))
    tok = lambda w, j=0: pl.BlockSpec((1, tm, w), lambda bi, i: (bi, i, j))
    lay2 = lambda r, c: pl.BlockSpec((None, r, c), lambda bi, i: (layer, 0, 0))
    in_specs = [
        tok(ATTN_Q_DIM), tok(HG_DIM), tok(HG_DIM), tok(HG_DIM, HGP_OG), tok(d, 0), tok(d, 1), tok(d),
        pl.BlockSpec((1, 1, d), mod_map), pl.BlockSpec((1, 1, d), mod_map), pl.BlockSpec((1, 1, d), mod_map),
        lay2(1, HG_DIM), lay2(ATTN_Q_DIM, d), lay2(HG_DIM, d), lay2(d, d),
        pl.BlockSpec((None, None, 1, d), lambda bi, i: (layer, 0, 0, 0)),
        pl.BlockSpec((None, None, 1, d), lambda bi, i: (layer, 0, 0, 0)),
        lay2(N_EXPERTS, d), lay2(N_EXPERTS, d),
    ]
    return pl.pallas_call(
        functools.partial(_mixout_kernel, alpha=alpha),
        grid=(b, l // tm),
        in_specs=in_specs,
        out_specs=[tok(d), tok(d), pl.BlockSpec((1, N_EXPERTS, tm), lambda bi, i: (bi, 0, i))],
        out_shape=[
            jax.ShapeDtypeStruct((b, l, d), F32),
            jax.ShapeDtypeStruct((b, l, d), BF16),
            jax.ShapeDtypeStruct((b, N_EXPERTS, l), F32),
        ],
        compiler_params=_cparams("parallel", "arbitrary"),
        name="mixer_output",
    )(attn, o_f, o_b, hgp, gab, gab, x, g1, sc2, sh2, norm_g, w_a, w_h, w_o, ln_g, ln_b, wr_hi, wr_lo)


def _select_kernel(aff_ref, triu_ref, rk_ref, st_ref, *, cap):
    a = aff_ref[0]
    ne, n = a.shape
    ts = triu_ref.shape[0]
    bits = lax.bitcast_convert_type(a, jnp.int32)
    capf = jnp.float32(cap)

    def count_ge(v):
        return jnp.sum(jnp.where(bits >= v, 1.0, 0.0), axis=1, keepdims=True)

    def body(_, carry):
        lo, hi = carry
        mid = lo + ((hi - lo) >> 1)
        ok = count_ge(mid) >= capf
        return jnp.where(ok, mid, lo), jnp.where(ok, hi, mid)

    lo0 = jnp.zeros((ne, 1), jnp.int32)
    hi0 = jnp.full((ne, 1), 0x7F800000, jnp.int32)
    thr, _ = lax.fori_loop(0, 31, body, (lo0, hi0))
    gt = bits > thr
    eq = bits == thr
    need = capf - jnp.sum(jnp.where(gt, 1.0, 0.0), axis=1, keepdims=True)
    triu = triu_ref[...]
    lane = lax.broadcasted_iota(jnp.int32, st_ref.shape[1:], 1)
    eq_carry = jnp.zeros((ne, 1), F32)
    sel_carry = jnp.zeros((ne, 1), F32)
    starts = jnp.zeros(st_ref.shape[1:], F32)
    for t in range(n // ts):
        sl = slice(t * ts, (t + 1) * ts)
        eq_t = eq[:, sl]
        eq_pref = _dot(jnp.where(eq_t, 1.0, 0.0).astype(BF16), triu) + eq_carry
        sel_t = gt[:, sl] | (eq_t & (eq_pref <= need))
        sel_f = jnp.where(sel_t, 1.0, 0.0)
        sel_pref = _dot(sel_f.astype(BF16), triu) + sel_carry
        rk_ref[0, :, sl] = jnp.where(sel_t, sel_pref - 1.0, -1.0)
        starts = jnp.where(lane == t, sel_carry, starts)
        eq_carry = eq_pref[:, ts - 1:ts]
        sel_carry = sel_pref[:, ts - 1:ts]
    st_ref[0] = starts


def _select(aff, triu, cap):
    b, ne, n = aff.shape
    return pl.pallas_call(
        functools.partial(_select_kernel, cap=cap),
        grid=(b,),
        in_specs=[
            pl.BlockSpec((1, ne, n), lambda bi: (bi, 0, 0)),
            pl.BlockSpec(triu.shape, lambda bi: (0, 0)),
        ],
        out_specs=[
            pl.BlockSpec((1, ne, n), lambda bi: (bi, 0, 0)),
            pl.BlockSpec((1, ne, 128), lambda bi: (bi, 0, 0)),
        ],
        out_shape=[
            jax.ShapeDtypeStruct((b, ne, n), F32),
            jax.ShapeDtypeStruct((b, ne, 128), F32),
        ],
        compiler_params=_cparams("parallel"),
        name="expert_choice_select",
    )(aff, triu)


def _slot_window(cap, tile):
    return min(cap, tile + SLOT_ALIGN)


def _window_start(s0, cap, tile):
    return jnp.minimum((s0 // SLOT_ALIGN) * SLOT_ALIGN, cap - _slot_window(cap, tile))


def _ffn_kernel(st_ref, u_ref, rk_ref, aff_ref, wg_ref, wu_ref, wd_ref, y_ref, xs_ref, *, cap, ns, unroll):
    b, e = pl.program_id(0), pl.program_id(1)
    d = u_ref.shape[2]
    nt = rk_ref.shape[2]
    w = _slot_window(cap, TOKEN_TILE)
    sub = TOKEN_TILE // ROUTE_TILE
    base = (b * N_EXPERTS + e) * (ns + 1)

    def dispatch(t):
        s0, s1 = st_ref[base + t * sub], st_ref[base + (t + 1) * sub]
        a = pl.multiple_of(_window_start(s0, cap, TOKEN_TILE), SLOT_ALIGN)
        slot = a + lax.broadcasted_iota(jnp.int32, (w, 1), 0)
        hit = slot.astype(F32) == rk_ref[0, 0, pl.ds(t, 1), :]
        mine = (slot >= s0) & (slot < s1)
        tok0 = pl.multiple_of(t * TOKEN_TILE, TOKEN_TILE)
        rows = _dot(jnp.where(hit, 1.0, 0.0).astype(BF16), u_ref[0, pl.ds(tok0, TOKEN_TILE), :])
        wts = jnp.sum(jnp.where(hit, aff_ref[0, 0, pl.ds(t, 1), :], 0.0), axis=1, keepdims=True)
        vals = jnp.concatenate([rows, jnp.broadcast_to(wts, (w, 128))], axis=1)
        pltpu.store(xs_ref.at[pl.ds(a, w), :], vals, mask=jnp.broadcast_to(mine, vals.shape))

    def group(i, carry):
        for j in range(unroll):
            dispatch(i * unroll + j)
        return carry

    lax.fori_loop(0, nt // unroll, group, 0)

    rc = min(cap, 256)
    for c in range(cap // rc):
        rows = slice(c * rc, (c + 1) * rc)
        xb = xs_ref[rows, 0:d].astype(BF16)
        h = _silu(_dot(xb, wg_ref[0])) * _dot(xb, wu_ref[0])
        y_ref[0, 0, rows, :] = (_dot(h.astype(BF16), wd_ref[0]) * xs_ref[rows, d:d + 1]).astype(BF16)


def _expert_ffn(starts, u2, rk, aff, w_gate, w_up, w_down, layer, cap):
    b, n, d = u2.shape
    ff = w_gate.shape[-1]
    ns = n // ROUTE_TILE
    nt = n // TOKEN_TILE
    rk4 = rk.reshape(b, N_EXPERTS, nt, TOKEN_TILE)
    aff4 = aff.reshape(b, N_EXPERTS, nt, TOKEN_TILE)
    wspec = lambda r, c: pl.BlockSpec((None, 1, r, c), lambda bi, e, st: (layer, e, 0, 0))
    rspec = pl.BlockSpec((1, 1, nt, TOKEN_TILE), lambda bi, e, st: (bi, e, 0, 0))
    grid_spec = pltpu.PrefetchScalarGridSpec(
        num_scalar_prefetch=1,
        grid=(b, N_EXPERTS),
        in_specs=[
            pl.BlockSpec((1, n, d), lambda bi, e, st: (bi, 0, 0), pipeline_mode=pl.Buffered(1)),
            rspec, rspec,
            wspec(d, ff), wspec(d, ff), wspec(ff, d),
        ],
        out_specs=pl.BlockSpec((1, 1, cap, d), lambda bi, e, st: (bi, e, 0, 0)),
        scratch_shapes=[pltpu.VMEM((cap, d + 128), F32)],
    )
    return pl.pallas_call(
        functools.partial(_ffn_kernel, cap=cap, ns=ns, unroll=min(nt, 4)),
        grid_spec=grid_spec,
        out_shape=jax.ShapeDtypeStruct((b, N_EXPERTS, cap, d), BF16),
        compiler_params=_cparams("parallel", "arbitrary"),
        name="expert_ffn",
    )(starts, u2, rk4, aff4, w_gate, w_up, w_down)


def _combine_kernel(*refs, cap, ns, alpha):
    st_ref, rk_ref, x_ref, g2_ref, lng_ref, lnb_ref = refs[:6]
    y_refs = refs[6:6 + N_EXPERTS]
    o_ref = refs[6 + N_EXPERTS]
    b, t = pl.program_id(0), pl.program_id(1)
    w = _slot_window(cap, ROUTE_TILE)
    acc = jnp.zeros(x_ref.shape[1:], F32)
    for e in range(N_EXPERTS):
        a = _window_start(st_ref[(b * N_EXPERTS + e) * (ns + 1) + t], cap, ROUTE_TILE)
        slot = (a + lax.broadcasted_iota(jnp.int32, (w, 1), 0)).astype(F32)
        hit = slot == rk_ref[0, e:e + 1, :]
        acc = acc + _dot_tn(jnp.where(hit, 1.0, 0.0).astype(BF16), y_refs[e][...])
    xr = alpha * x_ref[0] + g2_ref[0] * acc
    o_ref[0] = _layer_norm(xr) * lng_ref[...] + lnb_ref[...]


def _combine(starts, rk, x, g2, ln_g, ln_b, y, layer, cap, alpha):
    b, n, d = x.shape
    tt = ROUTE_TILE
    ns = n // tt
    w = _slot_window(cap, tt)
    per_batch = g2.shape[0] > 1
    mod_map = (lambda bi, t, st: (bi, 0, 0)) if per_batch else (lambda bi, t, st: (0, 0, 0))

    def y_spec(e):
        def index_map(bi, t, st):
            be = bi * N_EXPERTS + e
            a = _window_start(st[be * (ns + 1) + t], cap, tt)
            return pl.multiple_of(be * cap + a, SLOT_ALIGN), 0
        return pl.BlockSpec((pl.Element(w), pl.Element(d)), index_map)

    y = y.reshape(b * N_EXPERTS * cap, d)
    grid_spec = pltpu.PrefetchScalarGridSpec(
        num_scalar_prefetch=1,
        grid=(b, ns),
        in_specs=[
            pl.BlockSpec((1, N_EXPERTS, tt), lambda bi, t, st: (bi, 0, t)),
            pl.BlockSpec((1, tt, d), lambda bi, t, st: (bi, t, 0)),
            pl.BlockSpec((1, 1, d), mod_map),
            pl.BlockSpec((None, None, 1, d), lambda bi, t, st: (layer, 1, 0, 0)),
            pl.BlockSpec((None, None, 1, d), lambda bi, t, st: (layer, 1, 0, 0)),
        ] + [y_spec(e) for e in range(N_EXPERTS)],
        out_specs=pl.BlockSpec((1, tt, d), lambda bi, t, st: (bi, t, 0)),
    )
    return pl.pallas_call(
        functools.partial(_combine_kernel, cap=cap, ns=ns, alpha=alpha),
        grid_spec=grid_spec,
        out_shape=jax.ShapeDtypeStruct((b, n, d), F32),
        compiler_params=_cparams("parallel", "arbitrary"),
        name="moe_combine",
    )(starts, rk, x, g2, ln_g, ln_b, *([y] * N_EXPERTS))


def _moe(x, u2, aff, g2, ln_g, ln_b, triu, w_gate, w_up, w_down, layer, alpha):
    b, n, _ = x.shape
    cap = CAPACITY_FACTOR * n // N_EXPERTS
    ns = n // ROUTE_TILE
    rk, st = _select(aff, triu, cap)
    starts = jnp.concatenate([st[:, :, :ns], jnp.full((b, N_EXPERTS, 1), cap, F32)], axis=2)
    starts = starts.astype(jnp.int32).reshape(-1)
    y = _expert_ffn(starts, u2, rk, aff, w_gate, w_up, w_down, layer, cap)
    return _combine(starts, rk, x, g2, ln_g, ln_b, y, layer, cap, alpha)


def _rope_tables(l):
    pos = jnp.arange(l)
    nf = HEAD_DIM // 4
    inv = ROPE_THETA ** (-jnp.arange(nf, dtype=F32) / nf)
    ang_r = (pos // GRID_W).astype(F32)[:, None] * inv[None, :]
    ang_c = (pos % GRID_W).astype(F32)[:, None] * inv[None, :]
    cos = jnp.concatenate([jnp.cos(ang_r)] * 2 + [jnp.cos(ang_c)] * 2, axis=1)
    sin = jnp.concatenate([-jnp.sin(ang_r), jnp.sin(ang_r), -jnp.sin(ang_c), jnp.sin(ang_c)], axis=1)
    return jnp.tile(cos, (1, ATTN_HEADS)), jnp.tile(sin, (1, ATTN_HEADS))


def kernel(x, c, ctx, c_ctx, w_mod, b_mod, w_in, attn_sink, hgrn_lb_fw, hgrn_lb_bw, hgrn_norm_g,
           w_branch_attn, w_branch_hgrn, w_out, w_router, w_gate, w_up, w_down, ln_g, ln_b):
    depth, d = w_mod.shape[0], x.shape[-1]
    b = x.shape[0]
    assert x.shape[1] % (4 * TOKEN_TILE) == 0 and ctx.shape[1] == TOKEN_TILE
    alpha = (2 * depth) ** 0.25
    mod_rows = -(-(b + 1) // 8) * 8
    c_all = jnp.concatenate([c, c_ctx[None, :], jnp.zeros((mod_rows - b - 1, d), F32)], axis=0)
    b_mod3 = b_mod[:, None, :]
    norm_g3 = hgrn_norm_g[:, None, :]
    ln_g4, ln_b4 = ln_g[:, :, None, :], ln_b[:, :, None, :]
    w_in_b = _relayout_w_in(w_in)
    w_a_b, w_h_b, w_o_b = w_branch_attn.astype(BF16), w_branch_hgrn.astype(BF16), w_out.astype(BF16)
    w_g_b, w_u_b, w_d_b = w_gate.astype(BF16), w_up.astype(BF16), w_down.astype(BF16)
    wr_t = jnp.swapaxes(w_router, 1, 2)
    wr_hi = wr_t.astype(BF16)
    wr_lo = (wr_t - wr_hi.astype(F32)).astype(BF16)
    rope_tabs = _rope_tables(x.shape[1])
    tri, lmask = _hgrn_constants(HG_CHUNK)
    r_i = lax.broadcasted_iota(jnp.int32, (ROUTE_TILE, ROUTE_TILE), 0)
    c_i = lax.broadcasted_iota(jnp.int32, (ROUTE_TILE, ROUTE_TILE), 1)
    triu = (r_i <= c_i).astype(BF16)

    xc = ctx
    for l in range(depth):
        need_ctx = l < depth - 1
        mod = _modulation(c_all, w_mod, b_mod3, l)
        sh1, sc1, g1, sh2, sc2, g2 = [mod[:b, i * d:(i + 1) * d][:, None, :] for i in range(N_MOD)]
        sh1c, sc1c, g1c, sh2c, sc2c, g2c = [mod[b:b + 1, i * d:(i + 1) * d][:, None, :] for i in range(N_MOD)]
        cq, ckv, chg, cgab = _in_projection(xc, sc1c, sh1c, w_in_b, hgrn_lb_fw, hgrn_lb_bw, l, None)
        oc_f, oc_b, s_ctx = _hgrn_scan(chg, tri, lmask, None)
        q, kv, hgp, gab = _in_projection(x, sc1, sh1, w_in_b, hgrn_lb_fw, hgrn_lb_bw, l, rope_tabs)
        o_f, o_b, _ = _hgrn_scan(hgp, tri, lmask, s_ctx)
        attn = _attention(q, kv, ckv, attn_sink, l)
        x1, u2, aff = _mixer_output(attn, o_f, o_b, hgp, gab, x, g1, sc2, sh2, norm_g3, w_a_b, w_h_b, w_o_b,
                                    ln_g4, ln_b4, wr_hi, wr_lo, l, alpha)
        x = _moe(x1, u2, aff, g2, ln_g4, ln_b4, triu, w_g_b, w_u_b, w_d_b, l, alpha)
        if need_ctx:
            attn_c = _attention(cq, None, ckv, attn_sink, l)
            xc1, uc2, affc = _mixer_output(attn_c, oc_f, oc_b, chg, cgab, xc, g1c, sc2c, sh2c, norm_g3,
                                           w_a_b, w_h_b, w_o_b, ln_g4, ln_b4, wr_hi, wr_lo, l, alpha)
            xc = _moe(xc1, uc2, affc, g2c, ln_g4, ln_b4, triu, w_g_b, w_u_b, w_d_b, l, alpha)
    return x
```

```python
import functools

import jax
import jax.numpy as jnp
from jax import lax
from jax.experimental import pallas as pl
from jax.experimental.pallas import tpu as pltpu

F32 = jnp.float32
BF16 = jnp.bfloat16

GRID_W = 64
ATTN_HEADS = 8
ATTN_KV_HEADS = 2
HEAD_DIM = 64
WINDOW = 128
WBLOCK = 128
ROPE_THETA = 10000.0
HG_HEADS = 4
HG_DK = 128
HG_DV = 128
N_EXPERTS = 16
CAPACITY_FACTOR = 2
N_MOD = 6
LN_EPS = 1e-6
ATTN_Q_DIM = ATTN_HEADS * HEAD_DIM
ATTN_KV_DIM = ATTN_KV_HEADS * HEAD_DIM
HG_DIM = HG_HEADS * HG_DK

TOKEN_TILE = 256
HG_CHUNK = 128
ATTN_STEP_BLOCKS = 2
ROUTE_TILE = 128
SLOT_ALIGN = 16
DISPATCH_SMALL_WINDOW = 128
ROUTER_LANES = 128
VMEM_LIMIT_BYTES = 56 << 20


def _cparams(*sem):
    return pltpu.CompilerParams(dimension_semantics=sem, vmem_limit_bytes=VMEM_LIMIT_BYTES)


def _dot(a, b):
    return jnp.dot(a, b, preferred_element_type=F32)


def _dot_nt(a, b):
    return lax.dot_general(a, b, (((1,), (1,)), ((), ())), preferred_element_type=F32)


def _dot_tn(a, b):
    return lax.dot_general(a, b, (((0,), (0,)), ((), ())), preferred_element_type=F32)


def _split2(a):
    hi = a.astype(BF16)
    lo = (a - hi.astype(F32)).astype(BF16)
    return hi, lo


def _split3(a):
    hi = a.astype(BF16)
    r = a - hi.astype(F32)
    mid = r.astype(BF16)
    lo = (r - mid.astype(F32)).astype(BF16)
    return hi, mid, lo


def _layer_norm(x):
    mu = jnp.mean(x, axis=-1, keepdims=True)
    xc = x - mu
    var = jnp.mean(xc * xc, axis=-1, keepdims=True)
    return xc * lax.rsqrt(var + LN_EPS)


def _sigmoid(x):
    return 0.5 * jnp.tanh(0.5 * x) + 0.5


def _silu(x):
    return x * _sigmoid(x)


def _mod_kernel(c_ref, w_ref, b_ref, o_ref):
    a = _silu(c_ref[...])
    ah, al = _split2(a)
    wh, wl = _split2(w_ref[...])
    o_ref[...] = _dot(ah, wh) + _dot(ah, wl) + _dot(al, wh) + b_ref[...]


def _modulation(c_all, w_mod, b_mod, layer):
    rows, d = c_all.shape
    cols = w_mod.shape[-1]
    tn = 1024
    return pl.pallas_call(
        _mod_kernel,
        grid=(cols // tn,),
        in_specs=[
            pl.BlockSpec((rows, d), lambda j: (0, 0)),
            pl.BlockSpec((None, d, tn), lambda j: (layer, 0, j)),
            pl.BlockSpec((None, 1, tn), lambda j: (layer, 0, j)),
        ],
        out_specs=pl.BlockSpec((rows, tn), lambda j: (0, j)),
        out_shape=jax.ShapeDtypeStruct((rows, cols), F32),
        compiler_params=_cparams("arbitrary"),
        name="modulation",
    )(c_all, w_mod, b_mod)


def _rope(x, cos, sin):
    w = x.shape[1]
    lane = lax.broadcasted_iota(jnp.int32, x.shape, 1)
    partner = jnp.where((lane & 16) == 0, pltpu.roll(x, w - 16, 1), pltpu.roll(x, 16, 1))
    return x * cos + partner * sin


def _lower_bound(lb_ref, layer):
    logits = lb_ref[...]
    m = jnp.max(logits, axis=0, keepdims=True)
    ex = jnp.exp(logits - m)
    p = ex / jnp.sum(ex, axis=0, keepdims=True)
    lb = jnp.zeros_like(p[0:1])
    for j in range(1, layer + 1):
        lb = lb + p[j:j + 1]
    return lb


def _forget_gate(z, lb):
    log_sig = jnp.minimum(z, 0.0) - jnp.log1p(jnp.exp(-jnp.abs(z)))
    a = jnp.log(lb)
    bb = jnp.log1p(-lb) + log_sig
    log_f = jnp.maximum(a, bb) + jnp.log1p(jnp.exp(-jnp.abs(a - bb)))
    return log_f, (1.0 - lb) * _sigmoid(-z)


KV2_DIM = 2 * ATTN_KV_DIM
IN_Q0, IN_K0, IN_V0 = 0, ATTN_Q_DIM, ATTN_Q_DIM + KV2_DIM
IN_HG0 = IN_V0 + KV2_DIM
IN_GAB0 = IN_HG0 + 5 * HG_DIM
HGP_Q, HGP_GF, HGP_GB, HGP_KF, HGP_KB, HGP_V, HGP_OG = range(7)
IN_SLAB = 256


def _inproj_kernel(*refs, rope, layer):
    if rope:
        x_ref, sc_ref, sh_ref, w_ref, lbf_ref, lbb_ref, cos_ref, sin_ref, q_ref, kv_ref, hg_ref, gab_ref = refs
    else:
        x_ref, sc_ref, sh_ref, w_ref, lbf_ref, lbb_ref, q_ref, kv_ref, hg_ref, gab_ref = refs
    u = _layer_norm(x_ref[0]) * (1.0 + sc_ref[0]) + sh_ref[0]
    ub = u.astype(BF16)
    sw = IN_SLAB
    for c in range(0, ATTN_Q_DIM, sw):
        q = _dot(ub, w_ref[:, IN_Q0 + c:IN_Q0 + c + sw])
        q_ref[0, :, c:c + sw] = _rope(q, cos_ref[:, c:c + sw], sin_ref[:, c:c + sw]) if rope else q
    for c in range(0, KV2_DIM, sw):
        k = _dot(ub, w_ref[:, IN_K0 + c:IN_K0 + c + sw])
        kv_ref[0, :, c:c + sw] = _rope(k, cos_ref[:, c:c + sw], sin_ref[:, c:c + sw]) if rope else k
        kv_ref[0, :, KV2_DIM + c:KV2_DIM + c + sw] = _dot(ub, w_ref[:, IN_V0 + c:IN_V0 + c + sw])
    lb_f, lb_b = _lower_bound(lbf_ref, layer), _lower_bound(lbb_ref, layer)
    for c in range(0, HG_DIM, sw):
        def cols(j):
            return _dot(ub, w_ref[:, IN_HG0 + j * HG_DIM + c:IN_HG0 + j * HG_DIM + c + sw])

        def put(j, val):
            hg_ref[0, :, j * HG_DIM + c:j * HG_DIM + c + sw] = val

        put(HGP_Q, _silu(cols(0)))
        log_f, kk = _forget_gate(cols(1), lb_f[:, c:c + sw])
        put(HGP_GF, log_f)
        put(HGP_KF, kk)
        log_f, kk = _forget_gate(cols(2), lb_b[:, c:c + sw])
        put(HGP_GB, log_f)
        put(HGP_KB, kk)
        put(HGP_V, cols(3))
        put(HGP_OG, _silu(cols(4)))
    for c in range(0, gab_ref.shape[2], sw):
        gab_ref[0, :, c:c + sw] = _sigmoid(_dot(ub, w_ref[:, IN_GAB0 + c:IN_GAB0 + c + sw]))


def _in_projection(x, sc, sh, w_in, lb_fw, lb_bw, layer, rope_tabs):
    b, l, d = x.shape
    tm = TOKEN_TILE
    per_batch = sc.shape[0] > 1
    mod_map = (lambda bi, i: (bi, 0, 0)) if per_batch else (lambda bi, i: (0, 0, 0))
    in_dim = w_in.shape[-1]
    in_specs = [
        pl.BlockSpec((1, tm, d), lambda bi, i: (bi, i, 0)),
        pl.BlockSpec((1, 1, d), mod_map),
        pl.BlockSpec((1, 1, d), mod_map),
        pl.BlockSpec((None, d, in_dim), lambda bi, i: (layer, 0, 0), pipeline_mode=pl.Buffered(1)),
        pl.BlockSpec(lb_fw.shape, lambda bi, i: (0, 0)),
        pl.BlockSpec(lb_bw.shape, lambda bi, i: (0, 0)),
    ]
    args = [x, sc, sh, w_in, lb_fw, lb_bw]
    if rope_tabs is not None:
        in_specs += [pl.BlockSpec((tm, ATTN_Q_DIM), lambda bi, i: (i, 0))] * 2
        args += list(rope_tabs)
    widths = (ATTN_Q_DIM, 2 * KV2_DIM, 7 * HG_DIM, 2 * d)
    return pl.pallas_call(
        functools.partial(_inproj_kernel, rope=rope_tabs is not None, layer=layer),
        grid=(b, l // tm),
        in_specs=in_specs,
        out_specs=[pl.BlockSpec((1, tm, wd), lambda bi, i: (bi, i, 0)) for wd in widths],
        out_shape=[jax.ShapeDtypeStruct((b, l, wd), F32) for wd in widths],
        compiler_params=_cparams("parallel", "arbitrary"),
        name="in_projection",
    )(*args)


def _relayout_w_in(w_in):
    q0, k0, v0, r0 = 0, ATTN_Q_DIM, ATTN_Q_DIM + ATTN_KV_DIM, ATTN_Q_DIM + 2 * ATTN_KV_DIM
    parts = [w_in[..., q0:k0]]
    for c0 in (k0, v0):
        for g in range(ATTN_KV_HEADS):
            head = w_in[..., c0 + g * HEAD_DIM:c0 + (g + 1) * HEAD_DIM]
            parts += [head, head]
    parts.append(w_in[..., r0:])
    return jnp.concatenate(parts, axis=-1).astype(BF16)


def _attend(q, kvs, sinks, ok):
    tq = q.shape[0]
    lc = kvs[0].shape[0]
    rep = ATTN_HEADS // ATTN_KV_HEADS
    tile = 2 * HEAD_DIM
    low_half = lax.broadcasted_iota(jnp.int32, (tq, tile), 1) < HEAD_DIM
    q = q * HEAD_DIM ** -0.5
    if ok is not None:
        mask = jnp.concatenate([jnp.ones((tq, lc), jnp.bool_), ok], axis=1)
        mask = jnp.concatenate([mask] * rep, axis=0)
    tiles = []
    for g in range(ATTN_KV_HEADS):
        kk = jnp.concatenate([t[:, g * tile:(g + 1) * tile] for t in kvs], axis=0).astype(BF16)
        vv = jnp.concatenate([t[:, KV2_DIM + g * tile:KV2_DIM + (g + 1) * tile] for t in kvs], axis=0).astype(BF16)
        heads = range(g * rep, (g + 1) * rep)
        qs = []
        for h in heads:
            q2 = q[:, (h // 2) * tile:(h // 2 + 1) * tile]
            qs.append(jnp.where(low_half == (h % 2 == 0), q2, 0.0))
        s = _dot_nt(jnp.concatenate(qs, axis=0).astype(BF16), kk)
        if ok is not None:
            s = jnp.where(mask, s, -jnp.inf)
        sink = jnp.concatenate([jnp.full((tq, 1), sinks[h], F32) for h in heads], axis=0)
        m = jnp.maximum(jnp.max(s, axis=1, keepdims=True), sink)
        e = jnp.exp(s - m)
        den = jnp.exp(sink - m) + jnp.sum(e, axis=1, keepdims=True)
        o = _dot(e.astype(BF16), vv) / den
        for r in range(0, rep, 2):
            tiles.append(jnp.where(low_half, o[r * tq:(r + 1) * tq], o[(r + 1) * tq:(r + 2) * tq]))
    return jnp.concatenate(tiles, axis=1)


def _attn_kernel(*refs, layer, nsteps, windowed):
    sinks = [refs[0][layer, h] for h in range(ATTN_HEADS)]
    if not windowed:
        _, q_ref, ckv_ref, o_ref = refs
        o_ref[0] = _attend(q_ref[0], [ckv_ref[0]], sinks, None).astype(BF16)
        return
    _, q_ref, kvp_ref, kvc_ref, kvn_ref, ckv_ref, o_ref = refs
    n = pl.program_id(1)
    ti = lax.broadcasted_iota(jnp.int32, (WBLOCK, 3 * WBLOCK), 0)
    sj = lax.broadcasted_iota(jnp.int32, (WBLOCK, 3 * WBLOCK), 1)
    rel = sj - WBLOCK - ti
    band = (rel <= WINDOW) & (rel >= -WINDOW)
    blocks = [kvp_ref[0]] + [kvc_ref[0, i * WBLOCK:(i + 1) * WBLOCK] for i in range(ATTN_STEP_BLOCKS)] + [kvn_ref[0]]
    for i in range(ATTN_STEP_BLOCKS):
        ok = band
        if i == 0:
            ok = ok & ((sj >= WBLOCK) | (n > 0))
        if i == ATTN_STEP_BLOCKS - 1:
            ok = ok & ((sj < 2 * WBLOCK) | (n < nsteps - 1))
        rows = slice(i * WBLOCK, (i + 1) * WBLOCK)
        o_ref[0, rows, :] = _attend(q_ref[0, rows, :], [ckv_ref[0]] + blocks[i:i + 3], sinks, ok).astype(BF16)


def _attention(q, kv, ckv, sink, layer):
    b, l, _ = q.shape
    lc = ckv.shape[1]
    windowed = kv is not None
    smem = pl.BlockSpec(memory_space=pltpu.SMEM)
    cspec = pl.BlockSpec((1, lc, 2 * KV2_DIM), lambda bi, n: (bi, 0, 0))
    if windowed:
        kvw = 2 * KV2_DIM
        sb = ATTN_STEP_BLOCKS
        tq = sb * WBLOCK
        nb = l // WBLOCK
        in_specs = [
            smem, pl.BlockSpec((1, tq, ATTN_Q_DIM), lambda bi, n: (bi, n, 0)),
            pl.BlockSpec((1, WBLOCK, kvw), lambda bi, n: (bi, jnp.maximum(n * sb - 1, 0), 0)),
            pl.BlockSpec((1, tq, kvw), lambda bi, n: (bi, n, 0)),
            pl.BlockSpec((1, WBLOCK, kvw), lambda bi, n: (bi, jnp.minimum((n + 1) * sb, nb - 1), 0)),
            cspec,
        ]
        args = (sink, q, kv, kv, kv, ckv)
    else:
        tq = WBLOCK
        in_specs = [smem, pl.BlockSpec((1, tq, ATTN_Q_DIM), lambda bi, n: (bi, n, 0)), cspec]
        args = (sink, q, ckv)
    return pl.pallas_call(
        functools.partial(_attn_kernel, layer=layer, nsteps=l // tq, windowed=windowed),
        grid=(b, l // tq),
        in_specs=in_specs,
        out_specs=pl.BlockSpec((1, tq, ATTN_Q_DIM), lambda bi, n: (bi, n, 0)),
        out_shape=jax.ShapeDtypeStruct((b, l, ATTN_Q_DIM), BF16),
        compiler_params=_cparams("parallel", "arbitrary"),
        name="window_attention" if windowed else "context_attention",
    )(*args)


def _block_ref_rows(cum, blk, r):
    c, w = cum.shape
    if blk % 8 == 0:
        x = cum.reshape(c // blk, blk, w)
        return jnp.broadcast_to(x[:, r:r + 1, :], x.shape).reshape(c, w)
    pos = lax.broadcasted_iota(jnp.int32, cum.shape, 0) & (blk - 1)
    out = cum
    for p in range(blk):
        delta = r - p
        if delta == 0:
            continue
        out = jnp.where(pos == p, pltpu.roll(cum, (-delta) % c, 0), out)
    return out


def _hgrn_kernel(*refs, has_init):
    if has_init:
        (tri_ref, lm_ref, qf_ref, gf_ref, kf_ref, vf_ref, qb_ref, gb_ref, kb_ref, vb_ref, s0_ref,
         of_ref, ob_ref, s_ref) = refs
    else:
        (tri_ref, lm_ref, qf_ref, gf_ref, kf_ref, vf_ref, qb_ref, gb_ref, kb_ref, vb_ref,
         of_ref, ob_ref, s_ref) = refs

    @pl.when(pl.program_id(1) == 0)
    def _():
        if has_init:
            s_ref[...] = s0_ref[...]
        else:
            s_ref[...] = jnp.zeros(s_ref.shape, F32)

    tt = qf_ref.shape[1]
    c = tri_ref.shape[1]
    row = lax.broadcasted_iota(jnp.int32, (c, HG_DK), 0)
    levels = []
    hs = c // 2
    while hs >= 1:
        levels.append(hs)
        hs //= 2
    dirs = ((qf_ref, gf_ref, kf_ref, vf_ref, of_ref), (qb_ref, gb_ref, kb_ref, vb_ref, ob_ref))
    for d, (q_ref, g_ref, k_ref, v_ref, o_ref) in enumerate(dirs):
        backward = d == 1
        tri = tri_ref[d]
        q_halves = [((row & (2 * hs - 1)) >= hs) != backward for hs in levels]
        for h in range(HG_HEADS):
            sl = slice(h * HG_DK, (h + 1) * HG_DK)
            q_all, g_all, k_all, v_all = q_ref[0, :, sl], g_ref[0, :, sl], k_ref[0, :, sl], v_ref[0, :, sl]
            st = s_ref[0, d, h]
            chunks = range(tt // c)
            for ci in (reversed(chunks) if backward else chunks):
                rows = slice(ci * c, (ci + 1) * c)
                q, kk, v = q_all[rows], k_all[rows], v_all[rows]
                qb, kb, vb = q.astype(BF16), kk.astype(BF16), v.astype(BF16)
                g3 = _split3(g_all[rows])
                cum = _dot(tri, g3[0]) + _dot(tri, g3[1]) + _dot(tri, g3[2])
                o = _dot_nt((q * jnp.exp(cum)).astype(BF16), st.astype(BF16))
                o = o + jnp.sum(q * kk, axis=1, keepdims=True) * v
                scores = jnp.zeros((c, c), F32)
                for li, hs in enumerate(levels):
                    blk = 2 * hs
                    dd = cum - _block_ref_rows(cum, blk, hs if backward else hs - 1)
                    e = jnp.exp(-jnp.abs(dd)).astype(BF16)
                    x = jnp.where(q_halves[li], qb, kb) * e
                    scores = scores + _dot_nt(x, x) * lm_ref[d, li]
                o = o + _dot(scores.astype(BF16), vb)
                o_ref[0, rows, sl] = o
                tot = cum[0:1] if backward else cum[c - 1:c]
                kdec = kk * jnp.exp(tot - cum)
                st = st * jnp.exp(tot) + _dot_tn(vb, kdec.astype(BF16))
            s_ref[0, d, h] = st


def _hgrn_scan(hgp, tri, lmask, s0):
    b, l, _ = hgp.shape
    tt = TOKEN_TILE
    ns = l // tt
    has_init = s0 is not None
    fwd = lambda j: pl.BlockSpec((1, tt, HG_DIM), lambda bi, i: (bi, i, j))
    bwd = lambda j: pl.BlockSpec((1, tt, HG_DIM), lambda bi, i: (bi, ns - 1 - i, j))
    sspec = pl.BlockSpec((1, 2, HG_HEADS, HG_DV, HG_DK), lambda bi, i: (bi, 0, 0, 0, 0))
    in_specs = [
        pl.BlockSpec(tri.shape, lambda bi, i: (0, 0, 0)),
        pl.BlockSpec(lmask.shape, lambda bi, i: (0, 0, 0, 0)),
        fwd(HGP_Q), fwd(HGP_GF), fwd(HGP_KF), fwd(HGP_V), bwd(HGP_Q), bwd(HGP_GB), bwd(HGP_KB), bwd(HGP_V),
    ]
    args = [tri, lmask] + [hgp] * 8
    if has_init:
        in_specs.append(sspec)
        args.append(s0)
    return pl.pallas_call(
        functools.partial(_hgrn_kernel, has_init=has_init),
        grid=(b, ns),
        in_specs=in_specs,
        out_specs=[
            pl.BlockSpec((1, tt, HG_DIM), lambda bi, i: (bi, i, 0)),
            pl.BlockSpec((1, tt, HG_DIM), lambda bi, i: (bi, ns - 1 - i, 0)),
            sspec,
        ],
        out_shape=[
            jax.ShapeDtypeStruct((b, l, HG_DIM), F32),
            jax.ShapeDtypeStruct((b, l, HG_DIM), F32),
            jax.ShapeDtypeStruct((b, 2, HG_HEADS, HG_DV, HG_DK), F32),
        ],
        compiler_params=_cparams("parallel", "arbitrary"),
        name="hgrn_scan",
    )(*args)


def _hgrn_constants(c):
    t = lax.broadcasted_iota(jnp.int32, (c, c), 0)
    s = lax.broadcasted_iota(jnp.int32, (c, c), 1)
    tri = jnp.stack([s <= t, s >= t]).astype(BF16)
    masks = []
    for backward in (False, True):
        per_level = []
        hs = c // 2
        while hs >= 1:
            blk = 2 * hs
            same = (t // blk) == (s // blk)
            t_late, s_late = (t % blk) >= hs, (s % blk) >= hs
            pair = (jnp.logical_not(t_late) & s_late) if backward else (t_late & jnp.logical_not(s_late))
            per_level.append(same & pair)
            hs //= 2
        masks.append(jnp.stack(per_level))
    return tri, jnp.stack(masks).astype(F32)


def _mixout_kernel(attn_ref, of_ref, ob_ref, gh_ref, ga_ref, gb_ref, x_ref, g1_ref, sc2_ref, sh2_ref,
                   ng_ref, wa_ref, wh_ref, wo_ref, lng_ref, lnb_ref, wrh_ref, wrl_ref,
                   xo_ref, u2_ref, aff_ref, *, alpha):
    o = of_ref[0] + ob_ref[0]
    parts = []
    for h in range(HG_HEADS):
        oh = o[:, h * HG_DV:(h + 1) * HG_DV]
        parts.append(oh * lax.rsqrt(jnp.mean(oh * oh, axis=-1, keepdims=True) + LN_EPS))
    hg = jnp.concatenate(parts, axis=1) * ng_ref[...] * gh_ref[0]
    ya = _dot(attn_ref[0], wa_ref[...])
    yh = _dot(hg.astype(BF16), wh_ref[...])
    mix = ga_ref[0] * ya + gb_ref[0] * yh
    y = _dot(mix.astype(BF16), wo_ref[...])
    xn = _layer_norm(alpha * x_ref[0] + g1_ref[0] * y) * lng_ref[...] + lnb_ref[...]
    xo_ref[0] = xn
    u2 = _layer_norm(xn) * (1.0 + sc2_ref[0]) + sh2_ref[0]
    u2_ref[0] = u2.astype(BF16)
    uh, ul = _split2(u2)
    lg = _dot(uh, wrh_ref[...]) + _dot(ul, wrh_ref[...]) + _dot(uh, wrl_ref[...])
    lg = jnp.where(lax.broadcasted_iota(jnp.int32, lg.shape, 1) < N_EXPERTS, lg, -jnp.inf)
    ex = jnp.exp(lg - jnp.max(lg, axis=1, keepdims=True))
    aff = ex / jnp.sum(ex, axis=1, keepdims=True)
    aff_ref[0] = aff.T[0:N_EXPERTS, :]


def _mixer_output(attn, o_f, o_b, hgp, gab, x, g1, sc2, sh2, norm_g, w_a, w_h, w_o, ln_g, ln_b,
                  wr_hi, wr_lo, layer, alpha):
    b, l, d = x.shape
    tm = TOKEN_TILE
    per_batch = g1.shape[0] > 1
    mod_map = (lambda bi, i: (bi, 0, 0)) if per_batch else (lambda bi, i: (0, 0, 0))
    tok = lambda w, j=0: pl.BlockSpec((1, tm, w), lambda bi, i: (bi, i, j))
    lay2 = lambda r, c: pl.BlockSpec((None, r, c), lambda bi, i: (layer, 0, 0))
    in_specs = [
        tok(ATTN_Q_DIM), tok(HG_DIM), tok(HG_DIM), tok(HG_DIM, HGP_OG), tok(d, 0), tok(d, 1), tok(d),
        pl.BlockSpec((1, 1, d), mod_map), pl.BlockSpec((1, 1, d), mod_map), pl.BlockSpec((1, 1, d), mod_map),
        lay2(1, HG_DIM), lay2(ATTN_Q_DIM, d), lay2(HG_DIM, d), lay2(d, d),
        pl.BlockSpec((None, None, 1, d), lambda bi, i: (layer, 0, 0, 0)),
        pl.BlockSpec((None, None, 1, d), lambda bi, i: (layer, 0, 0, 0)),
        lay2(d, ROUTER_LANES), lay2(d, ROUTER_LANES),
    ]
    return pl.pallas_call(
        functools.partial(_mixout_kernel, alpha=alpha),
        grid=(b, l // tm),
        in_specs=in_specs,
        out_specs=[tok(d), tok(d), pl.BlockSpec((1, N_EXPERTS, tm), lambda bi, i: (bi, 0, i))],
        out_shape=[
            jax.ShapeDtypeStruct((b, l, d), F32),
            jax.ShapeDtypeStruct((b, l, d), BF16),
            jax.ShapeDtypeStruct((b, N_EXPERTS, l), F32),
        ],
        compiler_params=_cparams("parallel", "arbitrary"),
        name="mixer_output",
    )(attn, o_f, o_b, hgp, gab, gab, x, g1, sc2, sh2, norm_g, w_a, w_h, w_o, ln_g, ln_b, wr_hi, wr_lo)


def _select_kernel(aff_ref, triu_ref, rk_ref, rkt_ref, st_ref, *, cap):
    a = aff_ref[0]
    ne, n = a.shape
    ts = triu_ref.shape[0]
    bits = lax.bitcast_convert_type(a, jnp.int32)
    capf = jnp.float32(cap)

    def count_ge(v):
        return jnp.sum(jnp.where(bits >= v, 1.0, 0.0), axis=1, keepdims=True)

    def body(_, carry):
        lo, hi = carry
        mid = lo + ((hi - lo) >> 1)
        ok = count_ge(mid) >= capf
        return jnp.where(ok, mid, lo), jnp.where(ok, hi, mid)

    lo0 = jnp.zeros((ne, 1), jnp.int32)
    hi0 = jnp.full((ne, 1), 0x7F800000, jnp.int32)
    thr, _ = lax.fori_loop(0, 31, body, (lo0, hi0))
    gt = bits > thr
    eq = bits == thr
    need = capf - jnp.sum(jnp.where(gt, 1.0, 0.0), axis=1, keepdims=True)
    triu = triu_ref[...]
    lane = lax.broadcasted_iota(jnp.int32, st_ref.shape[1:], 1)
    eq_carry = jnp.zeros((ne, 1), F32)
    sel_carry = jnp.zeros((ne, 1), F32)
    starts = jnp.zeros(st_ref.shape[1:], F32)
    for t in range(n // ts):
        sl = slice(t * ts, (t + 1) * ts)
        eq_t = eq[:, sl]
        eq_pref = _dot(jnp.where(eq_t, 1.0, 0.0).astype(BF16), triu) + eq_carry
        sel_t = gt[:, sl] | (eq_t & (eq_pref <= need))
        sel_f = jnp.where(sel_t, 1.0, 0.0)
        sel_pref = _dot(sel_f.astype(BF16), triu) + sel_carry
        rk_t = jnp.where(sel_t, sel_pref - 1.0, -1.0)
        rk_ref[0, :, sl] = rk_t
        rkt_ref[0, sl, :] = rk_t.T
        starts = jnp.where(lane == t, sel_carry, starts)
        eq_carry = eq_pref[:, ts - 1:ts]
        sel_carry = sel_pref[:, ts - 1:ts]
    st_ref[0] = starts


def _select(aff, triu, cap):
    b, ne, n = aff.shape
    return pl.pallas_call(
        functools.partial(_select_kernel, cap=cap),
        grid=(b,),
        in_specs=[
            pl.BlockSpec((1, ne, n), lambda bi: (bi, 0, 0)),
            pl.BlockSpec(triu.shape, lambda bi: (0, 0)),
        ],
        out_specs=[
            pl.BlockSpec((1, ne, n), lambda bi: (bi, 0, 0)),
            pl.BlockSpec((1, n, ne), lambda bi: (bi, 0, 0)),
            pl.BlockSpec((1, ne, 128), lambda bi: (bi, 0, 0)),
        ],
        out_shape=[
            jax.ShapeDtypeStruct((b, ne, n), F32),
            jax.ShapeDtypeStruct((b, n, ne), F32),
            jax.ShapeDtypeStruct((b, ne, 128), F32),
        ],
        compiler_params=_cparams("parallel"),
        name="expert_choice_select",
    )(aff, triu)


def _slot_window(cap, tile):
    return min(cap, tile + SLOT_ALIGN)


def _aligned_start(s0, cap, w):
    return jnp.minimum((s0 // SLOT_ALIGN) * SLOT_ALIGN, cap - w)


def _window_start(s0, cap, tile):
    return _aligned_start(s0, cap, _slot_window(cap, tile))


def _ffn_kernel(st_ref, u_ref, rk_ref, aff_ref, wg_ref, wu_ref, wd_ref, y_ref, xs_ref, *, cap, ns, unroll):
    b, e = pl.program_id(0), pl.program_id(1)
    d = u_ref.shape[2]
    nt = rk_ref.shape[2]
    w_full = _slot_window(cap, TOKEN_TILE)
    w_small = min(cap, DISPATCH_SMALL_WINDOW)
    sub = TOKEN_TILE // ROUTE_TILE
    base = (b * N_EXPERTS + e) * (ns + 1)

    def bounds(t):
        return st_ref[base + t * sub], st_ref[base + (t + 1) * sub]

    def dispatch(t, w):
        s0, s1 = bounds(t)
        a = pl.multiple_of(_aligned_start(s0, cap, w), SLOT_ALIGN)
        slot = a + lax.broadcasted_iota(jnp.int32, (w, 1), 0)
        hit = slot.astype(F32) == rk_ref[0, 0, pl.ds(t, 1), :]
        mine = (slot >= s0) & (slot < s1)
        tok0 = pl.multiple_of(t * TOKEN_TILE, TOKEN_TILE)
        rows = _dot(jnp.where(hit, 1.0, 0.0).astype(BF16), u_ref[0, pl.ds(tok0, TOKEN_TILE), :])
        wts = jnp.sum(jnp.where(hit, aff_ref[0, 0, pl.ds(t, 1), :], 0.0), axis=1, keepdims=True)
        vals = jnp.concatenate([rows, jnp.broadcast_to(wts, (w, 128))], axis=1)
        pltpu.store(xs_ref.at[pl.ds(a, w), :], vals, mask=jnp.broadcast_to(mine, vals.shape))

    def run(w):
        def group(i, carry):
            for j in range(unroll):
                dispatch(i * unroll + j, w)
            return carry

        lax.fori_loop(0, nt // unroll, group, 0)

    if w_small < w_full:
        def fits(t, ok):
            s0, s1 = bounds(t)
            return ok & (s1 - _aligned_start(s0, cap, w_small) <= w_small)

        narrow = lax.fori_loop(0, nt, fits, True)
        pl.when(narrow)(lambda: run(w_small))
        pl.when(jnp.logical_not(narrow))(lambda: run(w_full))
    else:
        run(w_full)

    rc = min(cap, 256)
    for c in range(cap // rc):
        rows = slice(c * rc, (c + 1) * rc)
        xb = xs_ref[rows, 0:d].astype(BF16)
        h = _silu(_dot(xb, wg_ref[0])) * _dot(xb, wu_ref[0])
        y_ref[0, 0, rows, :] = (_dot(h.astype(BF16), wd_ref[0]) * xs_ref[rows, d:d + 1]).astype(BF16)


def _expert_ffn(starts, u2, rk, aff, w_gate, w_up, w_down, layer, cap):
    b, n, d = u2.shape
    ff = w_gate.shape[-1]
    ns = n // ROUTE_TILE
    nt = n // TOKEN_TILE
    rk4 = rk.reshape(b, N_EXPERTS, nt, TOKEN_TILE)
    aff4 = aff.reshape(b, N_EXPERTS, nt, TOKEN_TILE)
    wspec = lambda r, c: pl.BlockSpec((None, 1, r, c), lambda bi, e, st: (layer, e, 0, 0))
    rspec = pl.BlockSpec((1, 1, nt, TOKEN_TILE), lambda bi, e, st: (bi, e, 0, 0))
    grid_spec = pltpu.PrefetchScalarGridSpec(
        num_scalar_prefetch=1,
        grid=(b, N_EXPERTS),
        in_specs=[
            pl.BlockSpec((1, n, d), lambda bi, e, st: (bi, 0, 0), pipeline_mode=pl.Buffered(1)),
            rspec, rspec,
            wspec(d, ff), wspec(d, ff), wspec(ff, d),
        ],
        out_specs=pl.BlockSpec((1, 1, cap, d), lambda bi, e, st: (bi, e, 0, 0)),
        scratch_shapes=[pltpu.VMEM((cap, d + 128), F32)],
    )
    return pl.pallas_call(
        functools.partial(_ffn_kernel, cap=cap, ns=ns, unroll=min(nt, 4)),
        grid_spec=grid_spec,
        out_shape=jax.ShapeDtypeStruct((b, N_EXPERTS, cap, d), BF16),
        compiler_params=_cparams("parallel", "arbitrary"),
        name="expert_ffn",
    )(starts, u2, rk4, aff4, w_gate, w_up, w_down)


def _combine_kernel(*refs, cap, ns, alpha):
    st_ref, rkt_ref, x_ref, g2_ref, lng_ref, lnb_ref = refs[:6]
    y_refs = refs[6:6 + N_EXPERTS]
    o_ref = refs[6 + N_EXPERTS]
    b, t = pl.program_id(0), pl.program_id(1)
    sub = TOKEN_TILE // ROUTE_TILE
    w = _slot_window(cap, ROUTE_TILE)
    for half in range(sub):
        rows = slice(half * ROUTE_TILE, (half + 1) * ROUTE_TILE)
        acc = jnp.zeros((ROUTE_TILE, x_ref.shape[2]), F32)
        for e in range(N_EXPERTS):
            base = (b * N_EXPERTS + e) * (ns + 1) + t * sub
            a_tile = _window_start(st_ref[base], cap, TOKEN_TILE)
            a = _window_start(st_ref[base + half], cap, ROUTE_TILE)
            off = pl.multiple_of(a - a_tile, SLOT_ALIGN)
            slot = (a + lax.broadcasted_iota(jnp.int32, (1, w), 1)).astype(F32)
            hit = rkt_ref[0, rows, e:e + 1] == slot
            acc = acc + _dot(jnp.where(hit, 1.0, 0.0).astype(BF16), y_refs[e][pl.ds(off, w), :])
        xr = alpha * x_ref[0, rows, :] + g2_ref[0] * acc
        o_ref[0, rows, :] = _layer_norm(xr) * lng_ref[...] + lnb_ref[...]


def _combine(starts, rkt, x, g2, ln_g, ln_b, y, layer, cap, alpha):
    b, n, d = x.shape
    tt = TOKEN_TILE
    ns = n // ROUTE_TILE
    sub = TOKEN_TILE // ROUTE_TILE
    w = _slot_window(cap, tt)
    per_batch = g2.shape[0] > 1
    mod_map = (lambda bi, t, st: (bi, 0, 0)) if per_batch else (lambda bi, t, st: (0, 0, 0))

    def y_spec(e):
        def index_map(bi, t, st):
            be = bi * N_EXPERTS + e
            a = _window_start(st[be * (ns + 1) + t * sub], cap, tt)
            return pl.multiple_of(be * cap + a, SLOT_ALIGN), 0
        return pl.BlockSpec((pl.Element(w), pl.Element(d)), index_map)

    y = y.reshape(b * N_EXPERTS * cap, d)
    grid_spec = pltpu.PrefetchScalarGridSpec(
        num_scalar_prefetch=1,
        grid=(b, n // tt),
        in_specs=[
            pl.BlockSpec((1, tt, N_EXPERTS), lambda bi, t, st: (bi, t, 0)),
            pl.BlockSpec((1, tt, d), lambda bi, t, st: (bi, t, 0)),
            pl.BlockSpec((1, 1, d), mod_map),
            pl.BlockSpec((None, None, 1, d), lambda bi, t, st: (layer, 1, 0, 0)),
            pl.BlockSpec((None, None, 1, d), lambda bi, t, st: (layer, 1, 0, 0)),
        ] + [y_spec(e) for e in range(N_EXPERTS)],
        out_specs=pl.BlockSpec((1, tt, d), lambda bi, t, st: (bi, t, 0)),
    )
    return pl.pallas_call(
        functools.partial(_combine_kernel, cap=cap, ns=ns, alpha=alpha),
        grid_spec=grid_spec,
        out_shape=jax.ShapeDtypeStruct((b, n, d), F32),
        compiler_params=_cparams("parallel", "arbitrary"),
        name="moe_combine",
    )(starts, rkt, x, g2, ln_g, ln_b, *([y] * N_EXPERTS))


def _moe(x, u2, aff, g2, ln_g, ln_b, triu, w_gate, w_up, w_down, layer, alpha):
    b, n, _ = x.shape
    cap = CAPACITY_FACTOR * n // N_EXPERTS
    ns = n // ROUTE_TILE
    rk, rkt, st = _select(aff, triu, cap)
    starts = jnp.concatenate([st[:, :, :ns], jnp.full((b, N_EXPERTS, 1), cap, F32)], axis=2)
    starts = starts.astype(jnp.int32).reshape(-1)
    y = _expert_ffn(starts, u2, rk, aff, w_gate, w_up, w_down, layer, cap)
    return _combine(starts, rkt, x, g2, ln_g, ln_b, y, layer, cap, alpha)


def _rope_tables(l):
    pos = jnp.arange(l)
    nf = HEAD_DIM // 4
    inv = ROPE_THETA ** (-jnp.arange(nf, dtype=F32) / nf)
    ang_r = (pos // GRID_W).astype(F32)[:, None] * inv[None, :]
    ang_c = (pos % GRID_W).astype(F32)[:, None] * inv[None, :]
    cos = jnp.concatenate([jnp.cos(ang_r)] * 2 + [jnp.cos(ang_c)] * 2, axis=1)
    sin = jnp.concatenate([-jnp.sin(ang_r), jnp.sin(ang_r), -jnp.sin(ang_c), jnp.sin(ang_c)], axis=1)
    return jnp.tile(cos, (1, ATTN_HEADS)), jnp.tile(sin, (1, ATTN_HEADS))


def kernel(x, c, ctx, c_ctx, w_mod, b_mod, w_in, attn_sink, hgrn_lb_fw, hgrn_lb_bw, hgrn_norm_g,
           w_branch_attn, w_branch_hgrn, w_out, w_router, w_gate, w_up, w_down, ln_g, ln_b):
    depth, d = w_mod.shape[0], x.shape[-1]
    b = x.shape[0]
    assert x.shape[1] % (4 * TOKEN_TILE) == 0 and ctx.shape[1] == TOKEN_TILE
    alpha = (2 * depth) ** 0.25
    mod_rows = -(-(b + 1) // 8) * 8
    c_all = jnp.concatenate([c, c_ctx[None, :], jnp.zeros((mod_rows - b - 1, d), F32)], axis=0)
    b_mod3 = b_mod[:, None, :]
    norm_g3 = hgrn_norm_g[:, None, :]
    ln_g4, ln_b4 = ln_g[:, :, None, :], ln_b[:, :, None, :]
    w_in_b = _relayout_w_in(w_in)
    w_a_b, w_h_b, w_o_b = w_branch_attn.astype(BF16), w_branch_hgrn.astype(BF16), w_out.astype(BF16)
    w_g_b, w_u_b, w_d_b = w_gate.astype(BF16), w_up.astype(BF16), w_down.astype(BF16)
    wr_pad = jnp.pad(w_router, ((0, 0), (0, 0), (0, ROUTER_LANES - N_EXPERTS)))
    wr_hi = wr_pad.astype(BF16)
    wr_lo = (wr_pad - wr_hi.astype(F32)).astype(BF16)
    rope_tabs = _rope_tables(x.shape[1])
    tri, lmask = _hgrn_constants(HG_CHUNK)
    r_i = lax.broadcasted_iota(jnp.int32, (ROUTE_TILE, ROUTE_TILE), 0)
    c_i = lax.broadcasted_iota(jnp.int32, (ROUTE_TILE, ROUTE_TILE), 1)
    triu = (r_i <= c_i).astype(BF16)

    xc = ctx
    for l in range(depth):
        need_ctx = l < depth - 1
        mod = _modulation(c_all, w_mod, b_mod3, l)
        sh1, sc1, g1, sh2, sc2, g2 = [mod[:b, i * d:(i + 1) * d][:, None, :] for i in range(N_MOD)]
        sh1c, sc1c, g1c, sh2c, sc2c, g2c = [mod[b:b + 1, i * d:(i + 1) * d][:, None, :] for i in range(N_MOD)]
        cq, ckv, chg, cgab = _in_projection(xc, sc1c, sh1c, w_in_b, hgrn_lb_fw, hgrn_lb_bw, l, None)
        oc_f, oc_b, s_ctx = _hgrn_scan(chg, tri, lmask, None)
        q, kv, hgp, gab = _in_projection(x, sc1, sh1, w_in_b, hgrn_lb_fw, hgrn_lb_bw, l, rope_tabs)
        o_f, o_b, _ = _hgrn_scan(hgp, tri, lmask, s_ctx)
        attn = _attention(q, kv, ckv, attn_sink, l)
        x1, u2, aff = _mixer_output(attn, o_f, o_b, hgp, gab, x, g1, sc2, sh2, norm_g3, w_a_b, w_h_b, w_o_b,
                                    ln_g4, ln_b4, wr_hi, wr_lo, l, alpha)
        x = _moe(x1, u2, aff, g2, ln_g4, ln_b4, triu, w_g_b, w_u_b, w_d_b, l, alpha)
        if need_ctx:
            attn_c = _attention(cq, None, ckv, attn_sink, l)
            xc1, uc2, affc = _mixer_output(attn_c, oc_f, oc_b, chg, cgab, xc, g1c, sc2c, sh2c, norm_g3,
                                           w_a_b, w_h_b, w_o_b, ln_g4, ln_b4, wr_hi, wr_lo, l, alpha)
            xc = _moe(xc1, uc2, affc, g2c, ln_g4, ln_b4, triu, w_g_b, w_u_b, w_d_b, l, alpha)
    return x
```

```python
import functools

import jax
import jax.numpy as jnp
from jax import lax
from jax.experimental import pallas as pl
from jax.experimental.pallas import tpu as pltpu

F32 = jnp.float32
BF16 = jnp.bfloat16

GRID_W = 64
ATTN_HEADS = 8
ATTN_KV_HEADS = 2
HEAD_DIM = 64
WINDOW = 128
WBLOCK = 128
ROPE_THETA = 10000.0
HG_HEADS = 4
HG_DK = 128
HG_DV = 128
N_EXPERTS = 16
CAPACITY_FACTOR = 2
N_MOD = 6
LN_EPS = 1e-6
ATTN_Q_DIM = ATTN_HEADS * HEAD_DIM
ATTN_KV_DIM = ATTN_KV_HEADS * HEAD_DIM
HG_DIM = HG_HEADS * HG_DK

TOKEN_TILE = 256
HG_CHUNK = 128
ATTN_STEP_BLOCKS = 2
HG_DIRECT = 16
HG_EXP_LIMIT = 80.0
ROUTE_TILE = 128
SLOT_ALIGN = 16
DISPATCH_SMALL_WINDOW = 128
ROUTER_LANES = 128
VMEM_LIMIT_BYTES = 56 << 20


def _cparams(*sem):
    return pltpu.CompilerParams(dimension_semantics=sem, vmem_limit_bytes=VMEM_LIMIT_BYTES)


def _dot(a, b):
    return jnp.dot(a, b, preferred_element_type=F32)


def _dot_nt(a, b):
    return lax.dot_general(a, b, (((1,), (1,)), ((), ())), preferred_element_type=F32)


def _dot_tn(a, b):
    return lax.dot_general(a, b, (((0,), (0,)), ((), ())), preferred_element_type=F32)


def _split2(a):
    hi = a.astype(BF16)
    lo = (a - hi.astype(F32)).astype(BF16)
    return hi, lo


def _split3(a):
    hi = a.astype(BF16)
    r = a - hi.astype(F32)
    mid = r.astype(BF16)
    lo = (r - mid.astype(F32)).astype(BF16)
    return hi, mid, lo


def _layer_norm(x):
    mu = jnp.mean(x, axis=-1, keepdims=True)
    xc = x - mu
    var = jnp.mean(xc * xc, axis=-1, keepdims=True)
    return xc * lax.rsqrt(var + LN_EPS)


def _sigmoid(x):
    return 0.5 * jnp.tanh(0.5 * x) + 0.5


def _silu(x):
    return x * _sigmoid(x)


def _mod_kernel(c_ref, w_ref, b_ref, o_ref):
    a = _silu(c_ref[...])
    ah, al = _split2(a)
    wh, wl = _split2(w_ref[...])
    o_ref[...] = _dot(ah, wh) + _dot(ah, wl) + _dot(al, wh) + b_ref[...]


def _modulation(c_all, w_mod, b_mod, layer):
    rows, d = c_all.shape
    cols = w_mod.shape[-1]
    tn = 1024
    return pl.pallas_call(
        _mod_kernel,
        grid=(cols // tn,),
        in_specs=[
            pl.BlockSpec((rows, d), lambda j: (0, 0)),
            pl.BlockSpec((None, d, tn), lambda j: (layer, 0, j)),
            pl.BlockSpec((None, 1, tn), lambda j: (layer, 0, j)),
        ],
        out_specs=pl.BlockSpec((rows, tn), lambda j: (0, j)),
        out_shape=jax.ShapeDtypeStruct((rows, cols), F32),
        compiler_params=_cparams("arbitrary"),
        name="modulation",
    )(c_all, w_mod, b_mod)


def _rope(x, cos, sin):
    w = x.shape[1]
    lane = lax.broadcasted_iota(jnp.int32, x.shape, 1)
    partner = jnp.where((lane & 16) == 0, pltpu.roll(x, w - 16, 1), pltpu.roll(x, 16, 1))
    return x * cos + partner * sin


def _lower_bound(lb_ref, layer):
    logits = lb_ref[...]
    m = jnp.max(logits, axis=0, keepdims=True)
    ex = jnp.exp(logits - m)
    p = ex / jnp.sum(ex, axis=0, keepdims=True)
    lb = jnp.zeros_like(p[0:1])
    for j in range(1, layer + 1):
        lb = lb + p[j:j + 1]
    return lb


def _forget_gate(z, lb):
    log_sig = jnp.minimum(z, 0.0) - jnp.log1p(jnp.exp(-jnp.abs(z)))
    a = jnp.log(lb)
    bb = jnp.log1p(-lb) + log_sig
    log_f = jnp.maximum(a, bb) + jnp.log1p(jnp.exp(-jnp.abs(a - bb)))
    return log_f, (1.0 - lb) * _sigmoid(-z)


KV2_DIM = 2 * ATTN_KV_DIM
IN_Q0, IN_K0, IN_V0 = 0, ATTN_Q_DIM, ATTN_Q_DIM + KV2_DIM
IN_HG0 = IN_V0 + KV2_DIM
IN_GAB0 = IN_HG0 + 5 * HG_DIM
HGP_Q, HGP_GF, HGP_GB, HGP_KF, HGP_KB, HGP_V, HGP_OG = range(7)
IN_SLAB = 256


def _inproj_kernel(*refs, rope, layer):
    if rope:
        x_ref, sc_ref, sh_ref, w_ref, lbf_ref, lbb_ref, cos_ref, sin_ref, q_ref, kv_ref, hg_ref, gab_ref = refs
    else:
        x_ref, sc_ref, sh_ref, w_ref, lbf_ref, lbb_ref, q_ref, kv_ref, hg_ref, gab_ref = refs
    u = _layer_norm(x_ref[0]) * (1.0 + sc_ref[0]) + sh_ref[0]
    ub = u.astype(BF16)
    sw = IN_SLAB
    for c in range(0, ATTN_Q_DIM, sw):
        q = _dot(ub, w_ref[:, IN_Q0 + c:IN_Q0 + c + sw])
        q_ref[0, :, c:c + sw] = _rope(q, cos_ref[:, c:c + sw], sin_ref[:, c:c + sw]) if rope else q
    for c in range(0, KV2_DIM, sw):
        k = _dot(ub, w_ref[:, IN_K0 + c:IN_K0 + c + sw])
        kv_ref[0, :, c:c + sw] = _rope(k, cos_ref[:, c:c + sw], sin_ref[:, c:c + sw]) if rope else k
        kv_ref[0, :, KV2_DIM + c:KV2_DIM + c + sw] = _dot(ub, w_ref[:, IN_V0 + c:IN_V0 + c + sw])
    lb_f, lb_b = _lower_bound(lbf_ref, layer), _lower_bound(lbb_ref, layer)
    for c in range(0, HG_DIM, sw):
        def cols(j):
            return _dot(ub, w_ref[:, IN_HG0 + j * HG_DIM + c:IN_HG0 + j * HG_DIM + c + sw])

        def put(j, val):
            hg_ref[0, :, j * HG_DIM + c:j * HG_DIM + c + sw] = val

        put(HGP_Q, _silu(cols(0)))
        log_f, kk = _forget_gate(cols(1), lb_f[:, c:c + sw])
        put(HGP_GF, log_f)
        put(HGP_KF, kk)
        log_f, kk = _forget_gate(cols(2), lb_b[:, c:c + sw])
        put(HGP_GB, log_f)
        put(HGP_KB, kk)
        put(HGP_V, cols(3))
        put(HGP_OG, _silu(cols(4)))
    for c in range(0, gab_ref.shape[2], sw):
        gab_ref[0, :, c:c + sw] = _sigmoid(_dot(ub, w_ref[:, IN_GAB0 + c:IN_GAB0 + c + sw]))


def _in_projection(x, sc, sh, w_in, lb_fw, lb_bw, layer, rope_tabs):
    b, l, d = x.shape
    tm = TOKEN_TILE
    per_batch = sc.shape[0] > 1
    mod_map = (lambda bi, i: (bi, 0, 0)) if per_batch else (lambda bi, i: (0, 0, 0))
    in_dim = w_in.shape[-1]
    in_specs = [
        pl.BlockSpec((1, tm, d), lambda bi, i: (bi, i, 0)),
        pl.BlockSpec((1, 1, d), mod_map),
        pl.BlockSpec((1, 1, d), mod_map),
        pl.BlockSpec((None, d, in_dim), lambda bi, i: (layer, 0, 0), pipeline_mode=pl.Buffered(1)),
        pl.BlockSpec(lb_fw.shape, lambda bi, i: (0, 0)),
        pl.BlockSpec(lb_bw.shape, lambda bi, i: (0, 0)),
    ]
    args = [x, sc, sh, w_in, lb_fw, lb_bw]
    if rope_tabs is not None:
        in_specs += [pl.BlockSpec((tm, ATTN_Q_DIM), lambda bi, i: (i, 0))] * 2
        args += list(rope_tabs)
    widths = (ATTN_Q_DIM, 2 * KV2_DIM, 7 * HG_DIM, 2 * d)
    return pl.pallas_call(
        functools.partial(_inproj_kernel, rope=rope_tabs is not None, layer=layer),
        grid=(b, l // tm),
        in_specs=in_specs,
        out_specs=[pl.BlockSpec((1, tm, wd), lambda bi, i: (bi, i, 0)) for wd in widths],
        out_shape=[jax.ShapeDtypeStruct((b, l, wd), F32) for wd in widths],
        compiler_params=_cparams("parallel", "arbitrary"),
        name="in_projection",
    )(*args)


def _relayout_w_in(w_in):
    q0, k0, v0, r0 = 0, ATTN_Q_DIM, ATTN_Q_DIM + ATTN_KV_DIM, ATTN_Q_DIM + 2 * ATTN_KV_DIM
    parts = [w_in[..., q0:k0]]
    for c0 in (k0, v0):
        for g in range(ATTN_KV_HEADS):
            head = w_in[..., c0 + g * HEAD_DIM:c0 + (g + 1) * HEAD_DIM]
            parts += [head, head]
    parts.append(w_in[..., r0:])
    return jnp.concatenate(parts, axis=-1).astype(BF16)


def _attend(q, kvs, sinks, ok):
    tq = q.shape[0]
    lc = kvs[0].shape[0]
    rep = ATTN_HEADS // ATTN_KV_HEADS
    tile = 2 * HEAD_DIM
    low_half = lax.broadcasted_iota(jnp.int32, (tq, tile), 1) < HEAD_DIM
    q = q * HEAD_DIM ** -0.5
    if ok is not None:
        mask = jnp.concatenate([jnp.ones((tq, lc), jnp.bool_), ok], axis=1)
        mask = jnp.concatenate([mask] * rep, axis=0)
    tiles = []
    for g in range(ATTN_KV_HEADS):
        kk = jnp.concatenate([t[:, g * tile:(g + 1) * tile] for t in kvs], axis=0).astype(BF16)
        vv = jnp.concatenate([t[:, KV2_DIM + g * tile:KV2_DIM + (g + 1) * tile] for t in kvs], axis=0).astype(BF16)
        heads = range(g * rep, (g + 1) * rep)
        qs = []
        for h in heads:
            q2 = q[:, (h // 2) * tile:(h // 2 + 1) * tile]
            qs.append(jnp.where(low_half == (h % 2 == 0), q2, 0.0))
        s = _dot_nt(jnp.concatenate(qs, axis=0).astype(BF16), kk)
        if ok is not None:
            s = jnp.where(mask, s, -jnp.inf)
        sink = jnp.concatenate([jnp.full((tq, 1), sinks[h], F32) for h in heads], axis=0)
        m = jnp.maximum(jnp.max(s, axis=1, keepdims=True), sink)
        e = jnp.exp(s - m)
        den = jnp.exp(sink - m) + jnp.sum(e, axis=1, keepdims=True)
        o = _dot(e.astype(BF16), vv) / den
        for r in range(0, rep, 2):
            tiles.append(jnp.where(low_half, o[r * tq:(r + 1) * tq], o[(r + 1) * tq:(r + 2) * tq]))
    return jnp.concatenate(tiles, axis=1)


def _attn_kernel(*refs, layer, nsteps, windowed):
    sinks = [refs[0][layer, h] for h in range(ATTN_HEADS)]
    if not windowed:
        _, q_ref, ckv_ref, o_ref = refs
        o_ref[0] = _attend(q_ref[0], [ckv_ref[0]], sinks, None).astype(BF16)
        return
    _, q_ref, kvp_ref, kvc_ref, kvn_ref, ckv_ref, o_ref = refs
    n = pl.program_id(1)
    ti = lax.broadcasted_iota(jnp.int32, (WBLOCK, 3 * WBLOCK), 0)
    sj = lax.broadcasted_iota(jnp.int32, (WBLOCK, 3 * WBLOCK), 1)
    rel = sj - WBLOCK - ti
    band = (rel <= WINDOW) & (rel >= -WINDOW)
    blocks = [kvp_ref[0]] + [kvc_ref[0, i * WBLOCK:(i + 1) * WBLOCK] for i in range(ATTN_STEP_BLOCKS)] + [kvn_ref[0]]
    for i in range(ATTN_STEP_BLOCKS):
        ok = band
        if i == 0:
            ok = ok & ((sj >= WBLOCK) | (n > 0))
        if i == ATTN_STEP_BLOCKS - 1:
            ok = ok & ((sj < 2 * WBLOCK) | (n < nsteps - 1))
        rows = slice(i * WBLOCK, (i + 1) * WBLOCK)
        o_ref[0, rows, :] = _attend(q_ref[0, rows, :], [ckv_ref[0]] + blocks[i:i + 3], sinks, ok).astype(BF16)


def _attention(q, kv, ckv, sink, layer):
    b, l, _ = q.shape
    lc = ckv.shape[1]
    windowed = kv is not None
    smem = pl.BlockSpec(memory_space=pltpu.SMEM)
    cspec = pl.BlockSpec((1, lc, 2 * KV2_DIM), lambda bi, n: (bi, 0, 0))
    if windowed:
        kvw = 2 * KV2_DIM
        sb = ATTN_STEP_BLOCKS
        tq = sb * WBLOCK
        nb = l // WBLOCK
        in_specs = [
            smem, pl.BlockSpec((1, tq, ATTN_Q_DIM), lambda bi, n: (bi, n, 0)),
            pl.BlockSpec((1, WBLOCK, kvw), lambda bi, n: (bi, jnp.maximum(n * sb - 1, 0), 0)),
            pl.BlockSpec((1, tq, kvw), lambda bi, n: (bi, n, 0)),
            pl.BlockSpec((1, WBLOCK, kvw), lambda bi, n: (bi, jnp.minimum((n + 1) * sb, nb - 1), 0)),
            cspec,
        ]
        args = (sink, q, kv, kv, kv, ckv)
    else:
        tq = WBLOCK
        in_specs = [smem, pl.BlockSpec((1, tq, ATTN_Q_DIM), lambda bi, n: (bi, n, 0)), cspec]
        args = (sink, q, ckv)
    return pl.pallas_call(
        functools.partial(_attn_kernel, layer=layer, nsteps=l // tq, windowed=windowed),
        grid=(b, l // tq),
        in_specs=in_specs,
        out_specs=pl.BlockSpec((1, tq, ATTN_Q_DIM), lambda bi, n: (bi, n, 0)),
        out_shape=jax.ShapeDtypeStruct((b, l, ATTN_Q_DIM), BF16),
        compiler_params=_cparams("parallel", "arbitrary"),
        name="window_attention" if windowed else "context_attention",
    )(*args)


def _block_ref_rows(cum, blk, r):
    c, w = cum.shape
    if blk % 8 == 0:
        x = cum.reshape(c // blk, blk, w)
        return jnp.broadcast_to(x[:, r:r + 1, :], x.shape).reshape(c, w)
    pos = lax.broadcasted_iota(jnp.int32, cum.shape, 0) & (blk - 1)
    out = cum
    for p in range(blk):
        delta = r - p
        if delta == 0:
            continue
        out = jnp.where(pos == p, pltpu.roll(cum, (-delta) % c, 0), out)
    return out


def _hgrn_kernel(*refs, has_init):
    if has_init:
        (tri_ref, lm_ref, dm_ref, qf_ref, gf_ref, kf_ref, vf_ref, qb_ref, gb_ref, kb_ref, vb_ref, s0_ref,
         of_ref, ob_ref, s_ref, cum_ref) = refs
    else:
        (tri_ref, lm_ref, dm_ref, qf_ref, gf_ref, kf_ref, vf_ref, qb_ref, gb_ref, kb_ref, vb_ref,
         of_ref, ob_ref, s_ref, cum_ref) = refs

    @pl.when(pl.program_id(1) == 0)
    def _():
        if has_init:
            s_ref[...] = s0_ref[...]
        else:
            s_ref[...] = jnp.zeros(s_ref.shape, F32)

    tt = qf_ref.shape[1]
    c = tri_ref.shape[1]
    nch = tt // c
    row = lax.broadcasted_iota(jnp.int32, (c, HG_DK), 0)
    levels = []
    hs = c // 2
    while hs >= 1:
        levels.append(hs)
        hs //= 2
    dirs = ((qf_ref, gf_ref, kf_ref, vf_ref, of_ref), (qb_ref, gb_ref, kb_ref, vb_ref, ob_ref))

    def direct_dd(cum, backward):
        return cum - _block_ref_rows(cum, HG_DIRECT, HG_DIRECT - 1 if backward else 0)

    worst = jnp.zeros((c, HG_DK), F32)
    for d, (_, g_ref, _, _, _) in enumerate(dirs):
        for h in range(HG_HEADS):
            for ci in range(nch):
                g3 = _split3(g_ref[0, ci * c:(ci + 1) * c, h * HG_DK:(h + 1) * HG_DK])
                cum = _dot(tri_ref[d], g3[0]) + _dot(tri_ref[d], g3[1]) + _dot(tri_ref[d], g3[2])
                cum_ref[(d * HG_HEADS + h) * nch + ci] = cum
                worst = jnp.maximum(worst, -direct_dd(cum, d == 1))
    in_range = jnp.max(worst) <= HG_EXP_LIMIT

    def scan(direct):
        tree = [(li, hs) for li, hs in enumerate(levels) if not direct or hs >= HG_DIRECT]
        for d, (q_ref, g_ref, k_ref, v_ref, o_ref) in enumerate(dirs):
            backward = d == 1
            q_halves = {li: ((row & (2 * hs - 1)) >= hs) != backward for li, hs in tree}
            for h in range(HG_HEADS):
                sl = slice(h * HG_DK, (h + 1) * HG_DK)
                q_all, k_all, v_all = q_ref[0, :, sl], k_ref[0, :, sl], v_ref[0, :, sl]
                st = s_ref[0, d, h]
                chunks = range(nch)
                for ci in (reversed(chunks) if backward else chunks):
                    rows = slice(ci * c, (ci + 1) * c)
                    q, kk, v = q_all[rows], k_all[rows], v_all[rows]
                    qb, kb, vb = q.astype(BF16), kk.astype(BF16), v.astype(BF16)
                    cum = cum_ref[(d * HG_HEADS + h) * nch + ci]
                    o = _dot_nt((q * jnp.exp(cum)).astype(BF16), st.astype(BF16))
                    o = o + jnp.sum(q * kk, axis=1, keepdims=True) * v
                    scores = jnp.zeros((c, c), F32)
                    for li, hs in tree:
                        blk = 2 * hs
                        dd = cum - _block_ref_rows(cum, blk, hs if backward else hs - 1)
                        e = jnp.exp(-jnp.abs(dd)).astype(BF16)
                        x = jnp.where(q_halves[li], qb, kb) * e
                        scores = scores + _dot_nt(x, x) * lm_ref[d, li]
                    if direct:
                        dd = direct_dd(cum, backward)
                        xq = qb * jnp.exp(dd).astype(BF16)
                        xk = kb * jnp.exp(-dd).astype(BF16)
                        scores = scores + jnp.where(dm_ref[d] > 0.5, _dot_nt(xq, xk), 0.0)
                    o = o + _dot(scores.astype(BF16), vb)
                    o_ref[0, rows, sl] = o
                    tot = cum[0:1] if backward else cum[c - 1:c]
                    kdec = kk * jnp.exp(tot - cum)
                    st = st * jnp.exp(tot) + _dot_tn(vb, kdec.astype(BF16))
                s_ref[0, d, h] = st

    pl.when(in_range)(lambda: scan(True))
    pl.when(jnp.logical_not(in_range))(lambda: scan(False))


def _hgrn_scan(hgp, consts, s0):
    b, l, _ = hgp.shape
    tt = TOKEN_TILE
    ns = l // tt
    has_init = s0 is not None
    tri, lmask, dmask = consts
    fwd = lambda j: pl.BlockSpec((1, tt, HG_DIM), lambda bi, i: (bi, i, j))
    bwd = lambda j: pl.BlockSpec((1, tt, HG_DIM), lambda bi, i: (bi, ns - 1 - i, j))
    sspec = pl.BlockSpec((1, 2, HG_HEADS, HG_DV, HG_DK), lambda bi, i: (bi, 0, 0, 0, 0))
    in_specs = [
        pl.BlockSpec(tri.shape, lambda bi, i: (0, 0, 0)),
        pl.BlockSpec(lmask.shape, lambda bi, i: (0, 0, 0, 0)),
        pl.BlockSpec(dmask.shape, lambda bi, i: (0, 0, 0)),
        fwd(HGP_Q), fwd(HGP_GF), fwd(HGP_KF), fwd(HGP_V), bwd(HGP_Q), bwd(HGP_GB), bwd(HGP_KB), bwd(HGP_V),
    ]
    args = [tri, lmask, dmask] + [hgp] * 8
    if has_init:
        in_specs.append(sspec)
        args.append(s0)
    c = tri.shape[1]
    return pl.pallas_call(
        functools.partial(_hgrn_kernel, has_init=has_init),
        grid=(b, ns),
        in_specs=in_specs,
        out_specs=[
            pl.BlockSpec((1, tt, HG_DIM), lambda bi, i: (bi, i, 0)),
            pl.BlockSpec((1, tt, HG_DIM), lambda bi, i: (bi, ns - 1 - i, 0)),
            sspec,
        ],
        out_shape=[
            jax.ShapeDtypeStruct((b, l, HG_DIM), F32),
            jax.ShapeDtypeStruct((b, l, HG_DIM), F32),
            jax.ShapeDtypeStruct((b, 2, HG_HEADS, HG_DV, HG_DK), F32),
        ],
        scratch_shapes=[pltpu.VMEM((2 * HG_HEADS * (tt // c), c, HG_DK), F32)],
        compiler_params=_cparams("parallel", "arbitrary"),
        name="hgrn_scan",
    )(*args)


def _hgrn_constants(c):
    t = lax.broadcasted_iota(jnp.int32, (c, c), 0)
    s = lax.broadcasted_iota(jnp.int32, (c, c), 1)
    tri = jnp.stack([s <= t, s >= t]).astype(BF16)
    masks, direct = [], []
    for backward in (False, True):
        per_level = []
        hs = c // 2
        while hs >= 1:
            blk = 2 * hs
            same = (t // blk) == (s // blk)
            t_late, s_late = (t % blk) >= hs, (s % blk) >= hs
            pair = (jnp.logical_not(t_late) & s_late) if backward else (t_late & jnp.logical_not(s_late))
            per_level.append(same & pair)
            hs //= 2
        masks.append(jnp.stack(per_level))
        direct.append(((t // HG_DIRECT) == (s // HG_DIRECT)) & ((s > t) if backward else (s < t)))
    return tri, jnp.stack(masks).astype(F32), jnp.stack(direct).astype(F32)


def _mixout_kernel(attn_ref, of_ref, ob_ref, gh_ref, ga_ref, gb_ref, x_ref, g1_ref, sc2_ref, sh2_ref,
                   ng_ref, wa_ref, wh_ref, wo_ref, lng_ref, lnb_ref, wrh_ref, wrl_ref,
                   xo_ref, u2_ref, aff_ref, *, alpha):
    o = of_ref[0] + ob_ref[0]
    parts = []
    for h in range(HG_HEADS):
        oh = o[:, h * HG_DV:(h + 1) * HG_DV]
        parts.append(oh * lax.rsqrt(jnp.mean(oh * oh, axis=-1, keepdims=True) + LN_EPS))
    hg = jnp.concatenate(parts, axis=1) * ng_ref[...] * gh_ref[0]
    ya = _dot(attn_ref[0], wa_ref[...])
    yh = _dot(hg.astype(BF16), wh_ref[...])
    mix = ga_ref[0] * ya + gb_ref[0] * yh
    y = _dot(mix.astype(BF16), wo_ref[...])
    xn = _layer_norm(alpha * x_ref[0] + g1_ref[0] * y) * lng_ref[...] + lnb_ref[...]
    xo_ref[0] = xn
    u2 = _layer_norm(xn) * (1.0 + sc2_ref[0]) + sh2_ref[0]
    u2_ref[0] = u2.astype(BF16)
    uh, ul = _split2(u2)
    lg = _dot(uh, wrh_ref[...]) + _dot(ul, wrh_ref[...]) + _dot(uh, wrl_ref[...])
    lg = jnp.where(lax.broadcasted_iota(jnp.int32, lg.shape, 1) < N_EXPERTS, lg, -jnp.inf)
    ex = jnp.exp(lg - jnp.max(lg, axis=1, keepdims=True))
    aff = ex / jnp.sum(ex, axis=1, keepdims=True)
    aff_ref[0] = aff.T[0:N_EXPERTS, :]


def _mixer_output(attn, o_f, o_b, hgp, gab, x, g1, sc2, sh2, norm_g, w_a, w_h, w_o, ln_g, ln_b,
                  wr_hi, wr_lo, layer, alpha):
    b, l, d = x.shape
    tm = TOKEN_TILE
    per_batch = g1.shape[0] > 1
    mod_map = (lambda bi, i: (bi, 0, 0)) if per_batch else (lambda bi, i: (0, 0, 0))
    tok = lambda w, j=0: pl.BlockSpec((1, tm, w), lambda bi, i: (bi, i, j))
    lay2 = lambda r, c: pl.BlockSpec((None, r, c), lambda bi, i: (layer, 0, 0))
    in_specs = [
        tok(ATTN_Q_DIM), tok(HG_DIM), tok(HG_DIM), tok(HG_DIM, HGP_OG), tok(d, 0), tok(d, 1), tok(d),
        pl.BlockSpec((1, 1, d), mod_map), pl.BlockSpec((1, 1, d), mod_map), pl.BlockSpec((1, 1, d), mod_map),
        lay2(1, HG_DIM), lay2(ATTN_Q_DIM, d), lay2(HG_DIM, d), lay2(d, d),
        pl.BlockSpec((None, None, 1, d), lambda bi, i: (layer, 0, 0, 0)),
        pl.BlockSpec((None, None, 1, d), lambda bi, i: (layer, 0, 0, 0)),
        lay2(d, ROUTER_LANES), lay2(d, ROUTER_LANES),
    ]
    return pl.pallas_call(
        functools.partial(_mixout_kernel, alpha=alpha),
        grid=(b, l // tm),
        in_specs=in_specs,
        out_specs=[tok(d), tok(d), pl.BlockSpec((1, N_EXPERTS, tm), lambda bi, i: (bi, 0, i))],
        out_shape=[
            jax.ShapeDtypeStruct((b, l, d), F32),
            jax.ShapeDtypeStruct((b, l, d), BF16),
            jax.ShapeDtypeStruct((b, N_EXPERTS, l), F32),
        ],
        compiler_params=_cparams("parallel", "arbitrary"),
        name="mixer_output",
    )(attn, o_f, o_b, hgp, gab, gab, x, g1, sc2, sh2, norm_g, w_a, w_h, w_o, ln_g, ln_b, wr_hi, wr_lo)


def _select_kernel(aff_ref, triu_ref, rk_ref, rkt_ref, st_ref, *, cap):
    a = aff_ref[0]
    ne, n = a.shape
    ts = triu_ref.shape[0]
    bits = lax.bitcast_convert_type(a, jnp.int32)
    capf = jnp.float32(cap)

    def count_ge(v):
        return jnp.sum(jnp.where(bits >= v, 1.0, 0.0), axis=1, keepdims=True)

    def body(_, carry):
        lo, hi = carry
        mid = lo + ((hi - lo) >> 1)
        ok = count_ge(mid) >= capf
        return jnp.where(ok, mid, lo), jnp.where(ok, hi, mid)

    lo0 = jnp.zeros((ne, 1), jnp.int32)
    hi0 = jnp.full((ne, 1), 0x7F800000, jnp.int32)
    thr, _ = lax.fori_loop(0, 31, body, (lo0, hi0))
    gt = bits > thr
    eq = bits == thr
    need = capf - jnp.sum(jnp.where(gt, 1.0, 0.0), axis=1, keepdims=True)
    triu = triu_ref[...]
    lane = lax.broadcasted_iota(jnp.int32, st_ref.shape[1:], 1)
    eq_carry = jnp.zeros((ne, 1), F32)
    sel_carry = jnp.zeros((ne, 1), F32)
    starts = jnp.zeros(st_ref.shape[1:], F32)
    for t in range(n // ts):
        sl = slice(t * ts, (t + 1) * ts)
        eq_t = eq[:, sl]
        eq_pref = _dot(jnp.where(eq_t, 1.0, 0.0).astype(BF16), triu) + eq_carry
        sel_t = gt[:, sl] | (eq_t & (eq_pref <= need))
        sel_f = jnp.where(sel_t, 1.0, 0.0)
        sel_pref = _dot(sel_f.astype(BF16), triu) + sel_carry
        rk_t = jnp.where(sel_t, sel_pref - 1.0, -1.0)
        rk_ref[0, :, sl] = rk_t
        rkt_ref[0, sl, :] = rk_t.T
        starts = jnp.where(lane == t, sel_carry, starts)
        eq_carry = eq_pref[:, ts - 1:ts]
        sel_carry = sel_pref[:, ts - 1:ts]
    st_ref[0] = starts


def _select(aff, triu, cap):
    b, ne, n = aff.shape
    return pl.pallas_call(
        functools.partial(_select_kernel, cap=cap),
        grid=(b,),
        in_specs=[
            pl.BlockSpec((1, ne, n), lambda bi: (bi, 0, 0)),
            pl.BlockSpec(triu.shape, lambda bi: (0, 0)),
        ],
        out_specs=[
            pl.BlockSpec((1, ne, n), lambda bi: (bi, 0, 0)),
            pl.BlockSpec((1, n, ne), lambda bi: (bi, 0, 0)),
            pl.BlockSpec((1, ne, 128), lambda bi: (bi, 0, 0)),
        ],
        out_shape=[
            jax.ShapeDtypeStruct((b, ne, n), F32),
            jax.ShapeDtypeStruct((b, n, ne), F32),
            jax.ShapeDtypeStruct((b, ne, 128), F32),
        ],
        compiler_params=_cparams("parallel"),
        name="expert_choice_select",
    )(aff, triu)


def _slot_window(cap, tile):
    return min(cap, tile + SLOT_ALIGN)


def _aligned_start(s0, cap, w):
    return jnp.minimum((s0 // SLOT_ALIGN) * SLOT_ALIGN, cap - w)


def _window_start(s0, cap, tile):
    return _aligned_start(s0, cap, _slot_window(cap, tile))


def _ffn_kernel(st_ref, u_ref, rk_ref, aff_ref, wg_ref, wu_ref, wd_ref, y_ref, xs_ref, *, cap, ns, unroll):
    b, e = pl.program_id(0), pl.program_id(1)
    d = u_ref.shape[2]
    nt = rk_ref.shape[2]
    w_full = _slot_window(cap, TOKEN_TILE)
    w_small = min(cap, DISPATCH_SMALL_WINDOW)
    sub = TOKEN_TILE // ROUTE_TILE
    base = (b * N_EXPERTS + e) * (ns + 1)

    def bounds(t):
        return st_ref[base + t * sub], st_ref[base + (t + 1) * sub]

    def dispatch(t, w):
        s0, s1 = bounds(t)
        a = pl.multiple_of(_aligned_start(s0, cap, w), SLOT_ALIGN)
        slot = a + lax.broadcasted_iota(jnp.int32, (w, 1), 0)
        hit = slot.astype(F32) == rk_ref[0, 0, pl.ds(t, 1), :]
        mine = (slot >= s0) & (slot < s1)
        tok0 = pl.multiple_of(t * TOKEN_TILE, TOKEN_TILE)
        rows = _dot(jnp.where(hit, 1.0, 0.0).astype(BF16), u_ref[0, pl.ds(tok0, TOKEN_TILE), :])
        wts = jnp.sum(jnp.where(hit, aff_ref[0, 0, pl.ds(t, 1), :], 0.0), axis=1, keepdims=True)
        vals = jnp.concatenate([rows, jnp.broadcast_to(wts, (w, 128))], axis=1)
        pltpu.store(xs_ref.at[pl.ds(a, w), :], vals, mask=jnp.broadcast_to(mine, vals.shape))

    def run(w):
        def group(i, carry):
            for j in range(unroll):
                dispatch(i * unroll + j, w)
            return carry

        lax.fori_loop(0, nt // unroll, group, 0)

    if w_small < w_full:
        def fits(t, ok):
            s0, s1 = bounds(t)
            return ok & (s1 - _aligned_start(s0, cap, w_small) <= w_small)

        narrow = lax.fori_loop(0, nt, fits, True)
        pl.when(narrow)(lambda: run(w_small))
        pl.when(jnp.logical_not(narrow))(lambda: run(w_full))
    else:
        run(w_full)

    rc = min(cap, 256)
    for c in range(cap // rc):
        rows = slice(c * rc, (c + 1) * rc)
        xb = xs_ref[rows, 0:d].astype(BF16)
        h = _silu(_dot(xb, wg_ref[0])) * _dot(xb, wu_ref[0])
        y_ref[0, 0, rows, :] = (_dot(h.astype(BF16), wd_ref[0]) * xs_ref[rows, d:d + 1]).astype(BF16)


def _expert_ffn(starts, u2, rk, aff, w_gate, w_up, w_down, layer, cap):
    b, n, d = u2.shape
    ff = w_gate.shape[-1]
    ns = n // ROUTE_TILE
    nt = n // TOKEN_TILE
    rk4 = rk.reshape(b, N_EXPERTS, nt, TOKEN_TILE)
    aff4 = aff.reshape(b, N_EXPERTS, nt, TOKEN_TILE)
    wspec = lambda r, c: pl.BlockSpec((None, 1, r, c), lambda bi, e, st: (layer, e, 0, 0))
    rspec = pl.BlockSpec((1, 1, nt, TOKEN_TILE), lambda bi, e, st: (bi, e, 0, 0))
    grid_spec = pltpu.PrefetchScalarGridSpec(
        num_scalar_prefetch=1,
        grid=(b, N_EXPERTS),
        in_specs=[
            pl.BlockSpec((1, n, d), lambda bi, e, st: (bi, 0, 0), pipeline_mode=pl.Buffered(1)),
            rspec, rspec,
            wspec(d, ff), wspec(d, ff), wspec(ff, d),
        ],
        out_specs=pl.BlockSpec((1, 1, cap, d), lambda bi, e, st: (bi, e, 0, 0)),
        scratch_shapes=[pltpu.VMEM((cap, d + 128), F32)],
    )
    return pl.pallas_call(
        functools.partial(_ffn_kernel, cap=cap, ns=ns, unroll=min(nt, 4)),
        grid_spec=grid_spec,
        out_shape=jax.ShapeDtypeStruct((b, N_EXPERTS, cap, d), BF16),
        compiler_params=_cparams("parallel", "arbitrary"),
        name="expert_ffn",
    )(starts, u2, rk4, aff4, w_gate, w_up, w_down)


def _combine_kernel(*refs, cap, ns, alpha):
    st_ref, rkt_ref, x_ref, g2_ref, lng_ref, lnb_ref = refs[:6]
    y_refs = refs[6:6 + N_EXPERTS]
    o_ref = refs[6 + N_EXPERTS]
    b, t = pl.program_id(0), pl.program_id(1)
    sub = TOKEN_TILE // ROUTE_TILE
    w = _slot_window(cap, ROUTE_TILE)
    for half in range(sub):
        rows = slice(half * ROUTE_TILE, (half + 1) * ROUTE_TILE)
        acc = jnp.zeros((ROUTE_TILE, x_ref.shape[2]), F32)
        for e in range(N_EXPERTS):
            base = (b * N_EXPERTS + e) * (ns + 1) + t * sub
            a_tile = _window_start(st_ref[base], cap, TOKEN_TILE)
            a = _window_start(st_ref[base + half], cap, ROUTE_TILE)
            off = pl.multiple_of(a - a_tile, SLOT_ALIGN)
            slot = (a + lax.broadcasted_iota(jnp.int32, (1, w), 1)).astype(F32)
            hit = rkt_ref[0, rows, e:e + 1] == slot
            acc = acc + _dot(jnp.where(hit, 1.0, 0.0).astype(BF16), y_refs[e][pl.ds(off, w), :])
        xr = alpha * x_ref[0, rows, :] + g2_ref[0] * acc
        o_ref[0, rows, :] = _layer_norm(xr) * lng_ref[...] + lnb_ref[...]


def _combine(starts, rkt, x, g2, ln_g, ln_b, y, layer, cap, alpha):
    b, n, d = x.shape
    tt = TOKEN_TILE
    ns = n // ROUTE_TILE
    sub = TOKEN_TILE // ROUTE_TILE
    w = _slot_window(cap, tt)
    per_batch = g2.shape[0] > 1
    mod_map = (lambda bi, t, st: (bi, 0, 0)) if per_batch else (lambda bi, t, st: (0, 0, 0))

    def y_spec(e):
        def index_map(bi, t, st):
            be = bi * N_EXPERTS + e
            a = _window_start(st[be * (ns + 1) + t * sub], cap, tt)
            return pl.multiple_of(be * cap + a, SLOT_ALIGN), 0
        return pl.BlockSpec((pl.Element(w), pl.Element(d)), index_map)

    y = y.reshape(b * N_EXPERTS * cap, d)
    grid_spec = pltpu.PrefetchScalarGridSpec(
        num_scalar_prefetch=1,
        grid=(b, n // tt),
        in_specs=[
            pl.BlockSpec((1, tt, N_EXPERTS), lambda bi, t, st: (bi, t, 0)),
            pl.BlockSpec((1, tt, d), lambda bi, t, st: (bi, t, 0)),
            pl.BlockSpec((1, 1, d), mod_map),
            pl.BlockSpec((None, None, 1, d), lambda bi, t, st: (layer, 1, 0, 0)),
            pl.BlockSpec((None, None, 1, d), lambda bi, t, st: (layer, 1, 0, 0)),
        ] + [y_spec(e) for e in range(N_EXPERTS)],
        out_specs=pl.BlockSpec((1, tt, d), lambda bi, t, st: (bi, t, 0)),
    )
    return pl.pallas_call(
        functools.partial(_combine_kernel, cap=cap, ns=ns, alpha=alpha),
        grid_spec=grid_spec,
        out_shape=jax.ShapeDtypeStruct((b, n, d), F32),
        compiler_params=_cparams("parallel", "arbitrary"),
        name="moe_combine",
    )(starts, rkt, x, g2, ln_g, ln_b, *([y] * N_EXPERTS))


def _moe(x, u2, aff, g2, ln_g, ln_b, triu, w_gate, w_up, w_down, layer, alpha):
    b, n, _ = x.shape
    cap = CAPACITY_FACTOR * n // N_EXPERTS
    ns = n // ROUTE_TILE
    rk, rkt, st = _select(aff, triu, cap)
    starts = jnp.concatenate([st[:, :, :ns], jnp.full((b, N_EXPERTS, 1), cap, F32)], axis=2)
    starts = starts.astype(jnp.int32).reshape(-1)
    y = _expert_ffn(starts, u2, rk, aff, w_gate, w_up, w_down, layer, cap)
    return _combine(starts, rkt, x, g2, ln_g, ln_b, y, layer, cap, alpha)


def _rope_tables(l):
    pos = jnp.arange(l)
    nf = HEAD_DIM // 4
    inv = ROPE_THETA ** (-jnp.arange(nf, dtype=F32) / nf)
    ang_r = (pos // GRID_W).astype(F32)[:, None] * inv[None, :]
    ang_c = (pos % GRID_W).astype(F32)[:, None] * inv[None, :]
    cos = jnp.concatenate([jnp.cos(ang_r)] * 2 + [jnp.cos(ang_c)] * 2, axis=1)
    sin = jnp.concatenate([-jnp.sin(ang_r), jnp.sin(ang_r), -jnp.sin(ang_c), jnp.sin(ang_c)], axis=1)
    return jnp.tile(cos, (1, ATTN_HEADS)), jnp.tile(sin, (1, ATTN_HEADS))


def kernel(x, c, ctx, c_ctx, w_mod, b_mod, w_in, attn_sink, hgrn_lb_fw, hgrn_lb_bw, hgrn_norm_g,
           w_branch_attn, w_branch_hgrn, w_out, w_router, w_gate, w_up, w_down, ln_g, ln_b):
    depth, d = w_mod.shape[0], x.shape[-1]
    b = x.shape[0]
    assert x.shape[1] % (4 * TOKEN_TILE) == 0 and ctx.shape[1] == TOKEN_TILE
    alpha = (2 * depth) ** 0.25
    mod_rows = -(-(b + 1) // 8) * 8
    c_all = jnp.concatenate([c, c_ctx[None, :], jnp.zeros((mod_rows - b - 1, d), F32)], axis=0)
    b_mod3 = b_mod[:, None, :]
    norm_g3 = hgrn_norm_g[:, None, :]
    ln_g4, ln_b4 = ln_g[:, :, None, :], ln_b[:, :, None, :]
    w_in_b = _relayout_w_in(w_in)
    w_a_b, w_h_b, w_o_b = w_branch_attn.astype(BF16), w_branch_hgrn.astype(BF16), w_out.astype(BF16)
    w_g_b, w_u_b, w_d_b = w_gate.astype(BF16), w_up.astype(BF16), w_down.astype(BF16)
    wr_pad = jnp.pad(w_router, ((0, 0), (0, 0), (0, ROUTER_LANES - N_EXPERTS)))
    wr_hi = wr_pad.astype(BF16)
    wr_lo = (wr_pad - wr_hi.astype(F32)).astype(BF16)
    rope_tabs = _rope_tables(x.shape[1])
    hg_consts = _hgrn_constants(HG_CHUNK)
    r_i = lax.broadcasted_iota(jnp.int32, (ROUTE_TILE, ROUTE_TILE), 0)
    c_i = lax.broadcasted_iota(jnp.int32, (ROUTE_TILE, ROUTE_TILE), 1)
    triu = (r_i <= c_i).astype(BF16)

    xc = ctx
    for l in range(depth):
        need_ctx = l < depth - 1
        mod = _modulation(c_all, w_mod, b_mod3, l)
        sh1, sc1, g1, sh2, sc2, g2 = [mod[:b, i * d:(i + 1) * d][:, None, :] for i in range(N_MOD)]
        sh1c, sc1c, g1c, sh2c, sc2c, g2c = [mod[b:b + 1, i * d:(i + 1) * d][:, None, :] for i in range(N_MOD)]
        cq, ckv, chg, cgab = _in_projection(xc, sc1c, sh1c, w_in_b, hgrn_lb_fw, hgrn_lb_bw, l, None)
        oc_f, oc_b, s_ctx = _hgrn_scan(chg, hg_consts, None)
        q, kv, hgp, gab = _in_projection(x, sc1, sh1, w_in_b, hgrn_lb_fw, hgrn_lb_bw, l, rope_tabs)
        o_f, o_b, _ = _hgrn_scan(hgp, hg_consts, s_ctx)
        attn = _attention(q, kv, ckv, attn_sink, l)
        x1, u2, aff = _mixer_output(attn, o_f, o_b, hgp, gab, x, g1, sc2, sh2, norm_g3, w_a_b, w_h_b, w_o_b,
                                    ln_g4, ln_b4, wr_hi, wr_lo, l, alpha)
        x = _moe(x1, u2, aff, g2, ln_g4, ln_b4, triu, w_g_b, w_u_b, w_d_b, l, alpha)
        if need_ctx:
            attn_c = _attention(cq, None, ckv, attn_sink, l)
            xc1, uc2, affc = _mixer_output(attn_c, oc_f, oc_b, chg, cgab, xc, g1c, sc2c, sh2c, norm_g3,
                                           w_a_b, w_h_b, w_o_b, ln_g4, ln_b4, wr_hi, wr_lo, l, alpha)
            xc = _moe(xc1, uc2, affc, g2c, ln_g4, ln_b4, triu, w_g_b, w_u_b, w_d_b, l, alpha)
    return x
```

```python
import functools

import jax
import jax.numpy as jnp
from jax import lax
from jax.experimental import pallas as pl
from jax.experimental.pallas import tpu as pltpu

F32 = jnp.float32
BF16 = jnp.bfloat16

GRID_W = 64
ATTN_HEADS = 8
ATTN_KV_HEADS = 2
HEAD_DIM = 64
WINDOW = 128
WBLOCK = 128
ROPE_THETA = 10000.0
HG_HEADS = 4
HG_DK = 128
HG_DV = 128
N_EXPERTS = 16
CAPACITY_FACTOR = 2
N_MOD = 6
LN_EPS = 1e-6
ATTN_Q_DIM = ATTN_HEADS * HEAD_DIM
ATTN_KV_DIM = ATTN_KV_HEADS * HEAD_DIM
HG_DIM = HG_HEADS * HG_DK

TOKEN_TILE = 256
HG_CHUNK = 128
ATTN_STEP_BLOCKS = 2
HG_DIRECT = 16
HG_EXP_LIMIT = 80.0
ROUTE_TILE = 128
SLOT_ALIGN = 16
DISPATCH_WINDOWS = (128,)
COMBINE_WINDOW = 128
ROUTER_LANES = 128
VMEM_LIMIT_BYTES = 56 << 20


def _cparams(*sem):
    return pltpu.CompilerParams(dimension_semantics=sem, vmem_limit_bytes=VMEM_LIMIT_BYTES)


def _dot(a, b):
    return jnp.dot(a, b, preferred_element_type=F32)


def _dot_nt(a, b):
    return lax.dot_general(a, b, (((1,), (1,)), ((), ())), preferred_element_type=F32)


def _dot_tn(a, b):
    return lax.dot_general(a, b, (((0,), (0,)), ((), ())), preferred_element_type=F32)


def _split2(a):
    hi = a.astype(BF16)
    lo = (a - hi.astype(F32)).astype(BF16)
    return hi, lo


def _split3(a):
    hi = a.astype(BF16)
    r = a - hi.astype(F32)
    mid = r.astype(BF16)
    lo = (r - mid.astype(F32)).astype(BF16)
    return hi, mid, lo


def _layer_norm(x):
    mu = jnp.mean(x, axis=-1, keepdims=True)
    xc = x - mu
    var = jnp.mean(xc * xc, axis=-1, keepdims=True)
    return xc * lax.rsqrt(var + LN_EPS)


def _sigmoid(x):
    return 0.5 * jnp.tanh(0.5 * x) + 0.5


def _silu(x):
    return x * _sigmoid(x)


def _mod_kernel(c_ref, w_ref, b_ref, o_ref):
    a = _silu(c_ref[...])
    ah, al = _split2(a)
    wh, wl = _split2(w_ref[...])
    o_ref[...] = _dot(ah, wh) + _dot(ah, wl) + _dot(al, wh) + b_ref[...]


def _modulation(c_all, w_mod, b_mod, layer):
    rows, d = c_all.shape
    cols = w_mod.shape[-1]
    tn = 1024
    return pl.pallas_call(
        _mod_kernel,
        grid=(cols // tn,),
        in_specs=[
            pl.BlockSpec((rows, d), lambda j: (0, 0)),
            pl.BlockSpec((None, d, tn), lambda j: (layer, 0, j)),
            pl.BlockSpec((None, 1, tn), lambda j: (layer, 0, j)),
        ],
        out_specs=pl.BlockSpec((rows, tn), lambda j: (0, j)),
        out_shape=jax.ShapeDtypeStruct((rows, cols), F32),
        compiler_params=_cparams("arbitrary"),
        name="modulation",
    )(c_all, w_mod, b_mod)


def _rope(x, cos, sin):
    w = x.shape[1]
    lane = lax.broadcasted_iota(jnp.int32, x.shape, 1)
    partner = jnp.where((lane & 16) == 0, pltpu.roll(x, w - 16, 1), pltpu.roll(x, 16, 1))
    return x * cos + partner * sin


def _lower_bound(lb_ref, layer):
    logits = lb_ref[...]
    m = jnp.max(logits, axis=0, keepdims=True)
    ex = jnp.exp(logits - m)
    p = ex / jnp.sum(ex, axis=0, keepdims=True)
    lb = jnp.zeros_like(p[0:1])
    for j in range(1, layer + 1):
        lb = lb + p[j:j + 1]
    return lb


def _forget_gate(z, lb):
    log_sig = jnp.minimum(z, 0.0) - jnp.log1p(jnp.exp(-jnp.abs(z)))
    a = jnp.log(lb)
    bb = jnp.log1p(-lb) + log_sig
    log_f = jnp.maximum(a, bb) + jnp.log1p(jnp.exp(-jnp.abs(a - bb)))
    return log_f, (1.0 - lb) * _sigmoid(-z)


KV2_DIM = 2 * ATTN_KV_DIM
IN_Q0, IN_K0, IN_V0 = 0, ATTN_Q_DIM, ATTN_Q_DIM + KV2_DIM
IN_HG0 = IN_V0 + KV2_DIM
IN_GAB0 = IN_HG0 + 5 * HG_DIM
HGP_Q, HGP_GF, HGP_GB, HGP_KF, HGP_KB, HGP_V, HGP_OG = range(7)
IN_SLAB = 256


def _inproj_kernel(*refs, rope, layer):
    if rope:
        x_ref, sc_ref, sh_ref, w_ref, lbf_ref, lbb_ref, cos_ref, sin_ref, q_ref, kv_ref, hg_ref, gab_ref = refs
    else:
        x_ref, sc_ref, sh_ref, w_ref, lbf_ref, lbb_ref, q_ref, kv_ref, hg_ref, gab_ref = refs
    u = _layer_norm(x_ref[0]) * (1.0 + sc_ref[0]) + sh_ref[0]
    ub = u.astype(BF16)
    sw = IN_SLAB
    lb_f, lb_b = _lower_bound(lbf_ref, layer), _lower_bound(lbb_ref, layer)

    def slab(w0):
        return _dot(ub, w_ref[:, w0:w0 + sw])

    def rotated(x, c):
        return _rope(x, cos_ref[:, c:c + sw], sin_ref[:, c:c + sw]) if rope else x

    def q_slab(c):
        q_ref[0, :, c:c + sw] = rotated(slab(IN_Q0 + c), c)

    def k_slab(c):
        kv_ref[0, :, c:c + sw] = rotated(slab(IN_K0 + c), c)

    def v_slab(c):
        kv_ref[0, :, KV2_DIM + c:KV2_DIM + c + sw] = slab(IN_V0 + c)

    def hg_slab(src, c, fn, dsts):
        vals = fn(slab(IN_HG0 + src * HG_DIM + c))
        for j, val in zip(dsts, vals if isinstance(vals, tuple) else (vals,)):
            hg_ref[0, :, j * HG_DIM + c:j * HG_DIM + c + sw] = val

    def gate_slab(c):
        gab_ref[0, :, c:c + sw] = _sigmoid(slab(IN_GAB0 + c))

    heavy = []
    for c in range(0, HG_DIM, sw):
        heavy.append(functools.partial(hg_slab, 1, c, lambda z, c=c: _forget_gate(z, lb_f[:, c:c + sw]), (HGP_GF, HGP_KF)))
        heavy.append(functools.partial(hg_slab, 2, c, lambda z, c=c: _forget_gate(z, lb_b[:, c:c + sw]), (HGP_GB, HGP_KB)))
    light = [functools.partial(gate_slab, c) for c in range(0, gab_ref.shape[2], sw)]
    light += [functools.partial(q_slab, c) for c in range(0, ATTN_Q_DIM, sw)]
    for c in range(0, KV2_DIM, sw):
        light += [functools.partial(k_slab, c), functools.partial(v_slab, c)]
    for c in range(0, HG_DIM, sw):
        light.append(functools.partial(hg_slab, 0, c, _silu, (HGP_Q,)))
        light.append(functools.partial(hg_slab, 3, c, lambda v: v, (HGP_V,)))
        light.append(functools.partial(hg_slab, 4, c, _silu, (HGP_OG,)))
    per_heavy = len(light) // len(heavy)
    for i, task in enumerate(heavy):
        task()
        for t in light[i * per_heavy:(i + 1) * per_heavy]:
            t()
    for t in light[len(heavy) * per_heavy:]:
        t()


def _in_projection(x, sc, sh, w_in, lb_fw, lb_bw, layer, rope_tabs):
    b, l, d = x.shape
    tm = TOKEN_TILE
    per_batch = sc.shape[0] > 1
    mod_map = (lambda bi, i: (bi, 0, 0)) if per_batch else (lambda bi, i: (0, 0, 0))
    in_dim = w_in.shape[-1]
    in_specs = [
        pl.BlockSpec((1, tm, d), lambda bi, i: (bi, i, 0)),
        pl.BlockSpec((1, 1, d), mod_map),
        pl.BlockSpec((1, 1, d), mod_map),
        pl.BlockSpec((None, d, in_dim), lambda bi, i: (layer, 0, 0), pipeline_mode=pl.Buffered(1)),
        pl.BlockSpec(lb_fw.shape, lambda bi, i: (0, 0)),
        pl.BlockSpec(lb_bw.shape, lambda bi, i: (0, 0)),
    ]
    args = [x, sc, sh, w_in, lb_fw, lb_bw]
    if rope_tabs is not None:
        in_specs += [pl.BlockSpec((tm, ATTN_Q_DIM), lambda bi, i: (i, 0))] * 2
        args += list(rope_tabs)
    widths = (ATTN_Q_DIM, 2 * KV2_DIM, 7 * HG_DIM, 2 * d)
    return pl.pallas_call(
        functools.partial(_inproj_kernel, rope=rope_tabs is not None, layer=layer),
        grid=(b, l // tm),
        in_specs=in_specs,
        out_specs=[pl.BlockSpec((1, tm, wd), lambda bi, i: (bi, i, 0)) for wd in widths],
        out_shape=[jax.ShapeDtypeStruct((b, l, wd), F32) for wd in widths],
        compiler_params=_cparams("parallel", "arbitrary"),
        name="in_projection",
    )(*args)


def _relayout_w_in(w_in):
    q0, k0, v0, r0 = 0, ATTN_Q_DIM, ATTN_Q_DIM + ATTN_KV_DIM, ATTN_Q_DIM + 2 * ATTN_KV_DIM
    parts = [w_in[..., q0:k0]]
    for c0 in (k0, v0):
        for g in range(ATTN_KV_HEADS):
            head = w_in[..., c0 + g * HEAD_DIM:c0 + (g + 1) * HEAD_DIM]
            parts += [head, head]
    parts.append(w_in[..., r0:])
    return jnp.concatenate(parts, axis=-1).astype(BF16)


def _attend(q, kvs, sinks, ok):
    tq = q.shape[0]
    lc = kvs[0].shape[0]
    rep = ATTN_HEADS // ATTN_KV_HEADS
    tile = 2 * HEAD_DIM
    low_half = lax.broadcasted_iota(jnp.int32, (tq, tile), 1) < HEAD_DIM
    q = q * HEAD_DIM ** -0.5
    if ok is not None:
        mask = jnp.concatenate([jnp.ones((tq, lc), jnp.bool_), ok], axis=1)
        mask = jnp.concatenate([mask] * rep, axis=0)
    tiles = []
    for g in range(ATTN_KV_HEADS):
        kk = jnp.concatenate([t[:, g * tile:(g + 1) * tile] for t in kvs], axis=0).astype(BF16)
        vv = jnp.concatenate([t[:, KV2_DIM + g * tile:KV2_DIM + (g + 1) * tile] for t in kvs], axis=0).astype(BF16)
        heads = range(g * rep, (g + 1) * rep)
        qs = []
        for h in heads:
            q2 = q[:, (h // 2) * tile:(h // 2 + 1) * tile]
            qs.append(jnp.where(low_half == (h % 2 == 0), q2, 0.0))
        s = _dot_nt(jnp.concatenate(qs, axis=0).astype(BF16), kk)
        if ok is not None:
            s = jnp.where(mask, s, -jnp.inf)
        sink = jnp.concatenate([jnp.full((tq, 1), sinks[h], F32) for h in heads], axis=0)
        m = jnp.maximum(jnp.max(s, axis=1, keepdims=True), sink)
        e = jnp.exp(s - m)
        den = jnp.exp(sink - m) + jnp.sum(e, axis=1, keepdims=True)
        o = _dot(e.astype(BF16), vv) / den
        for r in range(0, rep, 2):
            tiles.append(jnp.where(low_half, o[r * tq:(r + 1) * tq], o[(r + 1) * tq:(r + 2) * tq]))
    return jnp.concatenate(tiles, axis=1)


def _attn_kernel(*refs, layer, nsteps, windowed):
    sinks = [refs[0][layer, h] for h in range(ATTN_HEADS)]
    if not windowed:
        _, q_ref, ckv_ref, o_ref = refs
        o_ref[0] = _attend(q_ref[0], [ckv_ref[0]], sinks, None).astype(BF16)
        return
    _, q_ref, kvp_ref, kvc_ref, kvn_ref, ckv_ref, o_ref = refs
    n = pl.program_id(1)
    ti = lax.broadcasted_iota(jnp.int32, (WBLOCK, 3 * WBLOCK), 0)
    sj = lax.broadcasted_iota(jnp.int32, (WBLOCK, 3 * WBLOCK), 1)
    rel = sj - WBLOCK - ti
    band = (rel <= WINDOW) & (rel >= -WINDOW)
    blocks = [kvp_ref[0]] + [kvc_ref[0, i * WBLOCK:(i + 1) * WBLOCK] for i in range(ATTN_STEP_BLOCKS)] + [kvn_ref[0]]
    for i in range(ATTN_STEP_BLOCKS):
        ok = band
        if i == 0:
            ok = ok & ((sj >= WBLOCK) | (n > 0))
        if i == ATTN_STEP_BLOCKS - 1:
            ok = ok & ((sj < 2 * WBLOCK) | (n < nsteps - 1))
        rows = slice(i * WBLOCK, (i + 1) * WBLOCK)
        o_ref[0, rows, :] = _attend(q_ref[0, rows, :], [ckv_ref[0]] + blocks[i:i + 3], sinks, ok).astype(BF16)


def _attention(q, kv, ckv, sink, layer):
    b, l, _ = q.shape
    lc = ckv.shape[1]
    windowed = kv is not None
    smem = pl.BlockSpec(memory_space=pltpu.SMEM)
    cspec = pl.BlockSpec((1, lc, 2 * KV2_DIM), lambda bi, n: (bi, 0, 0))
    if windowed:
        kvw = 2 * KV2_DIM
        sb = ATTN_STEP_BLOCKS
        tq = sb * WBLOCK
        nb = l // WBLOCK
        in_specs = [
            smem, pl.BlockSpec((1, tq, ATTN_Q_DIM), lambda bi, n: (bi, n, 0)),
            pl.BlockSpec((1, WBLOCK, kvw), lambda bi, n: (bi, jnp.maximum(n * sb - 1, 0), 0)),
            pl.BlockSpec((1, tq, kvw), lambda bi, n: (bi, n, 0)),
            pl.BlockSpec((1, WBLOCK, kvw), lambda bi, n: (bi, jnp.minimum((n + 1) * sb, nb - 1), 0)),
            cspec,
        ]
        args = (sink, q, kv, kv, kv, ckv)
    else:
        tq = WBLOCK
        in_specs = [smem, pl.BlockSpec((1, tq, ATTN_Q_DIM), lambda bi, n: (bi, n, 0)), cspec]
        args = (sink, q, ckv)
    return pl.pallas_call(
        functools.partial(_attn_kernel, layer=layer, nsteps=l // tq, windowed=windowed),
        grid=(b, l // tq),
        in_specs=in_specs,
        out_specs=pl.BlockSpec((1, tq, ATTN_Q_DIM), lambda bi, n: (bi, n, 0)),
        out_shape=jax.ShapeDtypeStruct((b, l, ATTN_Q_DIM), BF16),
        compiler_params=_cparams("parallel", "arbitrary"),
        name="window_attention" if windowed else "context_attention",
    )(*args)


def _block_ref_rows(cum, blk, r):
    c, w = cum.shape
    if blk % 8 == 0:
        x = cum.reshape(c // blk, blk, w)
        return jnp.broadcast_to(x[:, r:r + 1, :], x.shape).reshape(c, w)
    pos = lax.broadcasted_iota(jnp.int32, cum.shape, 0) & (blk - 1)
    out = cum
    for p in range(blk):
        delta = r - p
        if delta == 0:
            continue
        out = jnp.where(pos == p, pltpu.roll(cum, (-delta) % c, 0), out)
    return out


def _hgrn_kernel(*refs, has_init):
    if has_init:
        (tri_ref, lm_ref, dm_ref, qf_ref, gf_ref, kf_ref, vf_ref, qb_ref, gb_ref, kb_ref, vb_ref, s0_ref,
         of_ref, ob_ref, s_ref, cum_ref) = refs
    else:
        (tri_ref, lm_ref, dm_ref, qf_ref, gf_ref, kf_ref, vf_ref, qb_ref, gb_ref, kb_ref, vb_ref,
         of_ref, ob_ref, s_ref, cum_ref) = refs

    @pl.when(pl.program_id(1) == 0)
    def _():
        if has_init:
            s_ref[...] = s0_ref[...]
        else:
            s_ref[...] = jnp.zeros(s_ref.shape, F32)

    tt = qf_ref.shape[1]
    c = tri_ref.shape[1]
    nch = tt // c
    row = lax.broadcasted_iota(jnp.int32, (c, HG_DK), 0)
    levels = []
    hs = c // 2
    while hs >= 1:
        levels.append(hs)
        hs //= 2
    dirs = ((qf_ref, gf_ref, kf_ref, vf_ref, of_ref), (qb_ref, gb_ref, kb_ref, vb_ref, ob_ref))

    def direct_dd(cum, backward):
        return cum - _block_ref_rows(cum, HG_DIRECT, HG_DIRECT - 1 if backward else 0)

    worst = jnp.zeros((c, HG_DK), F32)
    for d, (_, g_ref, _, _, _) in enumerate(dirs):
        for h in range(HG_HEADS):
            for ci in range(nch):
                g3 = _split3(g_ref[0, ci * c:(ci + 1) * c, h * HG_DK:(h + 1) * HG_DK])
                cum = _dot(tri_ref[d], g3[0]) + _dot(tri_ref[d], g3[1]) + _dot(tri_ref[d], g3[2])
                cum_ref[(d * HG_HEADS + h) * nch + ci] = cum
                worst = jnp.maximum(worst, -direct_dd(cum, d == 1))
    in_range = jnp.max(worst) <= HG_EXP_LIMIT

    def scan(direct):
        tree = [(li, hs) for li, hs in enumerate(levels) if not direct or hs >= HG_DIRECT]
        for d, (q_ref, g_ref, k_ref, v_ref, o_ref) in enumerate(dirs):
            backward = d == 1
            q_halves = {li: ((row & (2 * hs - 1)) >= hs) != backward for li, hs in tree}
            for h in range(HG_HEADS):
                sl = slice(h * HG_DK, (h + 1) * HG_DK)
                q_all, k_all, v_all = q_ref[0, :, sl], k_ref[0, :, sl], v_ref[0, :, sl]
                st = s_ref[0, d, h]
                chunks = range(nch)
                for ci in (reversed(chunks) if backward else chunks):
                    rows = slice(ci * c, (ci + 1) * c)
                    q, kk, v = q_all[rows], k_all[rows], v_all[rows]
                    qb, kb, vb = q.astype(BF16), kk.astype(BF16), v.astype(BF16)
                    cum = cum_ref[(d * HG_HEADS + h) * nch + ci]
                    o = _dot_nt((q * jnp.exp(cum)).astype(BF16), st.astype(BF16))
                    if not direct:
                        o = o + jnp.sum(q * kk, axis=1, keepdims=True) * v
                    scores = jnp.zeros((c, c), F32)
                    for li, hs in tree:
                        blk = 2 * hs
                        dd = cum - _block_ref_rows(cum, blk, hs if backward else hs - 1)
                        e = jnp.exp(-jnp.abs(dd)).astype(BF16)
                        x = jnp.where(q_halves[li], qb, kb) * e
                        scores = scores + _dot_nt(x, x) * lm_ref[d, li]
                    if direct:
                        dd = direct_dd(cum, backward)
                        xq = qb * jnp.exp(dd).astype(BF16)
                        xk = kb * jnp.exp(-dd).astype(BF16)
                        scores = scores + jnp.where(dm_ref[d] > 0.5, _dot_nt(xq, xk), 0.0)
                    o = o + _dot(scores.astype(BF16), vb)
                    o_ref[0, rows, sl] = o
                    tot = cum[0:1] if backward else cum[c - 1:c]
                    kdec = kk * jnp.exp(tot - cum)
                    st = st * jnp.exp(tot) + _dot_tn(vb, kdec.astype(BF16))
                s_ref[0, d, h] = st

    pl.when(in_range)(lambda: scan(True))
    pl.when(jnp.logical_not(in_range))(lambda: scan(False))


def _hgrn_scan(hgp, consts, s0):
    b, l, _ = hgp.shape
    tt = TOKEN_TILE
    ns = l // tt
    has_init = s0 is not None
    tri, lmask, dmask = consts
    fwd = lambda j: pl.BlockSpec((1, tt, HG_DIM), lambda bi, i: (bi, i, j))
    bwd = lambda j: pl.BlockSpec((1, tt, HG_DIM), lambda bi, i: (bi, ns - 1 - i, j))
    sspec = pl.BlockSpec((1, 2, HG_HEADS, HG_DV, HG_DK), lambda bi, i: (bi, 0, 0, 0, 0))
    in_specs = [
        pl.BlockSpec(tri.shape, lambda bi, i: (0, 0, 0)),
        pl.BlockSpec(lmask.shape, lambda bi, i: (0, 0, 0, 0)),
        pl.BlockSpec(dmask.shape, lambda bi, i: (0, 0, 0)),
        fwd(HGP_Q), fwd(HGP_GF), fwd(HGP_KF), fwd(HGP_V), bwd(HGP_Q), bwd(HGP_GB), bwd(HGP_KB), bwd(HGP_V),
    ]
    args = [tri, lmask, dmask] + [hgp] * 8
    if has_init:
        in_specs.append(sspec)
        args.append(s0)
    c = tri.shape[1]
    return pl.pallas_call(
        functools.partial(_hgrn_kernel, has_init=has_init),
        grid=(b, ns),
        in_specs=in_specs,
        out_specs=[
            pl.BlockSpec((1, tt, HG_DIM), lambda bi, i: (bi, i, 0)),
            pl.BlockSpec((1, tt, HG_DIM), lambda bi, i: (bi, ns - 1 - i, 0)),
            sspec,
        ],
        out_shape=[
            jax.ShapeDtypeStruct((b, l, HG_DIM), F32),
            jax.ShapeDtypeStruct((b, l, HG_DIM), F32),
            jax.ShapeDtypeStruct((b, 2, HG_HEADS, HG_DV, HG_DK), F32),
        ],
        scratch_shapes=[pltpu.VMEM((2 * HG_HEADS * (tt // c), c, HG_DK), F32)],
        compiler_params=_cparams("parallel", "arbitrary"),
        name="hgrn_scan",
    )(*args)


def _hgrn_constants(c):
    t = lax.broadcasted_iota(jnp.int32, (c, c), 0)
    s = lax.broadcasted_iota(jnp.int32, (c, c), 1)
    tri = jnp.stack([s <= t, s >= t]).astype(BF16)
    masks, direct = [], []
    for backward in (False, True):
        per_level = []
        hs = c // 2
        while hs >= 1:
            blk = 2 * hs
            same = (t // blk) == (s // blk)
            t_late, s_late = (t % blk) >= hs, (s % blk) >= hs
            pair = (jnp.logical_not(t_late) & s_late) if backward else (t_late & jnp.logical_not(s_late))
            per_level.append(same & pair)
            hs //= 2
        masks.append(jnp.stack(per_level))
        direct.append(((t // HG_DIRECT) == (s // HG_DIRECT)) & ((s >= t) if backward else (s <= t)))
    return tri, jnp.stack(masks).astype(F32), jnp.stack(direct).astype(F32)


def _mixout_kernel(attn_ref, of_ref, ob_ref, gh_ref, ga_ref, gb_ref, x_ref, g1_ref, sc2_ref, sh2_ref,
                   ng_ref, wa_ref, wh_ref, wo_ref, lng_ref, lnb_ref, wrh_ref, wrl_ref,
                   xo_ref, u2_ref, aff_ref, *, alpha):
    o = of_ref[0] + ob_ref[0]
    parts = []
    for h in range(HG_HEADS):
        oh = o[:, h * HG_DV:(h + 1) * HG_DV]
        parts.append(oh * lax.rsqrt(jnp.mean(oh * oh, axis=-1, keepdims=True) + LN_EPS))
    hg = jnp.concatenate(parts, axis=1) * ng_ref[...] * gh_ref[0]
    ya = _dot(attn_ref[0], wa_ref[...])
    yh = _dot(hg.astype(BF16), wh_ref[...])
    mix = ga_ref[0] * ya + gb_ref[0] * yh
    y = _dot(mix.astype(BF16), wo_ref[...])
    xn = _layer_norm(alpha * x_ref[0] + g1_ref[0] * y) * lng_ref[...] + lnb_ref[...]
    xo_ref[0] = xn
    u2 = _layer_norm(xn) * (1.0 + sc2_ref[0]) + sh2_ref[0]
    u2_ref[0] = u2.astype(BF16)
    uh, ul = _split2(u2)
    lg = _dot(uh, wrh_ref[...]) + _dot(ul, wrh_ref[...]) + _dot(uh, wrl_ref[...])
    lg = jnp.where(lax.broadcasted_iota(jnp.int32, lg.shape, 1) < N_EXPERTS, lg, -jnp.inf)
    ex = jnp.exp(lg - jnp.max(lg, axis=1, keepdims=True))
    aff = ex / jnp.sum(ex, axis=1, keepdims=True)
    aff_ref[0] = aff.T[0:N_EXPERTS, :]


def _mixer_output(attn, o_f, o_b, hgp, gab, x, g1, sc2, sh2, norm_g, w_a, w_h, w_o, ln_g, ln_b,
                  wr_hi, wr_lo, layer, alpha):
    b, l, d = x.shape
    tm = TOKEN_TILE
    per_batch = g1.shape[0] > 1
    mod_map = (lambda bi, i: (bi, 0, 0)) if per_batch else (lambda bi, i: (0, 0, 0))
    tok = lambda w, j=0: pl.BlockSpec((1, tm, w), lambda bi, i: (bi, i, j))
    lay2 = lambda r, c: pl.BlockSpec((None, r, c), lambda bi, i: (layer, 0, 0))
    in_specs = [
        tok(ATTN_Q_DIM), tok(HG_DIM), tok(HG_DIM), tok(HG_DIM, HGP_OG), tok(d, 0), tok(d, 1), tok(d),
        pl.BlockSpec((1, 1, d), mod_map), pl.BlockSpec((1, 1, d), mod_map), pl.BlockSpec((1, 1, d), mod_map),
        lay2(1, HG_DIM), lay2(ATTN_Q_DIM, d), lay2(HG_DIM, d), lay2(d, d),
        pl.BlockSpec((None, None, 1, d), lambda bi, i: (layer, 0, 0, 0)),
        pl.BlockSpec((None, None, 1, d), lambda bi, i: (layer, 0, 0, 0)),
        lay2(d, ROUTER_LANES), lay2(d, ROUTER_LANES),
    ]
    return pl.pallas_call(
        functools.partial(_mixout_kernel, alpha=alpha),
        grid=(b, l // tm),
        in_specs=in_specs,
        out_specs=[tok(d), tok(d), pl.BlockSpec((1, N_EXPERTS, tm), lambda bi, i: (bi, 0, i))],
        out_shape=[
            jax.ShapeDtypeStruct((b, l, d), F32),
            jax.ShapeDtypeStruct((b, l, d), BF16),
            jax.ShapeDtypeStruct((b, N_EXPERTS, l), F32),
        ],
        compiler_params=_cparams("parallel", "arbitrary"),
        name="mixer_output",
    )(attn, o_f, o_b, hgp, gab, gab, x, g1, sc2, sh2, norm_g, w_a, w_h, w_o, ln_g, ln_b, wr_hi, wr_lo)


def _select_kernel(aff_ref, triu_ref, rk_ref, rkt_ref, st_ref, *, cap):
    a = aff_ref[0]
    ne, n = a.shape
    ts = triu_ref.shape[0]
    bits = lax.bitcast_convert_type(a, jnp.int32)
    capf = jnp.float32(cap)

    def count_ge(v):
        return jnp.sum(jnp.where(bits >= v, 1.0, 0.0), axis=1, keepdims=True)

    def body(_, carry):
        lo, hi = carry
        mid = lo + ((hi - lo) >> 1)
        ok = count_ge(mid) >= capf
        return jnp.where(ok, mid, lo), jnp.where(ok, hi, mid)

    lo0 = jnp.zeros((ne, 1), jnp.int32)
    hi0 = jnp.full((ne, 1), 0x7F800000, jnp.int32)
    thr, _ = lax.fori_loop(0, 31, body, (lo0, hi0))
    gt = bits > thr
    eq = bits == thr
    need = capf - jnp.sum(jnp.where(gt, 1.0, 0.0), axis=1, keepdims=True)
    triu = triu_ref[...]
    lane = lax.broadcasted_iota(jnp.int32, st_ref.shape[1:], 1)
    eq_carry = jnp.zeros((ne, 1), F32)
    sel_carry = jnp.zeros((ne, 1), F32)
    starts = jnp.zeros(st_ref.shape[1:], F32)
    for t in range(n // ts):
        sl = slice(t * ts, (t + 1) * ts)
        eq_t = eq[:, sl]
        eq_pref = _dot(jnp.where(eq_t, 1.0, 0.0).astype(BF16), triu) + eq_carry
        sel_t = gt[:, sl] | (eq_t & (eq_pref <= need))
        sel_f = jnp.where(sel_t, 1.0, 0.0)
        sel_pref = _dot(sel_f.astype(BF16), triu) + sel_carry
        rk_t = jnp.where(sel_t, sel_pref - 1.0, -1.0)
        rk_ref[0, :, sl] = rk_t
        rkt_ref[0, sl, :] = rk_t.T
        starts = jnp.where(lane == t, sel_carry, starts)
        eq_carry = eq_pref[:, ts - 1:ts]
        sel_carry = sel_pref[:, ts - 1:ts]
    st_ref[0] = starts


def _select(aff, triu, cap):
    b, ne, n = aff.shape
    return pl.pallas_call(
        functools.partial(_select_kernel, cap=cap),
        grid=(b,),
        in_specs=[
            pl.BlockSpec((1, ne, n), lambda bi: (bi, 0, 0)),
            pl.BlockSpec(triu.shape, lambda bi: (0, 0)),
        ],
        out_specs=[
            pl.BlockSpec((1, ne, n), lambda bi: (bi, 0, 0)),
            pl.BlockSpec((1, n, ne), lambda bi: (bi, 0, 0)),
            pl.BlockSpec((1, ne, 128), lambda bi: (bi, 0, 0)),
        ],
        out_shape=[
            jax.ShapeDtypeStruct((b, ne, n), F32),
            jax.ShapeDtypeStruct((b, n, ne), F32),
            jax.ShapeDtypeStruct((b, ne, 128), F32),
        ],
        compiler_params=_cparams("parallel"),
        name="expert_choice_select",
    )(aff, triu)


def _slot_window(cap, tile):
    return min(cap, tile + SLOT_ALIGN)


def _aligned_start(s0, cap, w):
    return jnp.minimum((s0 // SLOT_ALIGN) * SLOT_ALIGN, cap - w)


def _window_start(s0, cap, tile):
    return _aligned_start(s0, cap, _slot_window(cap, tile))


def _ffn_kernel(st_ref, u_ref, rk_ref, aff_ref, wg_ref, wu_ref, wd_ref, y_ref, xs_ref,
                *, cap, ns, unroll, expert_major):
    b, e = pl.program_id(0), pl.program_id(1)
    if expert_major:
        b, e = e, b
    d = u_ref.shape[2]
    nt = rk_ref.shape[2]
    w_full = _slot_window(cap, TOKEN_TILE)
    widths = [w for w in DISPATCH_WINDOWS if w < w_full] + [w_full]
    sub = TOKEN_TILE // ROUTE_TILE
    base = (b * N_EXPERTS + e) * (ns + 1)

    def bounds(t):
        return st_ref[base + t * sub], st_ref[base + (t + 1) * sub]

    def dispatch(t, w):
        s0, s1 = bounds(t)
        a = pl.multiple_of(_aligned_start(s0, cap, w), SLOT_ALIGN)
        slot = a + lax.broadcasted_iota(jnp.int32, (w, 1), 0)
        hit = slot.astype(F32) == rk_ref[0, 0, pl.ds(t, 1), :]
        mine = (slot >= s0) & (slot < s1)
        tok0 = pl.multiple_of(t * TOKEN_TILE, TOKEN_TILE)
        rows = _dot(jnp.where(hit, 1.0, 0.0).astype(BF16), u_ref[0, pl.ds(tok0, TOKEN_TILE), :])
        wts = jnp.sum(jnp.where(hit, aff_ref[0, 0, pl.ds(t, 1), :], 0.0), axis=1, keepdims=True)
        vals = jnp.concatenate([rows, jnp.broadcast_to(wts, (w, 128))], axis=1)
        pltpu.store(xs_ref.at[pl.ds(a, w), :], vals, mask=jnp.broadcast_to(mine, vals.shape))

    def run(w):
        def group(i, carry):
            for j in range(unroll):
                dispatch(i * unroll + j, w)
            return carry

        lax.fori_loop(0, nt // unroll, group, 0)

    def fits(t, ok):
        s0, s1 = bounds(t)
        return tuple(o & (s1 - _aligned_start(s0, cap, w) <= w) for o, w in zip(ok, widths[:-1]))

    ok = lax.fori_loop(0, nt, fits, (True,) * (len(widths) - 1))
    wider_needed = True
    for i, w in enumerate(widths):
        fit = ok[i] if i < len(ok) else True
        pl.when(jnp.logical_and(wider_needed, fit))(functools.partial(run, w))
        wider_needed = jnp.logical_and(wider_needed, jnp.logical_not(fit))

    rc = min(cap, 256)
    for c in range(cap // rc):
        rows = slice(c * rc, (c + 1) * rc)
        xb = xs_ref[rows, 0:d].astype(BF16)
        h = _silu(_dot(xb, wg_ref[0])) * _dot(xb, wu_ref[0])
        y_ref[0, 0, rows, :] = (_dot(h.astype(BF16), wd_ref[0]) * xs_ref[rows, d:d + 1]).astype(BF16)


def _expert_ffn(starts, u2, rk, aff, w_gate, w_up, w_down, layer, cap):
    b, n, d = u2.shape
    ff = w_gate.shape[-1]
    ns = n // ROUTE_TILE
    nt = n // TOKEN_TILE
    rk4 = rk.reshape(b, N_EXPERTS, nt, TOKEN_TILE)
    aff4 = aff.reshape(b, N_EXPERTS, nt, TOKEN_TILE)
    expert_major = n * d < 3 * d * ff
    grid = (N_EXPERTS, b) if expert_major else (b, N_EXPERTS)
    be = (lambda i, j: (j, i)) if expert_major else (lambda i, j: (i, j))
    wspec = lambda r, c: pl.BlockSpec((None, 1, r, c), lambda i, j, st: (layer, be(i, j)[1], 0, 0))
    rspec = pl.BlockSpec((1, 1, nt, TOKEN_TILE), lambda i, j, st: (*be(i, j), 0, 0))
    grid_spec = pltpu.PrefetchScalarGridSpec(
        num_scalar_prefetch=1,
        grid=grid,
        in_specs=[
            pl.BlockSpec((1, n, d), lambda i, j, st: (be(i, j)[0], 0, 0),
                         pipeline_mode=None if expert_major else pl.Buffered(1)),
            rspec, rspec,
            wspec(d, ff), wspec(d, ff), wspec(ff, d),
        ],
        out_specs=pl.BlockSpec((1, 1, cap, d), lambda i, j, st: (*be(i, j), 0, 0)),
        scratch_shapes=[pltpu.VMEM((cap, d + 128), F32)],
    )
    return pl.pallas_call(
        functools.partial(_ffn_kernel, cap=cap, ns=ns, unroll=min(nt, 4), expert_major=expert_major),
        grid_spec=grid_spec,
        out_shape=jax.ShapeDtypeStruct((b, N_EXPERTS, cap, d), BF16),
        compiler_params=_cparams("parallel", "arbitrary"),
        name="expert_ffn",
    )(starts, u2, rk4, aff4, w_gate, w_up, w_down)


def _combine_kernel(*refs, cap, ns, alpha):
    st_ref, rkt_ref, x_ref, g2_ref, lng_ref, lnb_ref = refs[:6]
    y_refs = refs[6:6 + N_EXPERTS]
    o_ref = refs[6 + N_EXPERTS]
    b, t = pl.program_id(0), pl.program_id(1)
    sub = TOKEN_TILE // ROUTE_TILE
    w = _slot_window(cap, ROUTE_TILE)
    for half in range(sub):
        rows = slice(half * ROUTE_TILE, (half + 1) * ROUTE_TILE)
        acc = jnp.zeros((ROUTE_TILE, x_ref.shape[2]), F32)
        for e in range(N_EXPERTS):
            base = (b * N_EXPERTS + e) * (ns + 1) + t * sub
            a_tile = _window_start(st_ref[base], cap, TOKEN_TILE)
            a = _window_start(st_ref[base + half], cap, ROUTE_TILE)
            off = pl.multiple_of(a - a_tile, SLOT_ALIGN)
            slot = (a + lax.broadcasted_iota(jnp.int32, (1, w), 1)).astype(F32)
            hit = rkt_ref[0, rows, e:e + 1] == slot
            acc = acc + _dot(jnp.where(hit, 1.0, 0.0).astype(BF16), y_refs[e][pl.ds(off, w), :])
        xr = alpha * x_ref[0, rows, :] + g2_ref[0] * acc
        o_ref[0, rows, :] = _layer_norm(xr) * lng_ref[...] + lnb_ref[...]


def _combine(starts, rkt, x, g2, ln_g, ln_b, y, layer, cap, alpha):
    b, n, d = x.shape
    tt = TOKEN_TILE
    ns = n // ROUTE_TILE
    sub = TOKEN_TILE // ROUTE_TILE
    w = _slot_window(cap, tt)
    per_batch = g2.shape[0] > 1
    mod_map = (lambda bi, t, st: (bi, 0, 0)) if per_batch else (lambda bi, t, st: (0, 0, 0))

    def y_spec(e):
        def index_map(bi, t, st):
            be = bi * N_EXPERTS + e
            a = _window_start(st[be * (ns + 1) + t * sub], cap, tt)
            return pl.multiple_of(be * cap + a, SLOT_ALIGN), 0
        return pl.BlockSpec((pl.Element(w), pl.Element(d)), index_map)

    y = y.reshape(b * N_EXPERTS * cap, d)
    grid_spec = pltpu.PrefetchScalarGridSpec(
        num_scalar_prefetch=1,
        grid=(b, n // tt),
        in_specs=[
            pl.BlockSpec((1, tt, N_EXPERTS), lambda bi, t, st: (bi, t, 0)),
            pl.BlockSpec((1, tt, d), lambda bi, t, st: (bi, t, 0)),
            pl.BlockSpec((1, 1, d), mod_map),
            pl.BlockSpec((None, None, 1, d), lambda bi, t, st: (layer, 1, 0, 0)),
            pl.BlockSpec((None, None, 1, d), lambda bi, t, st: (layer, 1, 0, 0)),
        ] + [y_spec(e) for e in range(N_EXPERTS)],
        out_specs=pl.BlockSpec((1, tt, d), lambda bi, t, st: (bi, t, 0)),
    )
    return pl.pallas_call(
        functools.partial(_combine_kernel, cap=cap, ns=ns, alpha=alpha),
        grid_spec=grid_spec,
        out_shape=jax.ShapeDtypeStruct((b, n, d), F32),
        compiler_params=_cparams("parallel", "arbitrary"),
        name="moe_combine",
    )(starts, rkt, x, g2, ln_g, ln_b, *([y] * N_EXPERTS))


def _combine_packed_kernel(st_ref, rkt_ref, x_ref, g2_ref, lng_ref, lnb_ref, y_hbm, o_ref, ybuf, sem,
                           *, cap, ns, alpha):
    b, t = pl.program_id(0), pl.program_id(1)
    nt = pl.num_programs(1)
    step = b * nt + t
    sub = TOKEN_TILE // ROUTE_TILE
    w = COMBINE_WINDOW

    def window_start(bb, tt, half, e):
        return _aligned_start(st_ref[(bb * N_EXPERTS + e) * (ns + 1) + tt * sub + half], cap, w)

    def copies(bb, tt, slot):
        out = []
        for half in range(sub):
            for e in range(N_EXPERTS):
                row0 = pl.multiple_of((bb * N_EXPERTS + e) * cap + window_start(bb, tt, half, e), SLOT_ALIGN)
                out.append(pltpu.make_async_copy(y_hbm.at[pl.ds(row0, w), :],
                                                 ybuf.at[slot, half, pl.ds(e * w, w), :], sem.at[slot]))
        return out

    slot = step % 2

    @pl.when(step == 0)
    def _():
        for cp in copies(b, t, 0):
            cp.start()

    @pl.when(step + 1 < pl.num_programs(0) * nt)
    def _():
        wrap = t + 1 == nt
        for cp in copies(jnp.where(wrap, b + 1, b), jnp.where(wrap, 0, t + 1), 1 - slot):
            cp.start()

    for cp in copies(b, t, slot):
        cp.wait()

    lane = lax.broadcasted_iota(jnp.int32, (1, w), 1)
    for half in range(sub):
        rows = slice(half * ROUTE_TILE, (half + 1) * ROUTE_TILE)
        hits = []
        for e in range(N_EXPERTS):
            slot_ids = (window_start(b, t, half, e) + lane).astype(F32)
            hits.append(jnp.where(rkt_ref[0, rows, e:e + 1] == slot_ids, 1.0, 0.0).astype(BF16))
        acc = _dot(jnp.concatenate(hits, axis=1), ybuf[slot, half])
        xr = alpha * x_ref[0, rows, :] + g2_ref[0] * acc
        o_ref[0, rows, :] = _layer_norm(xr) * lng_ref[...] + lnb_ref[...]


def _combine_packed(starts, rkt, x, g2, ln_g, ln_b, y, layer, cap, alpha):
    b, n, d = x.shape
    tt = TOKEN_TILE
    ns = n // ROUTE_TILE
    sub = TOKEN_TILE // ROUTE_TILE
    per_batch = g2.shape[0] > 1
    mod_map = (lambda bi, t, st: (bi, 0, 0)) if per_batch else (lambda bi, t, st: (0, 0, 0))
    grid_spec = pltpu.PrefetchScalarGridSpec(
        num_scalar_prefetch=1,
        grid=(b, n // tt),
        in_specs=[
            pl.BlockSpec((1, tt, N_EXPERTS), lambda bi, t, st: (bi, t, 0)),
            pl.BlockSpec((1, tt, d), lambda bi, t, st: (bi, t, 0)),
            pl.BlockSpec((1, 1, d), mod_map),
            pl.BlockSpec((None, None, 1, d), lambda bi, t, st: (layer, 1, 0, 0)),
            pl.BlockSpec((None, None, 1, d), lambda bi, t, st: (layer, 1, 0, 0)),
            pl.BlockSpec(memory_space=pl.ANY),
        ],
        out_specs=pl.BlockSpec((1, tt, d), lambda bi, t, st: (bi, t, 0)),
        scratch_shapes=[
            pltpu.VMEM((2, sub, N_EXPERTS * COMBINE_WINDOW, d), BF16),
            pltpu.SemaphoreType.DMA((2,)),
        ],
    )
    return pl.pallas_call(
        functools.partial(_combine_packed_kernel, cap=cap, ns=ns, alpha=alpha),
        grid_spec=grid_spec,
        out_shape=jax.ShapeDtypeStruct((b, n, d), F32),
        compiler_params=_cparams("arbitrary", "arbitrary"),
        name="moe_combine_packed",
    )(starts, rkt, x, g2, ln_g, ln_b, y.reshape(b * N_EXPERTS * cap, d))


def _moe(x, u2, aff, g2, ln_g, ln_b, triu, w_gate, w_up, w_down, layer, alpha):
    b, n, _ = x.shape
    cap = CAPACITY_FACTOR * n // N_EXPERTS
    ns = n // ROUTE_TILE
    rk, rkt, st = _select(aff, triu, cap)
    starts = jnp.concatenate([st[:, :, :ns], jnp.full((b, N_EXPERTS, 1), cap, F32)], axis=2)
    starts = starts.astype(jnp.int32).reshape(-1)
    y = _expert_ffn(starts, u2, rk, aff, w_gate, w_up, w_down, layer, cap)
    args = (starts, rkt, x, g2, ln_g, ln_b, y)
    general = lambda *a: _combine(*a, layer, cap, alpha)
    if cap < COMBINE_WINDOW:
        return general(*args)
    s0 = starts.reshape(b, N_EXPERTS, ns + 1)
    fits = jnp.all(s0[:, :, 1:] - _aligned_start(s0[:, :, :-1], cap, COMBINE_WINDOW) <= COMBINE_WINDOW)
    return lax.cond(fits, lambda *a: _combine_packed(*a, layer, cap, alpha), general, *args)


def _rope_tables(l):
    pos = jnp.arange(l)
    nf = HEAD_DIM // 4
    inv = ROPE_THETA ** (-jnp.arange(nf, dtype=F32) / nf)
    ang_r = (pos // GRID_W).astype(F32)[:, None] * inv[None, :]
    ang_c = (pos % GRID_W).astype(F32)[:, None] * inv[None, :]
    cos = jnp.concatenate([jnp.cos(ang_r)] * 2 + [jnp.cos(ang_c)] * 2, axis=1)
    sin = jnp.concatenate([-jnp.sin(ang_r), jnp.sin(ang_r), -jnp.sin(ang_c), jnp.sin(ang_c)], axis=1)
    return jnp.tile(cos, (1, ATTN_HEADS)), jnp.tile(sin, (1, ATTN_HEADS))


def kernel(x, c, ctx, c_ctx, w_mod, b_mod, w_in, attn_sink, hgrn_lb_fw, hgrn_lb_bw, hgrn_norm_g,
           w_branch_attn, w_branch_hgrn, w_out, w_router, w_gate, w_up, w_down, ln_g, ln_b):
    depth, d = w_mod.shape[0], x.shape[-1]
    b = x.shape[0]
    assert x.shape[1] % (4 * TOKEN_TILE) == 0 and ctx.shape[1] == TOKEN_TILE
    alpha = (2 * depth) ** 0.25
    mod_rows = -(-(b + 1) // 8) * 8
    c_all = jnp.concatenate([c, c_ctx[None, :], jnp.zeros((mod_rows - b - 1, d), F32)], axis=0)
    b_mod3 = b_mod[:, None, :]
    norm_g3 = hgrn_norm_g[:, None, :]
    ln_g4, ln_b4 = ln_g[:, :, None, :], ln_b[:, :, None, :]
    w_in_b = _relayout_w_in(w_in)
    w_a_b, w_h_b, w_o_b = w_branch_attn.astype(BF16), w_branch_hgrn.astype(BF16), w_out.astype(BF16)
    w_g_b, w_u_b, w_d_b = w_gate.astype(BF16), w_up.astype(BF16), w_down.astype(BF16)
    wr_pad = jnp.pad(w_router, ((0, 0), (0, 0), (0, ROUTER_LANES - N_EXPERTS)))
    wr_hi = wr_pad.astype(BF16)
    wr_lo = (wr_pad - wr_hi.astype(F32)).astype(BF16)
    rope_tabs = _rope_tables(x.shape[1])
    hg_consts = _hgrn_constants(HG_CHUNK)
    r_i = lax.broadcasted_iota(jnp.int32, (ROUTE_TILE, ROUTE_TILE), 0)
    c_i = lax.broadcasted_iota(jnp.int32, (ROUTE_TILE, ROUTE_TILE), 1)
    triu = (r_i <= c_i).astype(BF16)

    xc = ctx
    for l in range(depth):
        need_ctx = l < depth - 1
        mod = _modulation(c_all, w_mod, b_mod3, l)
        sh1, sc1, g1, sh2, sc2, g2 = [mod[:b, i * d:(i + 1) * d][:, None, :] for i in range(N_MOD)]
        sh1c, sc1c, g1c, sh2c, sc2c, g2c = [mod[b:b + 1, i * d:(i + 1) * d][:, None, :] for i in range(N_MOD)]
        cq, ckv, chg, cgab = _in_projection(xc, sc1c, sh1c, w_in_b, hgrn_lb_fw, hgrn_lb_bw, l, None)
        oc_f, oc_b, s_ctx = _hgrn_scan(chg, hg_consts, None)
        q, kv, hgp, gab = _in_projection(x, sc1, sh1, w_in_b, hgrn_lb_fw, hgrn_lb_bw, l, rope_tabs)
        o_f, o_b, _ = _hgrn_scan(hgp, hg_consts, s_ctx)
        attn = _attention(q, kv, ckv, attn_sink, l)
        x1, u2, aff = _mixer_output(attn, o_f, o_b, hgp, gab, x, g1, sc2, sh2, norm_g3, w_a_b, w_h_b, w_o_b,
                                    ln_g4, ln_b4, wr_hi, wr_lo, l, alpha)
        x = _moe(x1, u2, aff, g2, ln_g4, ln_b4, triu, w_g_b, w_u_b, w_d_b, l, alpha)
        if need_ctx:
            attn_c = _attention(cq, None, ckv, attn_sink, l)
            xc1, uc2, affc = _mixer_output(attn_c, oc_f, oc_b, chg, cgab, xc, g1c, sc2c, sh2c, norm_g3,
                                           w_a_b, w_h_b, w_o_b, ln_g4, ln_b4, wr_hi, wr_lo, l, alpha)
            xc = _moe(xc1, uc2, affc, g2c, ln_g4, ln_b4, triu, w_g_b, w_u_b, w_d_b, l, alpha)
    return x
```

```python
import functools

import jax
import jax.numpy as jnp
from jax import lax
from jax.experimental import pallas as pl
from jax.experimental.pallas import tpu as pltpu

F32 = jnp.float32
BF16 = jnp.bfloat16

GRID_W = 64
ATTN_HEADS = 8
ATTN_KV_HEADS = 2
HEAD_DIM = 64
WINDOW = 128
WBLOCK = 128
ROPE_THETA = 10000.0
HG_HEADS = 4
HG_DK = 128
HG_DV = 128
N_EXPERTS = 16
CAPACITY_FACTOR = 2
N_MOD = 6
LN_EPS = 1e-6
ATTN_Q_DIM = ATTN_HEADS * HEAD_DIM
ATTN_KV_DIM = ATTN_KV_HEADS * HEAD_DIM
HG_DIM = HG_HEADS * HG_DK

TOKEN_TILE = 256
HG_CHUNK = 128
ATTN_STEP_BLOCKS = 2
HG_DIRECT = 16
HG_EXP_LIMIT = 80.0
ROUTE_TILE = 128
SLOT_ALIGN = 16
DISPATCH_WINDOWS = (128,)
COMBINE_WINDOW = 64
ROUTER_LANES = 128
VMEM_LIMIT_BYTES = 56 << 20


def _cparams(*sem):
    return pltpu.CompilerParams(dimension_semantics=sem, vmem_limit_bytes=VMEM_LIMIT_BYTES)


def _dot(a, b):
    return jnp.dot(a, b, preferred_element_type=F32)


def _dot_nt(a, b):
    return lax.dot_general(a, b, (((1,), (1,)), ((), ())), preferred_element_type=F32)


def _dot_tn(a, b):
    return lax.dot_general(a, b, (((0,), (0,)), ((), ())), preferred_element_type=F32)


def _split2(a):
    hi = a.astype(BF16)
    lo = (a - hi.astype(F32)).astype(BF16)
    return hi, lo


def _split3(a):
    hi = a.astype(BF16)
    r = a - hi.astype(F32)
    mid = r.astype(BF16)
    lo = (r - mid.astype(F32)).astype(BF16)
    return hi, mid, lo


def _layer_norm(x):
    mu = jnp.mean(x, axis=-1, keepdims=True)
    xc = x - mu
    var = jnp.mean(xc * xc, axis=-1, keepdims=True)
    return xc * lax.rsqrt(var + LN_EPS)


def _sigmoid(x):
    return 0.5 * jnp.tanh(0.5 * x) + 0.5


def _silu(x):
    return x * _sigmoid(x)


def _mod_kernel(c_ref, w_ref, b_ref, o_ref):
    a = _silu(c_ref[...])
    ah, al = _split2(a)
    wh, wl = _split2(w_ref[...])
    o_ref[...] = _dot(ah, wh) + _dot(ah, wl) + _dot(al, wh) + b_ref[...]


def _modulation(c_all, w_mod, b_mod, layer):
    rows, d = c_all.shape
    cols = w_mod.shape[-1]
    tn = 1024
    return pl.pallas_call(
        _mod_kernel,
        grid=(cols // tn,),
        in_specs=[
            pl.BlockSpec((rows, d), lambda j: (0, 0)),
            pl.BlockSpec((None, d, tn), lambda j: (layer, 0, j)),
            pl.BlockSpec((None, 1, tn), lambda j: (layer, 0, j)),
        ],
        out_specs=pl.BlockSpec((rows, tn), lambda j: (0, j)),
        out_shape=jax.ShapeDtypeStruct((rows, cols), F32),
        compiler_params=_cparams("arbitrary"),
        name="modulation",
    )(c_all, w_mod, b_mod)


def _rope(x, cos, sin):
    w = x.shape[1]
    lane = lax.broadcasted_iota(jnp.int32, x.shape, 1)
    partner = jnp.where((lane & 16) == 0, pltpu.roll(x, w - 16, 1), pltpu.roll(x, 16, 1))
    return x * cos + partner * sin


def _lower_bound(lb_ref, layer):
    logits = lb_ref[...]
    m = jnp.max(logits, axis=0, keepdims=True)
    ex = jnp.exp(logits - m)
    p = ex / jnp.sum(ex, axis=0, keepdims=True)
    lb = jnp.zeros_like(p[0:1])
    for j in range(1, layer + 1):
        lb = lb + p[j:j + 1]
    return lb


def _forget_gate(z, lb):
    log_sig = jnp.minimum(z, 0.0) - jnp.log1p(jnp.exp(-jnp.abs(z)))
    a = jnp.log(lb)
    bb = jnp.log1p(-lb) + log_sig
    log_f = jnp.maximum(a, bb) + jnp.log1p(jnp.exp(-jnp.abs(a - bb)))
    return log_f, (1.0 - lb) * _sigmoid(-z)


KV2_DIM = 2 * ATTN_KV_DIM
IN_Q0, IN_K0, IN_V0 = 0, ATTN_Q_DIM, ATTN_Q_DIM + KV2_DIM
IN_HG0 = IN_V0 + KV2_DIM
IN_GAB0 = IN_HG0 + 5 * HG_DIM
HGP_Q, HGP_GF, HGP_GB, HGP_KF, HGP_KB, HGP_V, HGP_OG = range(7)
IN_SLAB = 256


def _inproj_kernel(*refs, rope, layer):
    if rope:
        x_ref, sc_ref, sh_ref, w_ref, lbf_ref, lbb_ref, cos_ref, sin_ref, q_ref, kv_ref, hg_ref, gab_ref = refs
    else:
        x_ref, sc_ref, sh_ref, w_ref, lbf_ref, lbb_ref, q_ref, kv_ref, hg_ref, gab_ref = refs
    u = _layer_norm(x_ref[0]) * (1.0 + sc_ref[0]) + sh_ref[0]
    ub = u.astype(BF16)
    sw = IN_SLAB
    lb_f, lb_b = _lower_bound(lbf_ref, layer), _lower_bound(lbb_ref, layer)

    def slab(w0):
        return _dot(ub, w_ref[:, w0:w0 + sw])

    def rotated(x, c):
        return _rope(x, cos_ref[:, c:c + sw], sin_ref[:, c:c + sw]) if rope else x

    def q_slab(c):
        q_ref[0, :, c:c + sw] = rotated(slab(IN_Q0 + c), c)

    def k_slab(c):
        kv_ref[0, :, c:c + sw] = rotated(slab(IN_K0 + c), c)

    def v_slab(c):
        kv_ref[0, :, KV2_DIM + c:KV2_DIM + c + sw] = slab(IN_V0 + c)

    def hg_slab(src, c, fn, dsts):
        vals = fn(slab(IN_HG0 + src * HG_DIM + c))
        for j, val in zip(dsts, vals if isinstance(vals, tuple) else (vals,)):
            hg_ref[0, :, j * HG_DIM + c:j * HG_DIM + c + sw] = val

    def gate_slab(c):
        gab_ref[0, :, c:c + sw] = _sigmoid(slab(IN_GAB0 + c))

    heavy = []
    for c in range(0, HG_DIM, sw):
        heavy.append(functools.partial(hg_slab, 1, c, lambda z, c=c: _forget_gate(z, lb_f[:, c:c + sw]), (HGP_GF, HGP_KF)))
        heavy.append(functools.partial(hg_slab, 2, c, lambda z, c=c: _forget_gate(z, lb_b[:, c:c + sw]), (HGP_GB, HGP_KB)))
    light = [functools.partial(gate_slab, c) for c in range(0, gab_ref.shape[2], sw)]
    light += [functools.partial(q_slab, c) for c in range(0, ATTN_Q_DIM, sw)]
    for c in range(0, KV2_DIM, sw):
        light += [functools.partial(k_slab, c), functools.partial(v_slab, c)]
    for c in range(0, HG_DIM, sw):
        light.append(functools.partial(hg_slab, 0, c, _silu, (HGP_Q,)))
        light.append(functools.partial(hg_slab, 3, c, lambda v: v, (HGP_V,)))
        light.append(functools.partial(hg_slab, 4, c, _silu, (HGP_OG,)))
    per_heavy = len(light) // len(heavy)
    for i, task in enumerate(heavy):
        task()
        for t in light[i * per_heavy:(i + 1) * per_heavy]:
            t()
    for t in light[len(heavy) * per_heavy:]:
        t()


def _in_projection(x, sc, sh, w_in, lb_fw, lb_bw, layer, rope_tabs):
    b, l, d = x.shape
    tm = TOKEN_TILE
    per_batch = sc.shape[0] > 1
    mod_map = (lambda bi, i: (bi, 0, 0)) if per_batch else (lambda bi, i: (0, 0, 0))
    in_dim = w_in.shape[-1]
    in_specs = [
        pl.BlockSpec((1, tm, d), lambda bi, i: (bi, i, 0)),
        pl.BlockSpec((1, 1, d), mod_map),
        pl.BlockSpec((1, 1, d), mod_map),
        pl.BlockSpec((None, d, in_dim), lambda bi, i: (layer, 0, 0), pipeline_mode=pl.Buffered(1)),
        pl.BlockSpec(lb_fw.shape, lambda bi, i: (0, 0)),
        pl.BlockSpec(lb_bw.shape, lambda bi, i: (0, 0)),
    ]
    args = [x, sc, sh, w_in, lb_fw, lb_bw]
    if rope_tabs is not None:
        in_specs += [pl.BlockSpec((tm, ATTN_Q_DIM), lambda bi, i: (i, 0))] * 2
        args += list(rope_tabs)
    widths = (ATTN_Q_DIM, 2 * KV2_DIM, 7 * HG_DIM, 2 * d)
    return pl.pallas_call(
        functools.partial(_inproj_kernel, rope=rope_tabs is not None, layer=layer),
        grid=(b, l // tm),
        in_specs=in_specs,
        out_specs=[pl.BlockSpec((1, tm, wd), lambda bi, i: (bi, i, 0)) for wd in widths],
        out_shape=[jax.ShapeDtypeStruct((b, l, wd), F32) for wd in widths],
        compiler_params=_cparams("parallel", "arbitrary"),
        name="in_projection",
    )(*args)


def _relayout_w_in(w_in):
    q0, k0, v0, r0 = 0, ATTN_Q_DIM, ATTN_Q_DIM + ATTN_KV_DIM, ATTN_Q_DIM + 2 * ATTN_KV_DIM
    parts = [w_in[..., q0:k0]]
    for c0 in (k0, v0):
        for g in range(ATTN_KV_HEADS):
            head = w_in[..., c0 + g * HEAD_DIM:c0 + (g + 1) * HEAD_DIM]
            parts += [head, head]
    parts.append(w_in[..., r0:])
    return jnp.concatenate(parts, axis=-1).astype(BF16)


def _attend(q, kvs, sinks, ok):
    tq = q.shape[0]
    lc = kvs[0].shape[0]
    rep = ATTN_HEADS // ATTN_KV_HEADS
    tile = 2 * HEAD_DIM
    low_half = lax.broadcasted_iota(jnp.int32, (tq, tile), 1) < HEAD_DIM
    q = q * HEAD_DIM ** -0.5
    if ok is not None:
        mask = jnp.concatenate([jnp.ones((tq, lc), jnp.bool_), ok], axis=1)
        mask = jnp.concatenate([mask] * rep, axis=0)
    tiles = []
    for g in range(ATTN_KV_HEADS):
        kk = jnp.concatenate([t[:, g * tile:(g + 1) * tile] for t in kvs], axis=0).astype(BF16)
        vv = jnp.concatenate([t[:, KV2_DIM + g * tile:KV2_DIM + (g + 1) * tile] for t in kvs], axis=0).astype(BF16)
        heads = range(g * rep, (g + 1) * rep)
        qs = []
        for h in heads:
            q2 = q[:, (h // 2) * tile:(h // 2 + 1) * tile]
            qs.append(jnp.where(low_half == (h % 2 == 0), q2, 0.0))
        s = _dot_nt(jnp.concatenate(qs, axis=0).astype(BF16), kk)
        if ok is not None:
            s = jnp.where(mask, s, -jnp.inf)
        sink = jnp.concatenate([jnp.full((tq, 1), sinks[h], F32) for h in heads], axis=0)
        m = jnp.maximum(jnp.max(s, axis=1, keepdims=True), sink)
        e = jnp.exp(s - m)
        den = jnp.exp(sink - m) + jnp.sum(e, axis=1, keepdims=True)
        o = _dot(e.astype(BF16), vv) / den
        for r in range(0, rep, 2):
            tiles.append(jnp.where(low_half, o[r * tq:(r + 1) * tq], o[(r + 1) * tq:(r + 2) * tq]))
    return jnp.concatenate(tiles, axis=1)


def _attn_kernel(*refs, layer, nsteps, windowed):
    sinks = [refs[0][layer, h] for h in range(ATTN_HEADS)]
    if not windowed:
        _, q_ref, ckv_ref, o_ref = refs
        o_ref[0] = _attend(q_ref[0], [ckv_ref[0]], sinks, None).astype(BF16)
        return
    _, q_ref, kvp_ref, kvc_ref, kvn_ref, ckv_ref, o_ref = refs
    n = pl.program_id(1)
    ti = lax.broadcasted_iota(jnp.int32, (WBLOCK, 3 * WBLOCK), 0)
    sj = lax.broadcasted_iota(jnp.int32, (WBLOCK, 3 * WBLOCK), 1)
    rel = sj - WBLOCK - ti
    band = (rel <= WINDOW) & (rel >= -WINDOW)
    blocks = [kvp_ref[0]] + [kvc_ref[0, i * WBLOCK:(i + 1) * WBLOCK] for i in range(ATTN_STEP_BLOCKS)] + [kvn_ref[0]]
    for i in range(ATTN_STEP_BLOCKS):
        ok = band
        if i == 0:
            ok = ok & ((sj >= WBLOCK) | (n > 0))
        if i == ATTN_STEP_BLOCKS - 1:
            ok = ok & ((sj < 2 * WBLOCK) | (n < nsteps - 1))
        rows = slice(i * WBLOCK, (i + 1) * WBLOCK)
        o_ref[0, rows, :] = _attend(q_ref[0, rows, :], [ckv_ref[0]] + blocks[i:i + 3], sinks, ok).astype(BF16)


def _attention(q, kv, ckv, sink, layer):
    b, l, _ = q.shape
    lc = ckv.shape[1]
    windowed = kv is not None
    smem = pl.BlockSpec(memory_space=pltpu.SMEM)
    cspec = pl.BlockSpec((1, lc, 2 * KV2_DIM), lambda bi, n: (bi, 0, 0))
    if windowed:
        kvw = 2 * KV2_DIM
        sb = ATTN_STEP_BLOCKS
        tq = sb * WBLOCK
        nb = l // WBLOCK
        in_specs = [
            smem, pl.BlockSpec((1, tq, ATTN_Q_DIM), lambda bi, n: (bi, n, 0)),
            pl.BlockSpec((1, WBLOCK, kvw), lambda bi, n: (bi, jnp.maximum(n * sb - 1, 0), 0)),
            pl.BlockSpec((1, tq, kvw), lambda bi, n: (bi, n, 0)),
            pl.BlockSpec((1, WBLOCK, kvw), lambda bi, n: (bi, jnp.minimum((n + 1) * sb, nb - 1), 0)),
            cspec,
        ]
        args = (sink, q, kv, kv, kv, ckv)
    else:
        tq = WBLOCK
        in_specs = [smem, pl.BlockSpec((1, tq, ATTN_Q_DIM), lambda bi, n: (bi, n, 0)), cspec]
        args = (sink, q, ckv)
    return pl.pallas_call(
        functools.partial(_attn_kernel, layer=layer, nsteps=l // tq, windowed=windowed),
        grid=(b, l // tq),
        in_specs=in_specs,
        out_specs=pl.BlockSpec((1, tq, ATTN_Q_DIM), lambda bi, n: (bi, n, 0)),
        out_shape=jax.ShapeDtypeStruct((b, l, ATTN_Q_DIM), BF16),
        compiler_params=_cparams("parallel", "arbitrary"),
        name="window_attention" if windowed else "context_attention",
    )(*args)


def _block_ref_rows(cum, blk, r):
    c, w = cum.shape
    if blk % 8 == 0:
        x = cum.reshape(c // blk, blk, w)
        return jnp.broadcast_to(x[:, r:r + 1, :], x.shape).reshape(c, w)
    pos = lax.broadcasted_iota(jnp.int32, cum.shape, 0) & (blk - 1)
    out = cum
    for p in range(blk):
        delta = r - p
        if delta == 0:
            continue
        out = jnp.where(pos == p, pltpu.roll(cum, (-delta) % c, 0), out)
    return out


def _hgrn_kernel(*refs, has_init):
    if has_init:
        (tri_ref, lm_ref, dm_ref, qf_ref, gf_ref, kf_ref, vf_ref, qb_ref, gb_ref, kb_ref, vb_ref, s0_ref,
         of_ref, ob_ref, s_ref, cum_ref) = refs
    else:
        (tri_ref, lm_ref, dm_ref, qf_ref, gf_ref, kf_ref, vf_ref, qb_ref, gb_ref, kb_ref, vb_ref,
         of_ref, ob_ref, s_ref, cum_ref) = refs

    @pl.when(pl.program_id(1) == 0)
    def _():
        if has_init:
            s_ref[...] = s0_ref[...]
        else:
            s_ref[...] = jnp.zeros(s_ref.shape, F32)

    tt = qf_ref.shape[1]
    c = tri_ref.shape[1]
    nch = tt // c
    row = lax.broadcasted_iota(jnp.int32, (c, HG_DK), 0)
    levels = []
    hs = c // 2
    while hs >= 1:
        levels.append(hs)
        hs //= 2
    dirs = ((qf_ref, gf_ref, kf_ref, vf_ref, of_ref), (qb_ref, gb_ref, kb_ref, vb_ref, ob_ref))

    def direct_dd(cum, backward):
        return cum - _block_ref_rows(cum, HG_DIRECT, HG_DIRECT - 1 if backward else 0)

    worst = jnp.zeros((c, HG_DK), F32)
    for d, (_, g_ref, _, _, _) in enumerate(dirs):
        for h in range(HG_HEADS):
            for ci in range(nch):
                g3 = _split3(g_ref[0, ci * c:(ci + 1) * c, h * HG_DK:(h + 1) * HG_DK])
                cum = _dot(tri_ref[d], g3[0]) + _dot(tri_ref[d], g3[1]) + _dot(tri_ref[d], g3[2])
                cum_ref[(d * HG_HEADS + h) * nch + ci] = cum
                worst = jnp.maximum(worst, -direct_dd(cum, d == 1))
    in_range = jnp.max(worst) <= HG_EXP_LIMIT

    def scan(direct):
        tree = [(li, hs) for li, hs in enumerate(levels) if not direct or hs >= HG_DIRECT]
        for d, (q_ref, g_ref, k_ref, v_ref, o_ref) in enumerate(dirs):
            backward = d == 1
            q_halves = {li: ((row & (2 * hs - 1)) >= hs) != backward for li, hs in tree}
            for h in range(HG_HEADS):
                sl = slice(h * HG_DK, (h + 1) * HG_DK)
                q_all, k_all, v_all = q_ref[0, :, sl], k_ref[0, :, sl], v_ref[0, :, sl]
                st = s_ref[0, d, h]
                chunks = range(nch)
                for ci in (reversed(chunks) if backward else chunks):
                    rows = slice(ci * c, (ci + 1) * c)
                    q, kk, v = q_all[rows], k_all[rows], v_all[rows]
                    qb, kb, vb = q.astype(BF16), kk.astype(BF16), v.astype(BF16)
                    cum = cum_ref[(d * HG_HEADS + h) * nch + ci]
                    o = _dot_nt((q * jnp.exp(cum)).astype(BF16), st.astype(BF16))
                    if not direct:
                        o = o + jnp.sum(q * kk, axis=1, keepdims=True) * v
                    scores = jnp.zeros((c, c), F32)
                    for li, hs in tree:
                        blk = 2 * hs
                        dd = cum - _block_ref_rows(cum, blk, hs if backward else hs - 1)
                        e = jnp.exp(-jnp.abs(dd)).astype(BF16)
                        x = jnp.where(q_halves[li], qb, kb) * e
                        scores = scores + _dot_nt(x, x) * lm_ref[d, li]
                    if direct:
                        dd = direct_dd(cum, backward)
                        xq = qb * jnp.exp(dd).astype(BF16)
                        xk = kb * jnp.exp(-dd).astype(BF16)
                        scores = scores + jnp.where(dm_ref[d] > 0.5, _dot_nt(xq, xk), 0.0)
                    o = o + _dot(scores.astype(BF16), vb)
                    o_ref[0, rows, sl] = o
                    tot = cum[0:1] if backward else cum[c - 1:c]
                    kdec = kk * jnp.exp(tot - cum)
                    st = st * jnp.exp(tot) + _dot_tn(vb, kdec.astype(BF16))
                s_ref[0, d, h] = st

    pl.when(in_range)(lambda: scan(True))
    pl.when(jnp.logical_not(in_range))(lambda: scan(False))


def _hgrn_scan(hgp, consts, s0):
    b, l, _ = hgp.shape
    tt = TOKEN_TILE
    ns = l // tt
    has_init = s0 is not None
    tri, lmask, dmask = consts
    fwd = lambda j: pl.BlockSpec((1, tt, HG_DIM), lambda bi, i: (bi, i, j))
    bwd = lambda j: pl.BlockSpec((1, tt, HG_DIM), lambda bi, i: (bi, ns - 1 - i, j))
    sspec = pl.BlockSpec((1, 2, HG_HEADS, HG_DV, HG_DK), lambda bi, i: (bi, 0, 0, 0, 0))
    in_specs = [
        pl.BlockSpec(tri.shape, lambda bi, i: (0, 0, 0)),
        pl.BlockSpec(lmask.shape, lambda bi, i: (0, 0, 0, 0)),
        pl.BlockSpec(dmask.shape, lambda bi, i: (0, 0, 0)),
        fwd(HGP_Q), fwd(HGP_GF), fwd(HGP_KF), fwd(HGP_V), bwd(HGP_Q), bwd(HGP_GB), bwd(HGP_KB), bwd(HGP_V),
    ]
    args = [tri, lmask, dmask] + [hgp] * 8
    if has_init:
        in_specs.append(sspec)
        args.append(s0)
    c = tri.shape[1]
    return pl.pallas_call(
        functools.partial(_hgrn_kernel, has_init=has_init),
        grid=(b, ns),
        in_specs=in_specs,
        out_specs=[
            pl.BlockSpec((1, tt, HG_DIM), lambda bi, i: (bi, i, 0)),
            pl.BlockSpec((1, tt, HG_DIM), lambda bi, i: (bi, ns - 1 - i, 0)),
            sspec,
        ],
        out_shape=[
            jax.ShapeDtypeStruct((b, l, HG_DIM), F32),
            jax.ShapeDtypeStruct((b, l, HG_DIM), F32),
            jax.ShapeDtypeStruct((b, 2, HG_HEADS, HG_DV, HG_DK), F32),
        ],
        scratch_shapes=[pltpu.VMEM((2 * HG_HEADS * (tt // c), c, HG_DK), F32)],
        compiler_params=_cparams("parallel", "arbitrary"),
        name="hgrn_scan",
    )(*args)


def _hgrn_constants(c):
    t = lax.broadcasted_iota(jnp.int32, (c, c), 0)
    s = lax.broadcasted_iota(jnp.int32, (c, c), 1)
    tri = jnp.stack([s <= t, s >= t]).astype(BF16)
    masks, direct = [], []
    for backward in (False, True):
        per_level = []
        hs = c // 2
        while hs >= 1:
            blk = 2 * hs
            same = (t // blk) == (s // blk)
            t_late, s_late = (t % blk) >= hs, (s % blk) >= hs
            pair = (jnp.logical_not(t_late) & s_late) if backward else (t_late & jnp.logical_not(s_late))
            per_level.append(same & pair)
            hs //= 2
        masks.append(jnp.stack(per_level))
        direct.append(((t // HG_DIRECT) == (s // HG_DIRECT)) & ((s >= t) if backward else (s <= t)))
    return tri, jnp.stack(masks).astype(F32), jnp.stack(direct).astype(F32)


def _mixout_kernel(attn_ref, of_ref, ob_ref, gh_ref, ga_ref, gb_ref, x_ref, g1_ref, sc2_ref, sh2_ref,
                   ng_ref, wa_ref, wh_ref, wo_ref, lng_ref, lnb_ref, wrh_ref, wrl_ref,
                   xo_ref, u2_ref, aff_ref, *, alpha):
    o = of_ref[0] + ob_ref[0]
    parts = []
    for h in range(HG_HEADS):
        oh = o[:, h * HG_DV:(h + 1) * HG_DV]
        parts.append(oh * lax.rsqrt(jnp.mean(oh * oh, axis=-1, keepdims=True) + LN_EPS))
    hg = jnp.concatenate(parts, axis=1) * ng_ref[...] * gh_ref[0]
    ya = _dot(attn_ref[0], wa_ref[...])
    yh = _dot(hg.astype(BF16), wh_ref[...])
    mix = ga_ref[0] * ya + gb_ref[0] * yh
    y = _dot(mix.astype(BF16), wo_ref[...])
    xn = _layer_norm(alpha * x_ref[0] + g1_ref[0] * y) * lng_ref[...] + lnb_ref[...]
    xo_ref[0] = xn
    u2 = _layer_norm(xn) * (1.0 + sc2_ref[0]) + sh2_ref[0]
    u2_ref[0] = u2.astype(BF16)
    uh, ul = _split2(u2)
    lg = _dot(uh, wrh_ref[...]) + _dot(ul, wrh_ref[...]) + _dot(uh, wrl_ref[...])
    lg = jnp.where(lax.broadcasted_iota(jnp.int32, lg.shape, 1) < N_EXPERTS, lg, -jnp.inf)
    ex = jnp.exp(lg - jnp.max(lg, axis=1, keepdims=True))
    aff = ex / jnp.sum(ex, axis=1, keepdims=True)
    aff_ref[0] = aff.T[0:N_EXPERTS, :]


def _mixer_output(attn, o_f, o_b, hgp, gab, x, g1, sc2, sh2, norm_g, w_a, w_h, w_o, ln_g, ln_b,
                  wr_hi, wr_lo, layer, alpha):
    b, l, d = x.shape
    tm = TOKEN_TILE
    per_batch = g1.shape[0] > 1
    mod_map = (lambda bi, i: (bi, 0, 0)) if per_batch else (lambda bi, i: (0, 0, 0))
    tok = lambda w, j=0: pl.BlockSpec((1, tm, w), lambda bi, i: (bi, i, j))
    lay2 = lambda r, c: pl.BlockSpec((None, r, c), lambda bi, i: (layer, 0, 0))
    in_specs = [
        tok(ATTN_Q_DIM), tok(HG_DIM), tok(HG_DIM), tok(HG_DIM, HGP_OG), tok(d, 0), tok(d, 1), tok(d),
        pl.BlockSpec((1, 1, d), mod_map), pl.BlockSpec((1, 1, d), mod_map), pl.BlockSpec((1, 1, d), mod_map),
        lay2(1, HG_DIM), lay2(ATTN_Q_DIM, d), lay2(HG_DIM, d), lay2(d, d),
        pl.BlockSpec((None, None, 1, d), lambda bi, i: (layer, 0, 0, 0)),
        pl.BlockSpec((None, None, 1, d), lambda bi, i: (layer, 0, 0, 0)),
        lay2(d, ROUTER_LANES), lay2(d, ROUTER_LANES),
    ]
    return pl.pallas_call(
        functools.partial(_mixout_kernel, alpha=alpha),
        grid=(b, l // tm),
        in_specs=in_specs,
        out_specs=[tok(d), tok(d), pl.BlockSpec((1, N_EXPERTS, tm), lambda bi, i: (bi, 0, i))],
        out_shape=[
            jax.ShapeDtypeStruct((b, l, d), F32),
            jax.ShapeDtypeStruct((b, l, d), BF16),
            jax.ShapeDtypeStruct((b, N_EXPERTS, l), F32),
        ],
        compiler_params=_cparams("parallel", "arbitrary"),
        name="mixer_output",
    )(attn, o_f, o_b, hgp, gab, gab, x, g1, sc2, sh2, norm_g, w_a, w_h, w_o, ln_g, ln_b, wr_hi, wr_lo)


def _select_kernel(aff_ref, triu_ref, rk_ref, rkt_ref, st_ref, *, cap):
    a = aff_ref[0]
    ne, n = a.shape
    ts = triu_ref.shape[0]
    bits = lax.bitcast_convert_type(a, jnp.int32)
    capf = jnp.float32(cap)

    def count_ge(v):
        return jnp.sum(jnp.where(bits >= v, 1.0, 0.0), axis=1, keepdims=True)

    def body(_, carry):
        lo, hi = carry
        mid = lo + ((hi - lo) >> 1)
        ok = count_ge(mid) >= capf
        return jnp.where(ok, mid, lo), jnp.where(ok, hi, mid)

    lo0 = jnp.zeros((ne, 1), jnp.int32)
    hi0 = jnp.full((ne, 1), 0x7F800000, jnp.int32)
    thr, _ = lax.fori_loop(0, 31, body, (lo0, hi0))
    gt = bits > thr
    eq = bits == thr
    need = capf - jnp.sum(jnp.where(gt, 1.0, 0.0), axis=1, keepdims=True)
    triu = triu_ref[...]
    lane = lax.broadcasted_iota(jnp.int32, st_ref.shape[1:], 1)
    eq_carry = jnp.zeros((ne, 1), F32)
    sel_carry = jnp.zeros((ne, 1), F32)
    starts = jnp.zeros(st_ref.shape[1:], F32)
    for t in range(n // ts):
        sl = slice(t * ts, (t + 1) * ts)
        eq_t = eq[:, sl]
        eq_pref = _dot(jnp.where(eq_t, 1.0, 0.0).astype(BF16), triu) + eq_carry
        sel_t = gt[:, sl] | (eq_t & (eq_pref <= need))
        sel_f = jnp.where(sel_t, 1.0, 0.0)
        sel_pref = _dot(sel_f.astype(BF16), triu) + sel_carry
        rk_t = jnp.where(sel_t, sel_pref - 1.0, -1.0)
        rk_ref[0, :, sl] = rk_t
        rkt_ref[0, sl, :] = rk_t.T
        starts = jnp.where(lane == t, sel_carry, starts)
        eq_carry = eq_pref[:, ts - 1:ts]
        sel_carry = sel_pref[:, ts - 1:ts]
    st_ref[0] = starts


def _select(aff, triu, cap):
    b, ne, n = aff.shape
    return pl.pallas_call(
        functools.partial(_select_kernel, cap=cap),
        grid=(b,),
        in_specs=[
            pl.BlockSpec((1, ne, n), lambda bi: (bi, 0, 0)),
            pl.BlockSpec(triu.shape, lambda bi: (0, 0)),
        ],
        out_specs=[
            pl.BlockSpec((1, ne, n), lambda bi: (bi, 0, 0)),
            pl.BlockSpec((1, n, ne), lambda bi: (bi, 0, 0)),
            pl.BlockSpec((1, ne, 128), lambda bi: (bi, 0, 0)),
        ],
        out_shape=[
            jax.ShapeDtypeStruct((b, ne, n), F32),
            jax.ShapeDtypeStruct((b, n, ne), F32),
            jax.ShapeDtypeStruct((b, ne, 128), F32),
        ],
        compiler_params=_cparams("parallel"),
        name="expert_choice_select",
    )(aff, triu)


def _slot_window(cap, tile):
    return min(cap, tile + SLOT_ALIGN)


def _aligned_start(s0, cap, w):
    return jnp.minimum((s0 // SLOT_ALIGN) * SLOT_ALIGN, cap - w)


def _window_start(s0, cap, tile):
    return _aligned_start(s0, cap, _slot_window(cap, tile))


def _ffn_kernel(st_ref, u_ref, rk_ref, aff_ref, wg_ref, wu_ref, wd_ref, y_ref, xs_ref,
                *, cap, ns, unroll, expert_major):
    b, e = pl.program_id(0), pl.program_id(1)
    if expert_major:
        b, e = e, b
    d = u_ref.shape[2]
    nt = rk_ref.shape[2]
    w_full = _slot_window(cap, TOKEN_TILE)
    widths = [w for w in DISPATCH_WINDOWS if w < w_full] + [w_full]
    sub = TOKEN_TILE // ROUTE_TILE
    base = (b * N_EXPERTS + e) * (ns + 1)

    def bounds(t):
        return st_ref[base + t * sub], st_ref[base + (t + 1) * sub]

    def dispatch(t, w):
        s0, s1 = bounds(t)
        a = pl.multiple_of(_aligned_start(s0, cap, w), SLOT_ALIGN)
        slot = a + lax.broadcasted_iota(jnp.int32, (w, 1), 0)
        hit = slot.astype(F32) == rk_ref[0, 0, pl.ds(t, 1), :]
        mine = (slot >= s0) & (slot < s1)
        tok0 = pl.multiple_of(t * TOKEN_TILE, TOKEN_TILE)
        rows = _dot(jnp.where(hit, 1.0, 0.0).astype(BF16), u_ref[0, pl.ds(tok0, TOKEN_TILE), :])
        wts = jnp.sum(jnp.where(hit, aff_ref[0, 0, pl.ds(t, 1), :], 0.0), axis=1, keepdims=True)
        vals = jnp.concatenate([rows, jnp.broadcast_to(wts, (w, 128))], axis=1)
        pltpu.store(xs_ref.at[pl.ds(a, w), :], vals, mask=jnp.broadcast_to(mine, vals.shape))

    def run(w):
        def group(i, carry):
            for j in range(unroll):
                dispatch(i * unroll + j, w)
            return carry

        lax.fori_loop(0, nt // unroll, group, 0)

    def fits(t, ok):
        s0, s1 = bounds(t)
        return tuple(o & (s1 - _aligned_start(s0, cap, w) <= w) for o, w in zip(ok, widths[:-1]))

    ok = lax.fori_loop(0, nt, fits, (True,) * (len(widths) - 1))
    wider_needed = True
    for i, w in enumerate(widths):
        fit = ok[i] if i < len(ok) else True
        pl.when(jnp.logical_and(wider_needed, fit))(functools.partial(run, w))
        wider_needed = jnp.logical_and(wider_needed, jnp.logical_not(fit))

    rc = min(cap, 256)
    for c in range(cap // rc):
        rows = slice(c * rc, (c + 1) * rc)
        xb = xs_ref[rows, 0:d].astype(BF16)
        h = _silu(_dot(xb, wg_ref[0])) * _dot(xb, wu_ref[0])
        y_ref[0, 0, rows, :] = (_dot(h.astype(BF16), wd_ref[0]) * xs_ref[rows, d:d + 1]).astype(BF16)


def _expert_ffn(starts, u2, rk, aff, w_gate, w_up, w_down, layer, cap):
    b, n, d = u2.shape
    ff = w_gate.shape[-1]
    ns = n // ROUTE_TILE
    nt = n // TOKEN_TILE
    rk4 = rk.reshape(b, N_EXPERTS, nt, TOKEN_TILE)
    aff4 = aff.reshape(b, N_EXPERTS, nt, TOKEN_TILE)
    expert_major = n * d < 3 * d * ff
    grid = (N_EXPERTS, b) if expert_major else (b, N_EXPERTS)
    be = (lambda i, j: (j, i)) if expert_major else (lambda i, j: (i, j))
    wspec = lambda r, c: pl.BlockSpec((None, 1, r, c), lambda i, j, st: (layer, be(i, j)[1], 0, 0))
    rspec = pl.BlockSpec((1, 1, nt, TOKEN_TILE), lambda i, j, st: (*be(i, j), 0, 0))
    grid_spec = pltpu.PrefetchScalarGridSpec(
        num_scalar_prefetch=1,
        grid=grid,
        in_specs=[
            pl.BlockSpec((1, n, d), lambda i, j, st: (be(i, j)[0], 0, 0),
                         pipeline_mode=None if expert_major else pl.Buffered(1)),
            rspec, rspec,
            wspec(d, ff), wspec(d, ff), wspec(ff, d),
        ],
        out_specs=pl.BlockSpec((1, 1, cap, d), lambda i, j, st: (*be(i, j), 0, 0)),
        scratch_shapes=[pltpu.VMEM((cap, d + 128), F32)],
    )
    return pl.pallas_call(
        functools.partial(_ffn_kernel, cap=cap, ns=ns, unroll=min(nt, 4), expert_major=expert_major),
        grid_spec=grid_spec,
        out_shape=jax.ShapeDtypeStruct((b, N_EXPERTS, cap, d), BF16),
        compiler_params=_cparams("parallel", "arbitrary"),
        name="expert_ffn",
    )(starts, u2, rk4, aff4, w_gate, w_up, w_down)


def _combine_kernel(*refs, cap, ns, alpha):
    st_ref, rkt_ref, x_ref, g2_ref, lng_ref, lnb_ref = refs[:6]
    y_refs = refs[6:6 + N_EXPERTS]
    o_ref = refs[6 + N_EXPERTS]
    b, t = pl.program_id(0), pl.program_id(1)
    sub = TOKEN_TILE // ROUTE_TILE
    w = _slot_window(cap, ROUTE_TILE)
    for half in range(sub):
        rows = slice(half * ROUTE_TILE, (half + 1) * ROUTE_TILE)
        acc = jnp.zeros((ROUTE_TILE, x_ref.shape[2]), F32)
        for e in range(N_EXPERTS):
            base = (b * N_EXPERTS + e) * (ns + 1) + t * sub
            a_tile = _window_start(st_ref[base], cap, TOKEN_TILE)
            a = _window_start(st_ref[base + half], cap, ROUTE_TILE)
            off = pl.multiple_of(a - a_tile, SLOT_ALIGN)
            slot = (a + lax.broadcasted_iota(jnp.int32, (1, w), 1)).astype(F32)
            hit = rkt_ref[0, rows, e:e + 1] == slot
            acc = acc + _dot(jnp.where(hit, 1.0, 0.0).astype(BF16), y_refs[e][pl.ds(off, w), :])
        xr = alpha * x_ref[0, rows, :] + g2_ref[0] * acc
        o_ref[0, rows, :] = _layer_norm(xr) * lng_ref[...] + lnb_ref[...]


def _combine(starts, rkt, x, g2, ln_g, ln_b, y, layer, cap, alpha):
    b, n, d = x.shape
    tt = TOKEN_TILE
    ns = n // ROUTE_TILE
    sub = TOKEN_TILE // ROUTE_TILE
    w = _slot_window(cap, tt)
    per_batch = g2.shape[0] > 1
    mod_map = (lambda bi, t, st: (bi, 0, 0)) if per_batch else (lambda bi, t, st: (0, 0, 0))

    def y_spec(e):
        def index_map(bi, t, st):
            be = bi * N_EXPERTS + e
            a = _window_start(st[be * (ns + 1) + t * sub], cap, tt)
            return pl.multiple_of(be * cap + a, SLOT_ALIGN), 0
        return pl.BlockSpec((pl.Element(w), pl.Element(d)), index_map)

    y = y.reshape(b * N_EXPERTS * cap, d)
    grid_spec = pltpu.PrefetchScalarGridSpec(
        num_scalar_prefetch=1,
        grid=(b, n // tt),
        in_specs=[
            pl.BlockSpec((1, tt, N_EXPERTS), lambda bi, t, st: (bi, t, 0)),
            pl.BlockSpec((1, tt, d), lambda bi, t, st: (bi, t, 0)),
            pl.BlockSpec((1, 1, d), mod_map),
            pl.BlockSpec((None, None, 1, d), lambda bi, t, st: (layer, 1, 0, 0)),
            pl.BlockSpec((None, None, 1, d), lambda bi, t, st: (layer, 1, 0, 0)),
        ] + [y_spec(e) for e in range(N_EXPERTS)],
        out_specs=pl.BlockSpec((1, tt, d), lambda bi, t, st: (bi, t, 0)),
    )
    return pl.pallas_call(
        functools.partial(_combine_kernel, cap=cap, ns=ns, alpha=alpha),
        grid_spec=grid_spec,
        out_shape=jax.ShapeDtypeStruct((b, n, d), F32),
        compiler_params=_cparams("parallel", "arbitrary"),
        name="moe_combine",
    )(starts, rkt, x, g2, ln_g, ln_b, *([y] * N_EXPERTS))


def _combine_packed_kernel(st_ref, rkt_ref, x_ref, g2_ref, lng_ref, lnb_ref, y_hbm, o_ref, ybuf, sem,
                           *, cap, ns, alpha):
    b, t = pl.program_id(0), pl.program_id(1)
    nt = pl.num_programs(1)
    step = b * nt + t
    sub = TOKEN_TILE // ROUTE_TILE
    w = COMBINE_WINDOW

    def window_start(bb, tt, half, e):
        return _aligned_start(st_ref[(bb * N_EXPERTS + e) * (ns + 1) + tt * sub + half], cap, w)

    def copies(bb, tt, slot):
        out = []
        for half in range(sub):
            for e in range(N_EXPERTS):
                row0 = pl.multiple_of((bb * N_EXPERTS + e) * cap + window_start(bb, tt, half, e), SLOT_ALIGN)
                out.append(pltpu.make_async_copy(y_hbm.at[pl.ds(row0, w), :],
                                                 ybuf.at[slot, half, pl.ds(e * w, w), :], sem.at[slot]))
        return out

    slot = step % 2

    @pl.when(step == 0)
    def _():
        for cp in copies(b, t, 0):
            cp.start()

    @pl.when(step + 1 < pl.num_programs(0) * nt)
    def _():
        wrap = t + 1 == nt
        for cp in copies(jnp.where(wrap, b + 1, b), jnp.where(wrap, 0, t + 1), 1 - slot):
            cp.start()

    for cp in copies(b, t, slot):
        cp.wait()

    lanes = 128
    per = lanes // w
    lane = lax.broadcasted_iota(jnp.int32, (1, lanes), 1)
    for half in range(sub):
        rows = slice(half * ROUTE_TILE, (half + 1) * ROUTE_TILE)
        hits = []
        for e0 in range(0, N_EXPERTS, per):
            slot_ids = window_start(b, t, half, e0) + lane
            rank = rkt_ref[0, rows, e0:e0 + 1]
            for j in range(1, per):
                own = lane >= j * w
                slot_ids = jnp.where(own, window_start(b, t, half, e0 + j) + lane - j * w, slot_ids)
                rank = jnp.where(own, rkt_ref[0, rows, e0 + j:e0 + j + 1], rank)
            hits.append(jnp.where(rank == slot_ids.astype(F32), 1.0, 0.0).astype(BF16))
        acc = _dot(jnp.concatenate(hits, axis=1), ybuf[slot, half])
        xr = alpha * x_ref[0, rows, :] + g2_ref[0] * acc
        o_ref[0, rows, :] = _layer_norm(xr) * lng_ref[...] + lnb_ref[...]


def _combine_packed(starts, rkt, x, g2, ln_g, ln_b, y, layer, cap, alpha):
    b, n, d = x.shape
    tt = TOKEN_TILE
    ns = n // ROUTE_TILE
    sub = TOKEN_TILE // ROUTE_TILE
    per_batch = g2.shape[0] > 1
    mod_map = (lambda bi, t, st: (bi, 0, 0)) if per_batch else (lambda bi, t, st: (0, 0, 0))
    grid_spec = pltpu.PrefetchScalarGridSpec(
        num_scalar_prefetch=1,
        grid=(b, n // tt),
        in_specs=[
            pl.BlockSpec((1, tt, N_EXPERTS), lambda bi, t, st: (bi, t, 0)),
            pl.BlockSpec((1, tt, d), lambda bi, t, st: (bi, t, 0)),
            pl.BlockSpec((1, 1, d), mod_map),
            pl.BlockSpec((None, None, 1, d), lambda bi, t, st: (layer, 1, 0, 0)),
            pl.BlockSpec((None, None, 1, d), lambda bi, t, st: (layer, 1, 0, 0)),
            pl.BlockSpec(memory_space=pl.ANY),
        ],
        out_specs=pl.BlockSpec((1, tt, d), lambda bi, t, st: (bi, t, 0)),
        scratch_shapes=[
            pltpu.VMEM((2, sub, N_EXPERTS * COMBINE_WINDOW, d), BF16),
            pltpu.SemaphoreType.DMA((2,)),
        ],
    )
    return pl.pallas_call(
        functools.partial(_combine_packed_kernel, cap=cap, ns=ns, alpha=alpha),
        grid_spec=grid_spec,
        out_shape=jax.ShapeDtypeStruct((b, n, d), F32),
        compiler_params=_cparams("arbitrary", "arbitrary"),
        name="moe_combine_packed",
    )(starts, rkt, x, g2, ln_g, ln_b, y.reshape(b * N_EXPERTS * cap, d))


def _moe(x, u2, aff, g2, ln_g, ln_b, triu, w_gate, w_up, w_down, layer, alpha):
    b, n, _ = x.shape
    cap = CAPACITY_FACTOR * n // N_EXPERTS
    ns = n // ROUTE_TILE
    rk, rkt, st = _select(aff, triu, cap)
    starts = jnp.concatenate([st[:, :, :ns], jnp.full((b, N_EXPERTS, 1), cap, F32)], axis=2)
    starts = starts.astype(jnp.int32).reshape(-1)
    y = _expert_ffn(starts, u2, rk, aff, w_gate, w_up, w_down, layer, cap)
    args = (starts, rkt, x, g2, ln_g, ln_b, y)
    general = lambda *a: _combine(*a, layer, cap, alpha)
    if cap < COMBINE_WINDOW:
        return general(*args)
    s0 = starts.reshape(b, N_EXPERTS, ns + 1)
    fits = jnp.all(s0[:, :, 1:] - _aligned_start(s0[:, :, :-1], cap, COMBINE_WINDOW) <= COMBINE_WINDOW)
    return lax.cond(fits, lambda *a: _combine_packed(*a, layer, cap, alpha), general, *args)


def _rope_tables(l):
    pos = jnp.arange(l)
    nf = HEAD_DIM // 4
    inv = ROPE_THETA ** (-jnp.arange(nf, dtype=F32) / nf)
    ang_r = (pos // GRID_W).astype(F32)[:, None] * inv[None, :]
    ang_c = (pos % GRID_W).astype(F32)[:, None] * inv[None, :]
    cos = jnp.concatenate([jnp.cos(ang_r)] * 2 + [jnp.cos(ang_c)] * 2, axis=1)
    sin = jnp.concatenate([-jnp.sin(ang_r), jnp.sin(ang_r), -jnp.sin(ang_c), jnp.sin(ang_c)], axis=1)
    return jnp.tile(cos, (1, ATTN_HEADS)), jnp.tile(sin, (1, ATTN_HEADS))


def kernel(x, c, ctx, c_ctx, w_mod, b_mod, w_in, attn_sink, hgrn_lb_fw, hgrn_lb_bw, hgrn_norm_g,
           w_branch_attn, w_branch_hgrn, w_out, w_router, w_gate, w_up, w_down, ln_g, ln_b):
    depth, d = w_mod.shape[0], x.shape[-1]
    b = x.shape[0]
    assert x.shape[1] % (4 * TOKEN_TILE) == 0 and ctx.shape[1] == TOKEN_TILE
    alpha = (2 * depth) ** 0.25
    mod_rows = -(-(b + 1) // 8) * 8
    c_all = jnp.concatenate([c, c_ctx[None, :], jnp.zeros((mod_rows - b - 1, d), F32)], axis=0)
    b_mod3 = b_mod[:, None, :]
    norm_g3 = hgrn_norm_g[:, None, :]
    ln_g4, ln_b4 = ln_g[:, :, None, :], ln_b[:, :, None, :]
    w_in_b = _relayout_w_in(w_in)
    w_a_b, w_h_b, w_o_b = w_branch_attn.astype(BF16), w_branch_hgrn.astype(BF16), w_out.astype(BF16)
    w_g_b, w_u_b, w_d_b = w_gate.astype(BF16), w_up.astype(BF16), w_down.astype(BF16)
    wr_pad = jnp.pad(w_router, ((0, 0), (0, 0), (0, ROUTER_LANES - N_EXPERTS)))
    wr_hi = wr_pad.astype(BF16)
    wr_lo = (wr_pad - wr_hi.astype(F32)).astype(BF16)
    rope_tabs = _rope_tables(x.shape[1])
    hg_consts = _hgrn_constants(HG_CHUNK)
    r_i = lax.broadcasted_iota(jnp.int32, (ROUTE_TILE, ROUTE_TILE), 0)
    c_i = lax.broadcasted_iota(jnp.int32, (ROUTE_TILE, ROUTE_TILE), 1)
    triu = (r_i <= c_i).astype(BF16)

    xc = ctx
    for l in range(depth):
        need_ctx = l < depth - 1
        mod = _modulation(c_all, w_mod, b_mod3, l)
        sh1, sc1, g1, sh2, sc2, g2 = [mod[:b, i * d:(i + 1) * d][:, None, :] for i in range(N_MOD)]
        sh1c, sc1c, g1c, sh2c, sc2c, g2c = [mod[b:b + 1, i * d:(i + 1) * d][:, None, :] for i in range(N_MOD)]
        cq, ckv, chg, cgab = _in_projection(xc, sc1c, sh1c, w_in_b, hgrn_lb_fw, hgrn_lb_bw, l, None)
        oc_f, oc_b, s_ctx = _hgrn_scan(chg, hg_consts, None)
        q, kv, hgp, gab = _in_projection(x, sc1, sh1, w_in_b, hgrn_lb_fw, hgrn_lb_bw, l, rope_tabs)
        o_f, o_b, _ = _hgrn_scan(hgp, hg_consts, s_ctx)
        attn = _attention(q, kv, ckv, attn_sink, l)
        x1, u2, aff = _mixer_output(attn, o_f, o_b, hgp, gab, x, g1, sc2, sh2, norm_g3, w_a_b, w_h_b, w_o_b,
                                    ln_g4, ln_b4, wr_hi, wr_lo, l, alpha)
        x = _moe(x1, u2, aff, g2, ln_g4, ln_b4, triu, w_g_b, w_u_b, w_d_b, l, alpha)
        if need_ctx:
            attn_c = _attention(cq, None, ckv, attn_sink, l)
            xc1, uc2, affc = _mixer_output(attn_c, oc_f, oc_b, chg, cgab, xc, g1c, sc2c, sh2c, norm_g3,
                                           w_a_b, w_h_b, w_o_b, ln_g4, ln_b4, wr_hi, wr_lo, l, alpha)
            xc = _moe(xc1, uc2, affc, g2c, ln_g4, ln_b4, triu, w_g_b, w_u_b, w_d_b, l, alpha)
    return x
```

```python
import functools

import jax
import jax.numpy as jnp
from jax import lax
from jax.experimental import pallas as pl
from jax.experimental.pallas import tpu as pltpu

F32 = jnp.float32
BF16 = jnp.bfloat16

GRID_W = 64
ATTN_HEADS = 8
ATTN_KV_HEADS = 2
HEAD_DIM = 64
WINDOW = 128
WBLOCK = 128
ROPE_THETA = 10000.0
HG_HEADS = 4
HG_DK = 128
HG_DV = 128
N_EXPERTS = 16
CAPACITY_FACTOR = 2
N_MOD = 6
LN_EPS = 1e-6
ATTN_Q_DIM = ATTN_HEADS * HEAD_DIM
ATTN_KV_DIM = ATTN_KV_HEADS * HEAD_DIM
HG_DIM = HG_HEADS * HG_DK

TOKEN_TILE = 256
HG_CHUNK = 128
ATTN_STEP_BLOCKS = 4
MIXER_STEP_TILES = 2
HG_DIRECT = 16
HG_EXP_LIMIT = 80.0
ROUTE_TILE = 128
SLOT_ALIGN = 16
DISPATCH_WINDOWS = (128,)
COMBINE_WINDOW = 64
ROUTER_LANES = 128
VMEM_LIMIT_BYTES = 56 << 20


def _cparams(*sem):
    return pltpu.CompilerParams(dimension_semantics=sem, vmem_limit_bytes=VMEM_LIMIT_BYTES)


def _dot(a, b):
    return jnp.dot(a, b, preferred_element_type=F32)


def _dot_nt(a, b):
    return lax.dot_general(a, b, (((1,), (1,)), ((), ())), preferred_element_type=F32)


def _dot_tn(a, b):
    return lax.dot_general(a, b, (((0,), (0,)), ((), ())), preferred_element_type=F32)


def _split2(a):
    hi = a.astype(BF16)
    lo = (a - hi.astype(F32)).astype(BF16)
    return hi, lo


def _split3(a):
    hi = a.astype(BF16)
    r = a - hi.astype(F32)
    mid = r.astype(BF16)
    lo = (r - mid.astype(F32)).astype(BF16)
    return hi, mid, lo


def _layer_norm(x):
    mu = jnp.mean(x, axis=-1, keepdims=True)
    xc = x - mu
    var = jnp.mean(xc * xc, axis=-1, keepdims=True)
    return xc * lax.rsqrt(var + LN_EPS)


def _sigmoid(x):
    return 0.5 * jnp.tanh(0.5 * x) + 0.5


def _silu(x):
    return x * _sigmoid(x)


def _mod_kernel(c_ref, w_ref, b_ref, o_ref):
    a = _silu(c_ref[...])
    ah, al = _split2(a)
    wh, wl = _split2(w_ref[...])
    o_ref[...] = _dot(ah, wh) + _dot(ah, wl) + _dot(al, wh) + b_ref[...]


def _modulation(c_all, w_mod, b_mod, layer):
    rows, d = c_all.shape
    cols = w_mod.shape[-1]
    tn = 1024
    return pl.pallas_call(
        _mod_kernel,
        grid=(cols // tn,),
        in_specs=[
            pl.BlockSpec((rows, d), lambda j: (0, 0)),
            pl.BlockSpec((None, d, tn), lambda j: (layer, 0, j)),
            pl.BlockSpec((None, 1, tn), lambda j: (layer, 0, j)),
        ],
        out_specs=pl.BlockSpec((rows, tn), lambda j: (0, j)),
        out_shape=jax.ShapeDtypeStruct((rows, cols), F32),
        compiler_params=_cparams("arbitrary"),
        name="modulation",
    )(c_all, w_mod, b_mod)


def _rope(x, cos, sin):
    w = x.shape[1]
    lane = lax.broadcasted_iota(jnp.int32, x.shape, 1)
    partner = jnp.where((lane & 16) == 0, pltpu.roll(x, w - 16, 1), pltpu.roll(x, 16, 1))
    return x * cos + partner * sin


def _lower_bound(lb_ref, layer):
    logits = lb_ref[...]
    m = jnp.max(logits, axis=0, keepdims=True)
    ex = jnp.exp(logits - m)
    p = ex / jnp.sum(ex, axis=0, keepdims=True)
    lb = jnp.zeros_like(p[0:1])
    for j in range(1, layer + 1):
        lb = lb + p[j:j + 1]
    return lb


def _forget_gate(z, lb):
    log_sig = jnp.minimum(z, 0.0) - jnp.log1p(jnp.exp(-jnp.abs(z)))
    a = jnp.log(lb)
    bb = jnp.log1p(-lb) + log_sig
    log_f = jnp.maximum(a, bb) + jnp.log1p(jnp.exp(-jnp.abs(a - bb)))
    return log_f, (1.0 - lb) * _sigmoid(-z)


KV2_DIM = 2 * ATTN_KV_DIM
IN_Q0, IN_K0, IN_V0 = 0, ATTN_Q_DIM, ATTN_Q_DIM + KV2_DIM
IN_HG0 = IN_V0 + KV2_DIM
IN_GAB0 = IN_HG0 + 5 * HG_DIM
HGP_Q, HGP_GF, HGP_GB, HGP_KF, HGP_KB, HGP_V, HGP_OG = range(7)
IN_SLAB = 256


def _inproj_kernel(*refs, rope, layer):
    if rope:
        x_ref, sc_ref, sh_ref, w_ref, lbf_ref, lbb_ref, cos_ref, sin_ref, q_ref, kv_ref, hg_ref, gab_ref = refs
    else:
        x_ref, sc_ref, sh_ref, w_ref, lbf_ref, lbb_ref, q_ref, kv_ref, hg_ref, gab_ref = refs
    u = _layer_norm(x_ref[0]) * (1.0 + sc_ref[0]) + sh_ref[0]
    ub = u.astype(BF16)
    sw = IN_SLAB
    lb_f, lb_b = _lower_bound(lbf_ref, layer), _lower_bound(lbb_ref, layer)

    def slab(w0):
        return _dot(ub, w_ref[:, w0:w0 + sw])

    def rotated(x, c):
        return _rope(x, cos_ref[:, c:c + sw], sin_ref[:, c:c + sw]) if rope else x

    def q_slab(c):
        q_ref[0, :, c:c + sw] = rotated(slab(IN_Q0 + c), c)

    def k_slab(c):
        kv_ref[0, :, c:c + sw] = rotated(slab(IN_K0 + c), c)

    def v_slab(c):
        kv_ref[0, :, KV2_DIM + c:KV2_DIM + c + sw] = slab(IN_V0 + c)

    def hg_slab(src, c, fn, dsts):
        vals = fn(slab(IN_HG0 + src * HG_DIM + c))
        for j, val in zip(dsts, vals if isinstance(vals, tuple) else (vals,)):
            hg_ref[0, :, j * HG_DIM + c:j * HG_DIM + c + sw] = val

    def gate_slab(c):
        gab_ref[0, :, c:c + sw] = _sigmoid(slab(IN_GAB0 + c))

    heavy = []
    for c in range(0, HG_DIM, sw):
        heavy.append(functools.partial(hg_slab, 1, c, lambda z, c=c: _forget_gate(z, lb_f[:, c:c + sw]), (HGP_GF, HGP_KF)))
        heavy.append(functools.partial(hg_slab, 2, c, lambda z, c=c: _forget_gate(z, lb_b[:, c:c + sw]), (HGP_GB, HGP_KB)))
    light = [functools.partial(gate_slab, c) for c in range(0, gab_ref.shape[2], sw)]
    light += [functools.partial(q_slab, c) for c in range(0, ATTN_Q_DIM, sw)]
    for c in range(0, KV2_DIM, sw):
        light += [functools.partial(k_slab, c), functools.partial(v_slab, c)]
    for c in range(0, HG_DIM, sw):
        light.append(functools.partial(hg_slab, 0, c, _silu, (HGP_Q,)))
        light.append(functools.partial(hg_slab, 3, c, lambda v: v, (HGP_V,)))
        light.append(functools.partial(hg_slab, 4, c, _silu, (HGP_OG,)))
    per_heavy = len(light) // len(heavy)
    for i, task in enumerate(heavy):
        task()
        for t in light[i * per_heavy:(i + 1) * per_heavy]:
            t()
    for t in light[len(heavy) * per_heavy:]:
        t()


def _in_projection(x, sc, sh, w_in, lb_fw, lb_bw, layer, rope_tabs):
    b, l, d = x.shape
    tm = TOKEN_TILE
    per_batch = sc.shape[0] > 1
    mod_map = (lambda bi, i: (bi, 0, 0)) if per_batch else (lambda bi, i: (0, 0, 0))
    in_dim = w_in.shape[-1]
    in_specs = [
        pl.BlockSpec((1, tm, d), lambda bi, i: (bi, i, 0)),
        pl.BlockSpec((1, 1, d), mod_map),
        pl.BlockSpec((1, 1, d), mod_map),
        pl.BlockSpec((None, d, in_dim), lambda bi, i: (layer, 0, 0), pipeline_mode=pl.Buffered(1)),
        pl.BlockSpec(lb_fw.shape, lambda bi, i: (0, 0)),
        pl.BlockSpec(lb_bw.shape, lambda bi, i: (0, 0)),
    ]
    args = [x, sc, sh, w_in, lb_fw, lb_bw]
    if rope_tabs is not None:
        in_specs += [pl.BlockSpec((tm, ATTN_Q_DIM), lambda bi, i: (i, 0))] * 2
        args += list(rope_tabs)
    widths = (ATTN_Q_DIM, 2 * KV2_DIM, 7 * HG_DIM, 2 * d)
    return pl.pallas_call(
        functools.partial(_inproj_kernel, rope=rope_tabs is not None, layer=layer),
        grid=(b, l // tm),
        in_specs=in_specs,
        out_specs=[pl.BlockSpec((1, tm, wd), lambda bi, i: (bi, i, 0)) for wd in widths],
        out_shape=[jax.ShapeDtypeStruct((b, l, wd), F32) for wd in widths],
        compiler_params=_cparams("parallel", "arbitrary"),
        name="in_projection",
    )(*args)


def _relayout_w_in(w_in):
    q0, k0, v0, r0 = 0, ATTN_Q_DIM, ATTN_Q_DIM + ATTN_KV_DIM, ATTN_Q_DIM + 2 * ATTN_KV_DIM
    parts = [w_in[..., q0:k0]]
    for c0 in (k0, v0):
        for g in range(ATTN_KV_HEADS):
            head = w_in[..., c0 + g * HEAD_DIM:c0 + (g + 1) * HEAD_DIM]
            parts += [head, head]
    parts.append(w_in[..., r0:])
    return jnp.concatenate(parts, axis=-1).astype(BF16)


def _attend(q, kvs, sinks, ok):
    tq = q.shape[0]
    lc = kvs[0].shape[0]
    rep = ATTN_HEADS // ATTN_KV_HEADS
    tile = 2 * HEAD_DIM
    low_half = lax.broadcasted_iota(jnp.int32, (tq, tile), 1) < HEAD_DIM
    q = q * HEAD_DIM ** -0.5
    if ok is not None:
        mask = jnp.concatenate([jnp.ones((tq, lc), jnp.bool_), ok], axis=1)
        mask = jnp.concatenate([mask] * rep, axis=0)
    tiles = []
    for g in range(ATTN_KV_HEADS):
        kk = jnp.concatenate([t[:, g * tile:(g + 1) * tile] for t in kvs], axis=0).astype(BF16)
        vv = jnp.concatenate([t[:, KV2_DIM + g * tile:KV2_DIM + (g + 1) * tile] for t in kvs], axis=0).astype(BF16)
        heads = range(g * rep, (g + 1) * rep)
        qs = []
        for h in heads:
            q2 = q[:, (h // 2) * tile:(h // 2 + 1) * tile]
            qs.append(jnp.where(low_half == (h % 2 == 0), q2, 0.0))
        s = _dot_nt(jnp.concatenate(qs, axis=0).astype(BF16), kk)
        if ok is not None:
            s = jnp.where(mask, s, -jnp.inf)
        sink = jnp.concatenate([jnp.full((tq, 1), sinks[h], F32) for h in heads], axis=0)
        m = jnp.maximum(jnp.max(s, axis=1, keepdims=True), sink)
        e = jnp.exp(s - m)
        den = jnp.exp(sink - m) + jnp.sum(e, axis=1, keepdims=True)
        o = _dot(e.astype(BF16), vv) / den
        for r in range(0, rep, 2):
            tiles.append(jnp.where(low_half, o[r * tq:(r + 1) * tq], o[(r + 1) * tq:(r + 2) * tq]))
    return jnp.concatenate(tiles, axis=1)


def _attn_kernel(*refs, layer, nsteps, windowed):
    sinks = [refs[0][layer, h] for h in range(ATTN_HEADS)]
    if not windowed:
        _, q_ref, ckv_ref, o_ref = refs
        o_ref[0] = _attend(q_ref[0], [ckv_ref[0]], sinks, None).astype(BF16)
        return
    _, q_ref, kvp_ref, kvc_ref, kvn_ref, ckv_ref, o_ref = refs
    n = pl.program_id(1)
    ti = lax.broadcasted_iota(jnp.int32, (WBLOCK, 3 * WBLOCK), 0)
    sj = lax.broadcasted_iota(jnp.int32, (WBLOCK, 3 * WBLOCK), 1)
    rel = sj - WBLOCK - ti
    band = (rel <= WINDOW) & (rel >= -WINDOW)
    blocks = [kvp_ref[0]] + [kvc_ref[0, i * WBLOCK:(i + 1) * WBLOCK] for i in range(ATTN_STEP_BLOCKS)] + [kvn_ref[0]]
    for i in range(ATTN_STEP_BLOCKS):
        ok = band
        if i == 0:
            ok = ok & ((sj >= WBLOCK) | (n > 0))
        if i == ATTN_STEP_BLOCKS - 1:
            ok = ok & ((sj < 2 * WBLOCK) | (n < nsteps - 1))
        rows = slice(i * WBLOCK, (i + 1) * WBLOCK)
        o_ref[0, rows, :] = _attend(q_ref[0, rows, :], [ckv_ref[0]] + blocks[i:i + 3], sinks, ok).astype(BF16)


def _attention(q, kv, ckv, sink, layer):
    b, l, _ = q.shape
    lc = ckv.shape[1]
    windowed = kv is not None
    smem = pl.BlockSpec(memory_space=pltpu.SMEM)
    cspec = pl.BlockSpec((1, lc, 2 * KV2_DIM), lambda bi, n: (bi, 0, 0))
    if windowed:
        kvw = 2 * KV2_DIM
        sb = ATTN_STEP_BLOCKS
        tq = sb * WBLOCK
        nb = l // WBLOCK
        in_specs = [
            smem, pl.BlockSpec((1, tq, ATTN_Q_DIM), lambda bi, n: (bi, n, 0)),
            pl.BlockSpec((1, WBLOCK, kvw), lambda bi, n: (bi, jnp.maximum(n * sb - 1, 0), 0)),
            pl.BlockSpec((1, tq, kvw), lambda bi, n: (bi, n, 0)),
            pl.BlockSpec((1, WBLOCK, kvw), lambda bi, n: (bi, jnp.minimum((n + 1) * sb, nb - 1), 0)),
            cspec,
        ]
        args = (sink, q, kv, kv, kv, ckv)
    else:
        tq = WBLOCK
        in_specs = [smem, pl.BlockSpec((1, tq, ATTN_Q_DIM), lambda bi, n: (bi, n, 0)), cspec]
        args = (sink, q, ckv)
    return pl.pallas_call(
        functools.partial(_attn_kernel, layer=layer, nsteps=l // tq, windowed=windowed),
        grid=(b, l // tq),
        in_specs=in_specs,
        out_specs=pl.BlockSpec((1, tq, ATTN_Q_DIM), lambda bi, n: (bi, n, 0)),
        out_shape=jax.ShapeDtypeStruct((b, l, ATTN_Q_DIM), BF16),
        compiler_params=_cparams("parallel", "arbitrary"),
        name="window_attention" if windowed else "context_attention",
    )(*args)


def _block_ref_rows(cum, blk, r):
    c, w = cum.shape
    if blk % 8 == 0:
        x = cum.reshape(c // blk, blk, w)
        return jnp.broadcast_to(x[:, r:r + 1, :], x.shape).reshape(c, w)
    pos = lax.broadcasted_iota(jnp.int32, cum.shape, 0) & (blk - 1)
    out = cum
    for p in range(blk):
        delta = r - p
        if delta == 0:
            continue
        out = jnp.where(pos == p, pltpu.roll(cum, (-delta) % c, 0), out)
    return out


def _hgrn_kernel(*refs, has_init):
    if has_init:
        (tri_ref, lm_ref, dm_ref, qf_ref, gf_ref, kf_ref, vf_ref, qb_ref, gb_ref, kb_ref, vb_ref, s0_ref,
         of_ref, ob_ref, s_ref, cum_ref) = refs
    else:
        (tri_ref, lm_ref, dm_ref, qf_ref, gf_ref, kf_ref, vf_ref, qb_ref, gb_ref, kb_ref, vb_ref,
         of_ref, ob_ref, s_ref, cum_ref) = refs

    @pl.when(pl.program_id(1) == 0)
    def _():
        if has_init:
            s_ref[...] = s0_ref[...]
        else:
            s_ref[...] = jnp.zeros(s_ref.shape, F32)

    tt = qf_ref.shape[1]
    c = tri_ref.shape[1]
    nch = tt // c
    row = lax.broadcasted_iota(jnp.int32, (c, HG_DK), 0)
    levels = []
    hs = c // 2
    while hs >= 1:
        levels.append(hs)
        hs //= 2
    dirs = ((qf_ref, gf_ref, kf_ref, vf_ref, of_ref), (qb_ref, gb_ref, kb_ref, vb_ref, ob_ref))

    def direct_dd(cum, backward):
        return cum - _block_ref_rows(cum, HG_DIRECT, HG_DIRECT - 1 if backward else 0)

    worst = jnp.zeros((c, HG_DK), F32)
    for d, (_, g_ref, _, _, _) in enumerate(dirs):
        for h in range(HG_HEADS):
            for ci in range(nch):
                g3 = _split3(g_ref[0, ci * c:(ci + 1) * c, h * HG_DK:(h + 1) * HG_DK])
                cum = _dot(tri_ref[d], g3[0]) + _dot(tri_ref[d], g3[1]) + _dot(tri_ref[d], g3[2])
                cum_ref[(d * HG_HEADS + h) * nch + ci] = cum
                worst = jnp.maximum(worst, -direct_dd(cum, d == 1))
    in_range = jnp.max(worst) <= HG_EXP_LIMIT

    def scan(direct):
        tree = [(li, hs) for li, hs in enumerate(levels) if not direct or hs >= HG_DIRECT]
        for d, (q_ref, g_ref, k_ref, v_ref, o_ref) in enumerate(dirs):
            backward = d == 1
            q_halves = {li: ((row & (2 * hs - 1)) >= hs) != backward for li, hs in tree}
            for h in range(HG_HEADS):
                sl = slice(h * HG_DK, (h + 1) * HG_DK)
                q_all, k_all, v_all = q_ref[0, :, sl], k_ref[0, :, sl], v_ref[0, :, sl]
                st = s_ref[0, d, h]
                chunks = range(nch)
                for ci in (reversed(chunks) if backward else chunks):
                    rows = slice(ci * c, (ci + 1) * c)
                    q, kk, v = q_all[rows], k_all[rows], v_all[rows]
                    qb, kb, vb = q.astype(BF16), kk.astype(BF16), v.astype(BF16)
                    cum = cum_ref[(d * HG_HEADS + h) * nch + ci]
                    o = _dot_nt((q * jnp.exp(cum)).astype(BF16), st.astype(BF16))
                    if not direct:
                        o = o + jnp.sum(q * kk, axis=1, keepdims=True) * v
                    scores = jnp.zeros((c, c), F32)
                    for li, hs in tree:
                        blk = 2 * hs
                        dd = cum - _block_ref_rows(cum, blk, hs if backward else hs - 1)
                        e = jnp.exp(-jnp.abs(dd)).astype(BF16)
                        x = jnp.where(q_halves[li], qb, kb) * e
                        scores = scores + _dot_nt(x, x) * lm_ref[d, li]
                    if direct:
                        dd = direct_dd(cum, backward)
                        xq = qb * jnp.exp(dd).astype(BF16)
                        xk = kb * jnp.exp(-dd).astype(BF16)
                        scores = scores + jnp.where(dm_ref[d] > 0.5, _dot_nt(xq, xk), 0.0)
                    o = o + _dot(scores.astype(BF16), vb)
                    o_ref[0, rows, sl] = o
                    tot = cum[0:1] if backward else cum[c - 1:c]
                    kdec = kk * jnp.exp(tot - cum)
                    st = st * jnp.exp(tot) + _dot_tn(vb, kdec.astype(BF16))
                s_ref[0, d, h] = st

    pl.when(in_range)(lambda: scan(True))
    pl.when(jnp.logical_not(in_range))(lambda: scan(False))


def _hgrn_scan(hgp, consts, s0):
    b, l, _ = hgp.shape
    tt = TOKEN_TILE
    ns = l // tt
    has_init = s0 is not None
    tri, lmask, dmask = consts
    fwd = lambda j: pl.BlockSpec((1, tt, HG_DIM), lambda bi, i: (bi, i, j))
    bwd = lambda j: pl.BlockSpec((1, tt, HG_DIM), lambda bi, i: (bi, ns - 1 - i, j))
    sspec = pl.BlockSpec((1, 2, HG_HEADS, HG_DV, HG_DK), lambda bi, i: (bi, 0, 0, 0, 0))
    in_specs = [
        pl.BlockSpec(tri.shape, lambda bi, i: (0, 0, 0)),
        pl.BlockSpec(lmask.shape, lambda bi, i: (0, 0, 0, 0)),
        pl.BlockSpec(dmask.shape, lambda bi, i: (0, 0, 0)),
        fwd(HGP_Q), fwd(HGP_GF), fwd(HGP_KF), fwd(HGP_V), bwd(HGP_Q), bwd(HGP_GB), bwd(HGP_KB), bwd(HGP_V),
    ]
    args = [tri, lmask, dmask] + [hgp] * 8
    if has_init:
        in_specs.append(sspec)
        args.append(s0)
    c = tri.shape[1]
    return pl.pallas_call(
        functools.partial(_hgrn_kernel, has_init=has_init),
        grid=(b, ns),
        in_specs=in_specs,
        out_specs=[
            pl.BlockSpec((1, tt, HG_DIM), lambda bi, i: (bi, i, 0)),
            pl.BlockSpec((1, tt, HG_DIM), lambda bi, i: (bi, ns - 1 - i, 0)),
            sspec,
        ],
        out_shape=[
            jax.ShapeDtypeStruct((b, l, HG_DIM), F32),
            jax.ShapeDtypeStruct((b, l, HG_DIM), F32),
            jax.ShapeDtypeStruct((b, 2, HG_HEADS, HG_DV, HG_DK), F32),
        ],
        scratch_shapes=[pltpu.VMEM((2 * HG_HEADS * (tt // c), c, HG_DK), F32)],
        compiler_params=_cparams("parallel", "arbitrary"),
        name="hgrn_scan",
    )(*args)


def _hgrn_constants(c):
    t = lax.broadcasted_iota(jnp.int32, (c, c), 0)
    s = lax.broadcasted_iota(jnp.int32, (c, c), 1)
    tri = jnp.stack([s <= t, s >= t]).astype(BF16)
    masks, direct = [], []
    for backward in (False, True):
        per_level = []
        hs = c // 2
        while hs >= 1:
            blk = 2 * hs
            same = (t // blk) == (s // blk)
            t_late, s_late = (t % blk) >= hs, (s % blk) >= hs
            pair = (jnp.logical_not(t_late) & s_late) if backward else (t_late & jnp.logical_not(s_late))
            per_level.append(same & pair)
            hs //= 2
        masks.append(jnp.stack(per_level))
        direct.append(((t // HG_DIRECT) == (s // HG_DIRECT)) & ((s >= t) if backward else (s <= t)))
    return tri, jnp.stack(masks).astype(F32), jnp.stack(direct).astype(F32)


def _mixout_kernel(attn_ref, of_ref, ob_ref, gh_ref, ga_ref, gb_ref, x_ref, g1_ref, sc2_ref, sh2_ref,
                   ng_ref, wa_ref, wh_ref, wo_ref, lng_ref, lnb_ref, wrh_ref, wrl_ref,
                   xo_ref, u2_ref, aff_ref, *, alpha):
    for r0 in range(0, x_ref.shape[1], TOKEN_TILE):
        rows = slice(r0, r0 + TOKEN_TILE)
        o = of_ref[0, rows, :] + ob_ref[0, rows, :]
        parts = []
        for h in range(HG_HEADS):
            oh = o[:, h * HG_DV:(h + 1) * HG_DV]
            parts.append(oh * lax.rsqrt(jnp.mean(oh * oh, axis=-1, keepdims=True) + LN_EPS))
        hg = jnp.concatenate(parts, axis=1) * ng_ref[...] * gh_ref[0, rows, :]
        ya = _dot(attn_ref[0, rows, :], wa_ref[...])
        yh = _dot(hg.astype(BF16), wh_ref[...])
        mix = ga_ref[0, rows, :] * ya + gb_ref[0, rows, :] * yh
        y = _dot(mix.astype(BF16), wo_ref[...])
        xn = _layer_norm(alpha * x_ref[0, rows, :] + g1_ref[0] * y) * lng_ref[...] + lnb_ref[...]
        xo_ref[0, rows, :] = xn
        u2 = _layer_norm(xn) * (1.0 + sc2_ref[0]) + sh2_ref[0]
        u2_ref[0, rows, :] = u2.astype(BF16)
        uh, ul = _split2(u2)
        lg = _dot(uh, wrh_ref[...]) + _dot(ul, wrh_ref[...]) + _dot(uh, wrl_ref[...])
        lg = jnp.where(lax.broadcasted_iota(jnp.int32, lg.shape, 1) < N_EXPERTS, lg, -jnp.inf)
        ex = jnp.exp(lg - jnp.max(lg, axis=1, keepdims=True))
        aff = ex / jnp.sum(ex, axis=1, keepdims=True)
        aff_ref[0, :, rows] = aff.T[0:N_EXPERTS, :]


def _mixer_output(attn, o_f, o_b, hgp, gab, x, g1, sc2, sh2, norm_g, w_a, w_h, w_o, ln_g, ln_b,
                  wr_hi, wr_lo, layer, alpha):
    b, l, d = x.shape
    tm = min(l, MIXER_STEP_TILES * TOKEN_TILE)
    per_batch = g1.shape[0] > 1
    mod_map = (lambda bi, i: (bi, 0, 0)) if per_batch else (lambda bi, i: (0, 0, 0))
    tok = lambda w, j=0: pl.BlockSpec((1, tm, w), lambda bi, i: (bi, i, j))
    lay2 = lambda r, c: pl.BlockSpec((None, r, c), lambda bi, i: (layer, 0, 0))
    in_specs = [
        tok(ATTN_Q_DIM), tok(HG_DIM), tok(HG_DIM), tok(HG_DIM, HGP_OG), tok(d, 0), tok(d, 1), tok(d),
        pl.BlockSpec((1, 1, d), mod_map), pl.BlockSpec((1, 1, d), mod_map), pl.BlockSpec((1, 1, d), mod_map),
        lay2(1, HG_DIM), lay2(ATTN_Q_DIM, d), lay2(HG_DIM, d), lay2(d, d),
        pl.BlockSpec((None, None, 1, d), lambda bi, i: (layer, 0, 0, 0)),
        pl.BlockSpec((None, None, 1, d), lambda bi, i: (layer, 0, 0, 0)),
        lay2(d, ROUTER_LANES), lay2(d, ROUTER_LANES),
    ]
    return pl.pallas_call(
        functools.partial(_mixout_kernel, alpha=alpha),
        grid=(b, l // tm),
        in_specs=in_specs,
        out_specs=[tok(d), tok(d), pl.BlockSpec((1, N_EXPERTS, tm), lambda bi, i: (bi, 0, i))],
        out_shape=[
            jax.ShapeDtypeStruct((b, l, d), F32),
            jax.ShapeDtypeStruct((b, l, d), BF16),
            jax.ShapeDtypeStruct((b, N_EXPERTS, l), F32),
        ],
        compiler_params=_cparams("parallel", "arbitrary"),
        name="mixer_output",
    )(attn, o_f, o_b, hgp, gab, gab, x, g1, sc2, sh2, norm_g, w_a, w_h, w_o, ln_g, ln_b, wr_hi, wr_lo)


def _select_kernel(aff_ref, triu_ref, rk_ref, rkt_ref, st_ref, *, cap):
    a = aff_ref[0]
    ne, n = a.shape
    ts = triu_ref.shape[0]
    bits = lax.bitcast_convert_type(a, jnp.int32)
    capf = jnp.float32(cap)

    def count_ge(v):
        return jnp.sum(jnp.where(bits >= v, 1.0, 0.0), axis=1, keepdims=True)

    def body(_, carry):
        lo, hi = carry
        mid = lo + ((hi - lo) >> 1)
        ok = count_ge(mid) >= capf
        return jnp.where(ok, mid, lo), jnp.where(ok, hi, mid)

    lo0 = jnp.zeros((ne, 1), jnp.int32)
    hi0 = jnp.full((ne, 1), 0x7F800000, jnp.int32)
    thr, _ = lax.fori_loop(0, 31, body, (lo0, hi0))
    gt = bits > thr
    eq = bits == thr
    need = capf - jnp.sum(jnp.where(gt, 1.0, 0.0), axis=1, keepdims=True)
    triu = triu_ref[...]
    lane = lax.broadcasted_iota(jnp.int32, st_ref.shape[1:], 1)
    eq_carry = jnp.zeros((ne, 1), F32)
    sel_carry = jnp.zeros((ne, 1), F32)
    starts = jnp.zeros(st_ref.shape[1:], F32)
    for t in range(n // ts):
        sl = slice(t * ts, (t + 1) * ts)
        eq_t = eq[:, sl]
        eq_pref = _dot(jnp.where(eq_t, 1.0, 0.0).astype(BF16), triu) + eq_carry
        sel_t = gt[:, sl] | (eq_t & (eq_pref <= need))
        sel_f = jnp.where(sel_t, 1.0, 0.0)
        sel_pref = _dot(sel_f.astype(BF16), triu) + sel_carry
        rk_t = jnp.where(sel_t, sel_pref - 1.0, -1.0)
        rk_ref[0, :, sl] = rk_t
        rkt_ref[0, sl, :] = rk_t.T
        starts = jnp.where(lane == t, sel_carry, starts)
        eq_carry = eq_pref[:, ts - 1:ts]
        sel_carry = sel_pref[:, ts - 1:ts]
    st_ref[0] = starts


def _select(aff, triu, cap):
    b, ne, n = aff.shape
    return pl.pallas_call(
        functools.partial(_select_kernel, cap=cap),
        grid=(b,),
        in_specs=[
            pl.BlockSpec((1, ne, n), lambda bi: (bi, 0, 0)),
            pl.BlockSpec(triu.shape, lambda bi: (0, 0)),
        ],
        out_specs=[
            pl.BlockSpec((1, ne, n), lambda bi: (bi, 0, 0)),
            pl.BlockSpec((1, n, ne), lambda bi: (bi, 0, 0)),
            pl.BlockSpec((1, ne, 128), lambda bi: (bi, 0, 0)),
        ],
        out_shape=[
            jax.ShapeDtypeStruct((b, ne, n), F32),
            jax.ShapeDtypeStruct((b, n, ne), F32),
            jax.ShapeDtypeStruct((b, ne, 128), F32),
        ],
        compiler_params=_cparams("parallel"),
        name="expert_choice_select",
    )(aff, triu)


def _slot_window(cap, tile):
    return min(cap, tile + SLOT_ALIGN)


def _aligned_start(s0, cap, w):
    return jnp.minimum((s0 // SLOT_ALIGN) * SLOT_ALIGN, cap - w)


def _window_start(s0, cap, tile):
    return _aligned_start(s0, cap, _slot_window(cap, tile))


def _ffn_kernel(st_ref, u_ref, rk_ref, aff_ref, wg_ref, wu_ref, wd_ref, y_ref, xs_ref,
                *, cap, ns, unroll, expert_major):
    b, e = pl.program_id(0), pl.program_id(1)
    if expert_major:
        b, e = e, b
    d = u_ref.shape[2]
    nt = rk_ref.shape[2]
    w_full = _slot_window(cap, TOKEN_TILE)
    widths = [w for w in DISPATCH_WINDOWS if w < w_full] + [w_full]
    sub = TOKEN_TILE // ROUTE_TILE
    base = (b * N_EXPERTS + e) * (ns + 1)

    def bounds(t):
        return st_ref[base + t * sub], st_ref[base + (t + 1) * sub]

    def dispatch(t, w):
        s0, s1 = bounds(t)
        a = pl.multiple_of(_aligned_start(s0, cap, w), SLOT_ALIGN)
        slot = a + lax.broadcasted_iota(jnp.int32, (w, 1), 0)
        hit = slot.astype(F32) == rk_ref[0, 0, pl.ds(t, 1), :]
        mine = (slot >= s0) & (slot < s1)
        tok0 = pl.multiple_of(t * TOKEN_TILE, TOKEN_TILE)
        rows = _dot(jnp.where(hit, 1.0, 0.0).astype(BF16), u_ref[0, pl.ds(tok0, TOKEN_TILE), :])
        wts = jnp.sum(jnp.where(hit, aff_ref[0, 0, pl.ds(t, 1), :], 0.0), axis=1, keepdims=True)
        vals = jnp.concatenate([rows, jnp.broadcast_to(wts, (w, 128))], axis=1)
        pltpu.store(xs_ref.at[pl.ds(a, w), :], vals, mask=jnp.broadcast_to(mine, vals.shape))

    def run(w):
        def group(i, carry):
            for j in range(unroll):
                dispatch(i * unroll + j, w)
            return carry

        lax.fori_loop(0, nt // unroll, group, 0)

    def fits(t, ok):
        s0, s1 = bounds(t)
        return tuple(o & (s1 - _aligned_start(s0, cap, w) <= w) for o, w in zip(ok, widths[:-1]))

    ok = lax.fori_loop(0, nt, fits, (True,) * (len(widths) - 1))
    wider_needed = True
    for i, w in enumerate(widths):
        fit = ok[i] if i < len(ok) else True
        pl.when(jnp.logical_and(wider_needed, fit))(functools.partial(run, w))
        wider_needed = jnp.logical_and(wider_needed, jnp.logical_not(fit))

    rc = min(cap, 256)
    for c in range(cap // rc):
        rows = slice(c * rc, (c + 1) * rc)
        xb = xs_ref[rows, 0:d].astype(BF16)
        h = _silu(_dot(xb, wg_ref[0])) * _dot(xb, wu_ref[0])
        y_ref[0, 0, rows, :] = (_dot(h.astype(BF16), wd_ref[0]) * xs_ref[rows, d:d + 1]).astype(BF16)


def _expert_ffn(starts, u2, rk, aff, w_gate, w_up, w_down, layer, cap):
    b, n, d = u2.shape
    ff = w_gate.shape[-1]
    ns = n // ROUTE_TILE
    nt = n // TOKEN_TILE
    rk4 = rk.reshape(b, N_EXPERTS, nt, TOKEN_TILE)
    aff4 = aff.reshape(b, N_EXPERTS, nt, TOKEN_TILE)
    expert_major = n * d < 3 * d * ff
    grid = (N_EXPERTS, b) if expert_major else (b, N_EXPERTS)
    be = (lambda i, j: (j, i)) if expert_major else (lambda i, j: (i, j))
    wspec = lambda r, c: pl.BlockSpec((None, 1, r, c), lambda i, j, st: (layer, be(i, j)[1], 0, 0))
    rspec = pl.BlockSpec((1, 1, nt, TOKEN_TILE), lambda i, j, st: (*be(i, j), 0, 0))
    grid_spec = pltpu.PrefetchScalarGridSpec(
        num_scalar_prefetch=1,
        grid=grid,
        in_specs=[
            pl.BlockSpec((1, n, d), lambda i, j, st: (be(i, j)[0], 0, 0),
                         pipeline_mode=None if expert_major else pl.Buffered(1)),
            rspec, rspec,
            wspec(d, ff), wspec(d, ff), wspec(ff, d),
        ],
        out_specs=pl.BlockSpec((1, 1, cap, d), lambda i, j, st: (*be(i, j), 0, 0)),
        scratch_shapes=[pltpu.VMEM((cap, d + 128), F32)],
    )
    return pl.pallas_call(
        functools.partial(_ffn_kernel, cap=cap, ns=ns, unroll=min(nt, 4), expert_major=expert_major),
        grid_spec=grid_spec,
        out_shape=jax.ShapeDtypeStruct((b, N_EXPERTS, cap, d), BF16),
        compiler_params=_cparams("parallel", "arbitrary"),
        name="expert_ffn",
    )(starts, u2, rk4, aff4, w_gate, w_up, w_down)


def _combine_kernel(*refs, cap, ns, alpha):
    st_ref, rkt_ref, x_ref, g2_ref, lng_ref, lnb_ref = refs[:6]
    y_refs = refs[6:6 + N_EXPERTS]
    o_ref = refs[6 + N_EXPERTS]
    b, t = pl.program_id(0), pl.program_id(1)
    sub = TOKEN_TILE // ROUTE_TILE
    w = _slot_window(cap, ROUTE_TILE)
    for half in range(sub):
        rows = slice(half * ROUTE_TILE, (half + 1) * ROUTE_TILE)
        acc = jnp.zeros((ROUTE_TILE, x_ref.shape[2]), F32)
        for e in range(N_EXPERTS):
            base = (b * N_EXPERTS + e) * (ns + 1) + t * sub
            a_tile = _window_start(st_ref[base], cap, TOKEN_TILE)
            a = _window_start(st_ref[base + half], cap, ROUTE_TILE)
            off = pl.multiple_of(a - a_tile, SLOT_ALIGN)
            slot = (a + lax.broadcasted_iota(jnp.int32, (1, w), 1)).astype(F32)
            hit = rkt_ref[0, rows, e:e + 1] == slot
            acc = acc + _dot(jnp.where(hit, 1.0, 0.0).astype(BF16), y_refs[e][pl.ds(off, w), :])
        xr = alpha * x_ref[0, rows, :] + g2_ref[0] * acc
        o_ref[0, rows, :] = _layer_norm(xr) * lng_ref[...] + lnb_ref[...]


def _combine(starts, rkt, x, g2, ln_g, ln_b, y, layer, cap, alpha):
    b, n, d = x.shape
    tt = TOKEN_TILE
    ns = n // ROUTE_TILE
    sub = TOKEN_TILE // ROUTE_TILE
    w = _slot_window(cap, tt)
    per_batch = g2.shape[0] > 1
    mod_map = (lambda bi, t, st: (bi, 0, 0)) if per_batch else (lambda bi, t, st: (0, 0, 0))

    def y_spec(e):
        def index_map(bi, t, st):
            be = bi * N_EXPERTS + e
            a = _window_start(st[be * (ns + 1) + t * sub], cap, tt)
            return pl.multiple_of(be * cap + a, SLOT_ALIGN), 0
        return pl.BlockSpec((pl.Element(w), pl.Element(d)), index_map)

    y = y.reshape(b * N_EXPERTS * cap, d)
    grid_spec = pltpu.PrefetchScalarGridSpec(
        num_scalar_prefetch=1,
        grid=(b, n // tt),
        in_specs=[
            pl.BlockSpec((1, tt, N_EXPERTS), lambda bi, t, st: (bi, t, 0)),
            pl.BlockSpec((1, tt, d), lambda bi, t, st: (bi, t, 0)),
            pl.BlockSpec((1, 1, d), mod_map),
            pl.BlockSpec((None, None, 1, d), lambda bi, t, st: (layer, 1, 0, 0)),
            pl.BlockSpec((None, None, 1, d), lambda bi, t, st: (layer, 1, 0, 0)),
        ] + [y_spec(e) for e in range(N_EXPERTS)],
        out_specs=pl.BlockSpec((1, tt, d), lambda bi, t, st: (bi, t, 0)),
    )
    return pl.pallas_call(
        functools.partial(_combine_kernel, cap=cap, ns=ns, alpha=alpha),
        grid_spec=grid_spec,
        out_shape=jax.ShapeDtypeStruct((b, n, d), F32),
        compiler_params=_cparams("parallel", "arbitrary"),
        name="moe_combine",
    )(starts, rkt, x, g2, ln_g, ln_b, *([y] * N_EXPERTS))


def _combine_packed_kernel(st_ref, rkt_ref, x_ref, g2_ref, lng_ref, lnb_ref, y_hbm, o_ref, ybuf, sem,
                           *, cap, ns, alpha):
    b, t = pl.program_id(0), pl.program_id(1)
    nt = pl.num_programs(1)
    step = b * nt + t
    sub = TOKEN_TILE // ROUTE_TILE
    w = COMBINE_WINDOW

    def window_start(bb, tt, half, e):
        return _aligned_start(st_ref[(bb * N_EXPERTS + e) * (ns + 1) + tt * sub + half], cap, w)

    def copies(bb, tt, slot):
        out = []
        for half in range(sub):
            for e in range(N_EXPERTS):
                row0 = pl.multiple_of((bb * N_EXPERTS + e) * cap + window_start(bb, tt, half, e), SLOT_ALIGN)
                out.append(pltpu.make_async_copy(y_hbm.at[pl.ds(row0, w), :],
                                                 ybuf.at[slot, half, pl.ds(e * w, w), :], sem.at[slot]))
        return out

    slot = step % 2

    @pl.when(step == 0)
    def _():
        for cp in copies(b, t, 0):
            cp.start()

    @pl.when(step + 1 < pl.num_programs(0) * nt)
    def _():
        wrap = t + 1 == nt
        for cp in copies(jnp.where(wrap, b + 1, b), jnp.where(wrap, 0, t + 1), 1 - slot):
            cp.start()

    for cp in copies(b, t, slot):
        cp.wait()

    lanes = 128
    per = lanes // w
    lane = lax.broadcasted_iota(jnp.int32, (1, lanes), 1)
    for half in range(sub):
        rows = slice(half * ROUTE_TILE, (half + 1) * ROUTE_TILE)
        hits = []
        for e0 in range(0, N_EXPERTS, per):
            slot_ids = window_start(b, t, half, e0) + lane
            rank = rkt_ref[0, rows, e0:e0 + 1]
            for j in range(1, per):
                own = lane >= j * w
                slot_ids = jnp.where(own, window_start(b, t, half, e0 + j) + lane - j * w, slot_ids)
                rank = jnp.where(own, rkt_ref[0, rows, e0 + j:e0 + j + 1], rank)
            hits.append(jnp.where(rank == slot_ids.astype(F32), 1.0, 0.0).astype(BF16))
        acc = _dot(jnp.concatenate(hits, axis=1), ybuf[slot, half])
        xr = alpha * x_ref[0, rows, :] + g2_ref[0] * acc
        o_ref[0, rows, :] = _layer_norm(xr) * lng_ref[...] + lnb_ref[...]


def _combine_packed(starts, rkt, x, g2, ln_g, ln_b, y, layer, cap, alpha):
    b, n, d = x.shape
    tt = TOKEN_TILE
    ns = n // ROUTE_TILE
    sub = TOKEN_TILE // ROUTE_TILE
    per_batch = g2.shape[0] > 1
    mod_map = (lambda bi, t, st: (bi, 0, 0)) if per_batch else (lambda bi, t, st: (0, 0, 0))
    grid_spec = pltpu.PrefetchScalarGridSpec(
        num_scalar_prefetch=1,
        grid=(b, n // tt),
        in_specs=[
            pl.BlockSpec((1, tt, N_EXPERTS), lambda bi, t, st: (bi, t, 0)),
            pl.BlockSpec((1, tt, d), lambda bi, t, st: (bi, t, 0)),
            pl.BlockSpec((1, 1, d), mod_map),
            pl.BlockSpec((None, None, 1, d), lambda bi, t, st: (layer, 1, 0, 0)),
            pl.BlockSpec((None, None, 1, d), lambda bi, t, st: (layer, 1, 0, 0)),
            pl.BlockSpec(memory_space=pl.ANY),
        ],
        out_specs=pl.BlockSpec((1, tt, d), lambda bi, t, st: (bi, t, 0)),
        scratch_shapes=[
            pltpu.VMEM((2, sub, N_EXPERTS * COMBINE_WINDOW, d), BF16),
            pltpu.SemaphoreType.DMA((2,)),
        ],
    )
    return pl.pallas_call(
        functools.partial(_combine_packed_kernel, cap=cap, ns=ns, alpha=alpha),
        grid_spec=grid_spec,
        out_shape=jax.ShapeDtypeStruct((b, n, d), F32),
        compiler_params=_cparams("arbitrary", "arbitrary"),
        name="moe_combine_packed",
    )(starts, rkt, x, g2, ln_g, ln_b, y.reshape(b * N_EXPERTS * cap, d))


def _moe(x, u2, aff, g2, ln_g, ln_b, triu, w_gate, w_up, w_down, layer, alpha):
    b, n, _ = x.shape
    cap = CAPACITY_FACTOR * n // N_EXPERTS
    ns = n // ROUTE_TILE
    rk, rkt, st = _select(aff, triu, cap)
    starts = jnp.concatenate([st[:, :, :ns], jnp.full((b, N_EXPERTS, 1), cap, F32)], axis=2)
    starts = starts.astype(jnp.int32).reshape(-1)
    y = _expert_ffn(starts, u2, rk, aff, w_gate, w_up, w_down, layer, cap)
    args = (starts, rkt, x, g2, ln_g, ln_b, y)
    general = lambda *a: _combine(*a, layer, cap, alpha)
    if cap < COMBINE_WINDOW:
        return general(*args)
    s0 = starts.reshape(b, N_EXPERTS, ns + 1)
    fits = jnp.all(s0[:, :, 1:] - _aligned_start(s0[:, :, :-1], cap, COMBINE_WINDOW) <= COMBINE_WINDOW)
    return lax.cond(fits, lambda *a: _combine_packed(*a, layer, cap, alpha), general, *args)


def _rope_tables(l):
    pos = jnp.arange(l)
    nf = HEAD_DIM // 4
    inv = ROPE_THETA ** (-jnp.arange(nf, dtype=F32) / nf)
    ang_r = (pos // GRID_W).astype(F32)[:, None] * inv[None, :]
    ang_c = (pos % GRID_W).astype(F32)[:, None] * inv[None, :]
    cos = jnp.concatenate([jnp.cos(ang_r)] * 2 + [jnp.cos(ang_c)] * 2, axis=1)
    sin = jnp.concatenate([-jnp.sin(ang_r), jnp.sin(ang_r), -jnp.sin(ang_c), jnp.sin(ang_c)], axis=1)
    return jnp.tile(cos, (1, ATTN_HEADS)), jnp.tile(sin, (1, ATTN_HEADS))


def kernel(x, c, ctx, c_ctx, w_mod, b_mod, w_in, attn_sink, hgrn_lb_fw, hgrn_lb_bw, hgrn_norm_g,
           w_branch_attn, w_branch_hgrn, w_out, w_router, w_gate, w_up, w_down, ln_g, ln_b):
    depth, d = w_mod.shape[0], x.shape[-1]
    b = x.shape[0]
    assert x.shape[1] % (4 * TOKEN_TILE) == 0 and ctx.shape[1] == TOKEN_TILE
    alpha = (2 * depth) ** 0.25
    mod_rows = -(-(b + 1) // 8) * 8
    c_all = jnp.concatenate([c, c_ctx[None, :], jnp.zeros((mod_rows - b - 1, d), F32)], axis=0)
    b_mod3 = b_mod[:, None, :]
    norm_g3 = hgrn_norm_g[:, None, :]
    ln_g4, ln_b4 = ln_g[:, :, None, :], ln_b[:, :, None, :]
    w_in_b = _relayout_w_in(w_in)
    w_a_b, w_h_b, w_o_b = w_branch_attn.astype(BF16), w_branch_hgrn.astype(BF16), w_out.astype(BF16)
    w_g_b, w_u_b, w_d_b = w_gate.astype(BF16), w_up.astype(BF16), w_down.astype(BF16)
    wr_pad = jnp.pad(w_router, ((0, 0), (0, 0), (0, ROUTER_LANES - N_EXPERTS)))
    wr_hi = wr_pad.astype(BF16)
    wr_lo = (wr_pad - wr_hi.astype(F32)).astype(BF16)
    rope_tabs = _rope_tables(x.shape[1])
    hg_consts = _hgrn_constants(HG_CHUNK)
    r_i = lax.broadcasted_iota(jnp.int32, (ROUTE_TILE, ROUTE_TILE), 0)
    c_i = lax.broadcasted_iota(jnp.int32, (ROUTE_TILE, ROUTE_TILE), 1)
    triu = (r_i <= c_i).astype(BF16)

    xc = ctx
    for l in range(depth):
        need_ctx = l < depth - 1
        mod = _modulation(c_all, w_mod, b_mod3, l)
        sh1, sc1, g1, sh2, sc2, g2 = [mod[:b, i * d:(i + 1) * d][:, None, :] for i in range(N_MOD)]
        sh1c, sc1c, g1c, sh2c, sc2c, g2c = [mod[b:b + 1, i * d:(i + 1) * d][:, None, :] for i in range(N_MOD)]
        cq, ckv, chg, cgab = _in_projection(xc, sc1c, sh1c, w_in_b, hgrn_lb_fw, hgrn_lb_bw, l, None)
        oc_f, oc_b, s_ctx = _hgrn_scan(chg, hg_consts, None)
        q, kv, hgp, gab = _in_projection(x, sc1, sh1, w_in_b, hgrn_lb_fw, hgrn_lb_bw, l, rope_tabs)
        o_f, o_b, _ = _hgrn_scan(hgp, hg_consts, s_ctx)
        attn = _attention(q, kv, ckv, attn_sink, l)
        x1, u2, aff = _mixer_output(attn, o_f, o_b, hgp, gab, x, g1, sc2, sh2, norm_g3, w_a_b, w_h_b, w_o_b,
                                    ln_g4, ln_b4, wr_hi, wr_lo, l, alpha)
        x = _moe(x1, u2, aff, g2, ln_g4, ln_b4, triu, w_g_b, w_u_b, w_d_b, l, alpha)
        if need_ctx:
            attn_c = _attention(cq, None, ckv, attn_sink, l)
            xc1, uc2, affc = _mixer_output(attn_c, oc_f, oc_b, chg, cgab, xc, g1c, sc2c, sh2c, norm_g3,
                                           w_a_b, w_h_b, w_o_b, ln_g4, ln_b4, wr_hi, wr_lo, l, alpha)
            xc = _moe(xc1, uc2, affc, g2c, ln_g4, ln_b4, triu, w_g_b, w_u_b, w_d_b, l, alpha)
    return x
```

```python
import functools

import jax
import jax.numpy as jnp
from jax import lax
from jax.experimental import pallas as pl
from jax.experimental.pallas import tpu as pltpu

F32 = jnp.float32
BF16 = jnp.bfloat16

GRID_W = 64
ATTN_HEADS = 8
ATTN_KV_HEADS = 2
HEAD_DIM = 64
WINDOW = 128
WBLOCK = 128
ROPE_THETA = 10000.0
HG_HEADS = 4
HG_DK = 128
HG_DV = 128
N_EXPERTS = 16
CAPACITY_FACTOR = 2
N_MOD = 6
LN_EPS = 1e-6
ATTN_Q_DIM = ATTN_HEADS * HEAD_DIM
ATTN_KV_DIM = ATTN_KV_HEADS * HEAD_DIM
HG_DIM = HG_HEADS * HG_DK

TOKEN_TILE = 256
HG_CHUNK = 128
ATTN_STEP_BLOCKS = 4
IN_STEP_TILES = 2
MIXER_STEP_TILES = 2
HG_DIRECT = 16
HG_EXP_LIMIT = 80.0
ROUTE_TILE = 128
SLOT_ALIGN = 16
DISPATCH_WINDOWS = (128,)
COMBINE_WINDOW = 64
SELECT_BISECTIONS = 48
SELECT_REFINEMENTS = 2
ROUTER_LANES = 128
VMEM_LIMIT_BYTES = 56 << 20


def _cparams(*sem):
    return pltpu.CompilerParams(dimension_semantics=sem, vmem_limit_bytes=VMEM_LIMIT_BYTES)


def _dot(a, b):
    return jnp.dot(a, b, preferred_element_type=F32)


def _dot_nt(a, b):
    return lax.dot_general(a, b, (((1,), (1,)), ((), ())), preferred_element_type=F32)


def _dot_tn(a, b):
    return lax.dot_general(a, b, (((0,), (0,)), ((), ())), preferred_element_type=F32)


def _split2(a):
    hi = a.astype(BF16)
    lo = (a - hi.astype(F32)).astype(BF16)
    return hi, lo


def _split3(a):
    hi = a.astype(BF16)
    r = a - hi.astype(F32)
    mid = r.astype(BF16)
    lo = (r - mid.astype(F32)).astype(BF16)
    return hi, mid, lo


def _layer_norm(x):
    mu = jnp.mean(x, axis=-1, keepdims=True)
    xc = x - mu
    var = jnp.mean(xc * xc, axis=-1, keepdims=True)
    return xc * lax.rsqrt(var + LN_EPS)


def _sigmoid(x):
    return 0.5 * jnp.tanh(0.5 * x) + 0.5


def _silu(x):
    return x * _sigmoid(x)


def _mod_kernel(c_ref, w_ref, b_ref, o_ref):
    a = _silu(c_ref[...])
    ah, al = _split2(a)
    wh, wl = _split2(w_ref[...])
    o_ref[...] = _dot(ah, wh) + _dot(ah, wl) + _dot(al, wh) + b_ref[...]


def _modulation(c_all, w_mod, b_mod, layer):
    rows, d = c_all.shape
    cols = w_mod.shape[-1]
    tn = 1024
    return pl.pallas_call(
        _mod_kernel,
        grid=(cols // tn,),
        in_specs=[
            pl.BlockSpec((rows, d), lambda j: (0, 0)),
            pl.BlockSpec((None, d, tn), lambda j: (layer, 0, j)),
            pl.BlockSpec((None, 1, tn), lambda j: (layer, 0, j)),
        ],
        out_specs=pl.BlockSpec((rows, tn), lambda j: (0, j)),
        out_shape=jax.ShapeDtypeStruct((rows, cols), F32),
        compiler_params=_cparams("arbitrary"),
        name="modulation",
    )(c_all, w_mod, b_mod)


def _rope(x, cos, sin):
    w = x.shape[1]
    lane = lax.broadcasted_iota(jnp.int32, x.shape, 1)
    partner = jnp.where((lane & 16) == 0, pltpu.roll(x, w - 16, 1), pltpu.roll(x, 16, 1))
    return x * cos + partner * sin


def _lower_bound(lb_ref, layer):
    logits = lb_ref[...]
    m = jnp.max(logits, axis=0, keepdims=True)
    ex = jnp.exp(logits - m)
    p = ex / jnp.sum(ex, axis=0, keepdims=True)
    lb = jnp.zeros_like(p[0:1])
    for j in range(1, layer + 1):
        lb = lb + p[j:j + 1]
    return lb


def _forget_gate(z, lb):
    log_sig = jnp.minimum(z, 0.0) - jnp.log1p(jnp.exp(-jnp.abs(z)))
    a = jnp.log(lb)
    bb = jnp.log1p(-lb) + log_sig
    log_f = jnp.maximum(a, bb) + jnp.log1p(jnp.exp(-jnp.abs(a - bb)))
    return log_f, (1.0 - lb) * _sigmoid(-z)


KV2_DIM = 2 * ATTN_KV_DIM
IN_Q0, IN_K0, IN_V0 = 0, ATTN_Q_DIM, ATTN_Q_DIM + KV2_DIM
IN_HG0 = IN_V0 + KV2_DIM
IN_GAB0 = IN_HG0 + 5 * HG_DIM
HGP_Q, HGP_GF, HGP_GB, HGP_KF, HGP_KB, HGP_V, HGP_OG = range(7)
IN_SLAB = 256


def _inproj_kernel(*refs, rope, layer):
    if rope:
        x_ref, sc_ref, sh_ref, w_ref, lbf_ref, lbb_ref, cos_ref, sin_ref, q_ref, kv_ref, hg_ref, gab_ref = refs
    else:
        x_ref, sc_ref, sh_ref, w_ref, lbf_ref, lbb_ref, q_ref, kv_ref, hg_ref, gab_ref = refs
    u = _layer_norm(x_ref[0]) * (1.0 + sc_ref[0]) + sh_ref[0]
    ub = u.astype(BF16)
    sw = IN_SLAB
    lb_f, lb_b = _lower_bound(lbf_ref, layer), _lower_bound(lbb_ref, layer)

    def slab(w0):
        return _dot(ub, w_ref[:, w0:w0 + sw])

    def rotated(x, c):
        return _rope(x, cos_ref[:, c:c + sw], sin_ref[:, c:c + sw]) if rope else x

    def q_slab(c):
        q_ref[0, :, c:c + sw] = rotated(slab(IN_Q0 + c), c)

    def k_slab(c):
        kv_ref[0, :, c:c + sw] = rotated(slab(IN_K0 + c), c)

    def v_slab(c):
        kv_ref[0, :, KV2_DIM + c:KV2_DIM + c + sw] = slab(IN_V0 + c)

    def hg_slab(src, c, fn, dsts):
        vals = fn(slab(IN_HG0 + src * HG_DIM + c))
        for j, val in zip(dsts, vals if isinstance(vals, tuple) else (vals,)):
            hg_ref[0, :, j * HG_DIM + c:j * HG_DIM + c + sw] = val

    def gate_slab(c):
        gab_ref[0, :, c:c + sw] = _sigmoid(slab(IN_GAB0 + c))

    heavy = []
    for c in range(0, HG_DIM, sw):
        heavy.append(functools.partial(hg_slab, 1, c, lambda z, c=c: _forget_gate(z, lb_f[:, c:c + sw]), (HGP_GF, HGP_KF)))
        heavy.append(functools.partial(hg_slab, 2, c, lambda z, c=c: _forget_gate(z, lb_b[:, c:c + sw]), (HGP_GB, HGP_KB)))
    light = [functools.partial(gate_slab, c) for c in range(0, gab_ref.shape[2], sw)]
    light += [functools.partial(q_slab, c) for c in range(0, ATTN_Q_DIM, sw)]
    for c in range(0, KV2_DIM, sw):
        light += [functools.partial(k_slab, c), functools.partial(v_slab, c)]
    for c in range(0, HG_DIM, sw):
        light.append(functools.partial(hg_slab, 0, c, _silu, (HGP_Q,)))
        light.append(functools.partial(hg_slab, 3, c, lambda v: v, (HGP_V,)))
        light.append(functools.partial(hg_slab, 4, c, _silu, (HGP_OG,)))
    per_heavy = len(light) // len(heavy)
    for i, task in enumerate(heavy):
        task()
        for t in light[i * per_heavy:(i + 1) * per_heavy]:
            t()
    for t in light[len(heavy) * per_heavy:]:
        t()


def _in_projection(x, sc, sh, w_in, lb_fw, lb_bw, layer, rope_tabs):
    b, l, d = x.shape
    tm = min(l, IN_STEP_TILES * TOKEN_TILE)
    per_batch = sc.shape[0] > 1
    mod_map = (lambda bi, i: (bi, 0, 0)) if per_batch else (lambda bi, i: (0, 0, 0))
    in_dim = w_in.shape[-1]
    in_specs = [
        pl.BlockSpec((1, tm, d), lambda bi, i: (bi, i, 0)),
        pl.BlockSpec((1, 1, d), mod_map),
        pl.BlockSpec((1, 1, d), mod_map),
        pl.BlockSpec((None, d, in_dim), lambda bi, i: (layer, 0, 0), pipeline_mode=pl.Buffered(1)),
        pl.BlockSpec(lb_fw.shape, lambda bi, i: (0, 0)),
        pl.BlockSpec(lb_bw.shape, lambda bi, i: (0, 0)),
    ]
    args = [x, sc, sh, w_in, lb_fw, lb_bw]
    if rope_tabs is not None:
        in_specs += [pl.BlockSpec((tm, ATTN_Q_DIM), lambda bi, i: (i, 0))] * 2
        args += list(rope_tabs)
    widths = (ATTN_Q_DIM, 2 * KV2_DIM, 7 * HG_DIM, 2 * d)
    return pl.pallas_call(
        functools.partial(_inproj_kernel, rope=rope_tabs is not None, layer=layer),
        grid=(b, l // tm),
        in_specs=in_specs,
        out_specs=[pl.BlockSpec((1, tm, wd), lambda bi, i: (bi, i, 0)) for wd in widths],
        out_shape=[jax.ShapeDtypeStruct((b, l, wd), F32) for wd in widths],
        compiler_params=_cparams("parallel", "arbitrary"),
        name="in_projection",
    )(*args)


def _relayout_w_in(w_in):
    q0, k0, v0, r0 = 0, ATTN_Q_DIM, ATTN_Q_DIM + ATTN_KV_DIM, ATTN_Q_DIM + 2 * ATTN_KV_DIM
    parts = [w_in[..., q0:k0]]
    for c0 in (k0, v0):
        for g in range(ATTN_KV_HEADS):
            head = w_in[..., c0 + g * HEAD_DIM:c0 + (g + 1) * HEAD_DIM]
            parts += [head, head]
    parts.append(w_in[..., r0:])
    return jnp.concatenate(parts, axis=-1).astype(BF16)


def _attend(q, kvs, sinks, ok):
    tq = q.shape[0]
    lc = kvs[0].shape[0]
    rep = ATTN_HEADS // ATTN_KV_HEADS
    tile = 2 * HEAD_DIM
    low_half = lax.broadcasted_iota(jnp.int32, (tq, tile), 1) < HEAD_DIM
    q = q * HEAD_DIM ** -0.5
    if ok is not None:
        mask = jnp.concatenate([jnp.ones((tq, lc), jnp.bool_), ok], axis=1)
        mask = jnp.concatenate([mask] * rep, axis=0)
    tiles = []
    for g in range(ATTN_KV_HEADS):
        kk = jnp.concatenate([t[:, g * tile:(g + 1) * tile] for t in kvs], axis=0).astype(BF16)
        vv = jnp.concatenate([t[:, KV2_DIM + g * tile:KV2_DIM + (g + 1) * tile] for t in kvs], axis=0).astype(BF16)
        heads = range(g * rep, (g + 1) * rep)
        qs = []
        for h in heads:
            q2 = q[:, (h // 2) * tile:(h // 2 + 1) * tile]
            qs.append(jnp.where(low_half == (h % 2 == 0), q2, 0.0))
        s = _dot_nt(jnp.concatenate(qs, axis=0).astype(BF16), kk)
        if ok is not None:
            s = jnp.where(mask, s, -jnp.inf)
        sink = jnp.concatenate([jnp.full((tq, 1), sinks[h], F32) for h in heads], axis=0)
        m = jnp.maximum(jnp.max(s, axis=1, keepdims=True), sink)
        e = jnp.exp(s - m)
        den = jnp.exp(sink - m) + jnp.sum(e, axis=1, keepdims=True)
        o = _dot(e.astype(BF16), vv) / den
        for r in range(0, rep, 2):
            tiles.append(jnp.where(low_half, o[r * tq:(r + 1) * tq], o[(r + 1) * tq:(r + 2) * tq]))
    return jnp.concatenate(tiles, axis=1)


def _attn_kernel(*refs, layer, nsteps, windowed):
    sinks = [refs[0][layer, h] for h in range(ATTN_HEADS)]
    if not windowed:
        _, q_ref, ckv_ref, o_ref = refs
        o_ref[0] = _attend(q_ref[0], [ckv_ref[0]], sinks, None).astype(BF16)
        return
    _, q_ref, kvp_ref, kvc_ref, kvn_ref, ckv_ref, o_ref = refs
    n = pl.program_id(1)
    ti = lax.broadcasted_iota(jnp.int32, (WBLOCK, 3 * WBLOCK), 0)
    sj = lax.broadcasted_iota(jnp.int32, (WBLOCK, 3 * WBLOCK), 1)
    rel = sj - WBLOCK - ti
    band = (rel <= WINDOW) & (rel >= -WINDOW)
    blocks = [kvp_ref[0]] + [kvc_ref[0, i * WBLOCK:(i + 1) * WBLOCK] for i in range(ATTN_STEP_BLOCKS)] + [kvn_ref[0]]
    for i in range(ATTN_STEP_BLOCKS):
        ok = band
        if i == 0:
            ok = ok & ((sj >= WBLOCK) | (n > 0))
        if i == ATTN_STEP_BLOCKS - 1:
            ok = ok & ((sj < 2 * WBLOCK) | (n < nsteps - 1))
        rows = slice(i * WBLOCK, (i + 1) * WBLOCK)
        o_ref[0, rows, :] = _attend(q_ref[0, rows, :], [ckv_ref[0]] + blocks[i:i + 3], sinks, ok).astype(BF16)


def _attention(q, kv, ckv, sink, layer):
    b, l, _ = q.shape
    lc = ckv.shape[1]
    windowed = kv is not None
    smem = pl.BlockSpec(memory_space=pltpu.SMEM)
    cspec = pl.BlockSpec((1, lc, 2 * KV2_DIM), lambda bi, n: (bi, 0, 0))
    if windowed:
        kvw = 2 * KV2_DIM
        sb = ATTN_STEP_BLOCKS
        tq = sb * WBLOCK
        nb = l // WBLOCK
        in_specs = [
            smem, pl.BlockSpec((1, tq, ATTN_Q_DIM), lambda bi, n: (bi, n, 0)),
            pl.BlockSpec((1, WBLOCK, kvw), lambda bi, n: (bi, jnp.maximum(n * sb - 1, 0), 0)),
            pl.BlockSpec((1, tq, kvw), lambda bi, n: (bi, n, 0)),
            pl.BlockSpec((1, WBLOCK, kvw), lambda bi, n: (bi, jnp.minimum((n + 1) * sb, nb - 1), 0)),
            cspec,
        ]
        args = (sink, q, kv, kv, kv, ckv)
    else:
        tq = WBLOCK
        in_specs = [smem, pl.BlockSpec((1, tq, ATTN_Q_DIM), lambda bi, n: (bi, n, 0)), cspec]
        args = (sink, q, ckv)
    return pl.pallas_call(
        functools.partial(_attn_kernel, layer=layer, nsteps=l // tq, windowed=windowed),
        grid=(b, l // tq),
        in_specs=in_specs,
        out_specs=pl.BlockSpec((1, tq, ATTN_Q_DIM), lambda bi, n: (bi, n, 0)),
        out_shape=jax.ShapeDtypeStruct((b, l, ATTN_Q_DIM), BF16),
        compiler_params=_cparams("parallel", "arbitrary"),
        name="window_attention" if windowed else "context_attention",
    )(*args)


def _block_ref_rows(cum, blk, r):
    c, w = cum.shape
    if blk % 8 == 0:
        x = cum.reshape(c // blk, blk, w)
        return jnp.broadcast_to(x[:, r:r + 1, :], x.shape).reshape(c, w)
    pos = lax.broadcasted_iota(jnp.int32, cum.shape, 0) & (blk - 1)
    out = cum
    for p in range(blk):
        delta = r - p
        if delta == 0:
            continue
        out = jnp.where(pos == p, pltpu.roll(cum, (-delta) % c, 0), out)
    return out


def _hgrn_kernel(*refs, has_init):
    if has_init:
        (tri_ref, lm_ref, dm_ref, qf_ref, gf_ref, kf_ref, vf_ref, qb_ref, gb_ref, kb_ref, vb_ref, s0_ref,
         of_ref, ob_ref, s_ref, cum_ref) = refs
    else:
        (tri_ref, lm_ref, dm_ref, qf_ref, gf_ref, kf_ref, vf_ref, qb_ref, gb_ref, kb_ref, vb_ref,
         of_ref, ob_ref, s_ref, cum_ref) = refs

    @pl.when(pl.program_id(1) == 0)
    def _():
        if has_init:
            s_ref[...] = s0_ref[...]
        else:
            s_ref[...] = jnp.zeros(s_ref.shape, F32)

    tt = qf_ref.shape[1]
    c = tri_ref.shape[1]
    nch = tt // c
    row = lax.broadcasted_iota(jnp.int32, (c, HG_DK), 0)
    levels = []
    hs = c // 2
    while hs >= 1:
        levels.append(hs)
        hs //= 2
    dirs = ((qf_ref, gf_ref, kf_ref, vf_ref, of_ref), (qb_ref, gb_ref, kb_ref, vb_ref, ob_ref))

    def direct_dd(cum, backward):
        return cum - _block_ref_rows(cum, HG_DIRECT, HG_DIRECT - 1 if backward else 0)

    worst = jnp.zeros((c, HG_DK), F32)
    for d, (_, g_ref, _, _, _) in enumerate(dirs):
        for h in range(HG_HEADS):
            for ci in range(nch):
                g3 = _split3(g_ref[0, ci * c:(ci + 1) * c, h * HG_DK:(h + 1) * HG_DK])
                cum = _dot(tri_ref[d], g3[0]) + _dot(tri_ref[d], g3[1]) + _dot(tri_ref[d], g3[2])
                cum_ref[(d * HG_HEADS + h) * nch + ci] = cum
                worst = jnp.maximum(worst, -direct_dd(cum, d == 1))
    in_range = jnp.max(worst) <= HG_EXP_LIMIT

    def scan(direct):
        tree = [(li, hs) for li, hs in enumerate(levels) if not direct or hs >= HG_DIRECT]
        for d, (q_ref, g_ref, k_ref, v_ref, o_ref) in enumerate(dirs):
            backward = d == 1
            q_halves = {li: ((row & (2 * hs - 1)) >= hs) != backward for li, hs in tree}
            for h in range(HG_HEADS):
                sl = slice(h * HG_DK, (h + 1) * HG_DK)
                q_all, k_all, v_all = q_ref[0, :, sl], k_ref[0, :, sl], v_ref[0, :, sl]
                st = s_ref[0, d, h]
                chunks = range(nch)
                for ci in (reversed(chunks) if backward else chunks):
                    rows = slice(ci * c, (ci + 1) * c)
                    q, kk, v = q_all[rows], k_all[rows], v_all[rows]
                    qb, kb, vb = q.astype(BF16), kk.astype(BF16), v.astype(BF16)
                    cum = cum_ref[(d * HG_HEADS + h) * nch + ci]
                    o = _dot_nt((q * jnp.exp(cum)).astype(BF16), st.astype(BF16))
                    if not direct:
                        o = o + jnp.sum(q * kk, axis=1, keepdims=True) * v
                    scores = jnp.zeros((c, c), F32)
                    for li, hs in tree:
                        blk = 2 * hs
                        dd = cum - _block_ref_rows(cum, blk, hs if backward else hs - 1)
                        e = jnp.exp(-jnp.abs(dd)).astype(BF16)
                        x = jnp.where(q_halves[li], qb, kb) * e
                        scores = scores + _dot_nt(x, x) * lm_ref[d, li]
                    if direct:
                        dd = direct_dd(cum, backward)
                        xq = qb * jnp.exp(dd).astype(BF16)
                        xk = kb * jnp.exp(-dd).astype(BF16)
                        scores = scores + jnp.where(dm_ref[d] > 0.5, _dot_nt(xq, xk), 0.0)
                    o = o + _dot(scores.astype(BF16), vb)
                    o_ref[0, rows, sl] = o
                    tot = cum[0:1] if backward else cum[c - 1:c]
                    kdec = kk * jnp.exp(tot - cum)
                    st = st * jnp.exp(tot) + _dot_tn(vb, kdec.astype(BF16))
                s_ref[0, d, h] = st

    pl.when(in_range)(lambda: scan(True))
    pl.when(jnp.logical_not(in_range))(lambda: scan(False))


def _hgrn_scan(hgp, consts, s0):
    b, l, _ = hgp.shape
    tt = TOKEN_TILE
    ns = l // tt
    has_init = s0 is not None
    tri, lmask, dmask = consts
    fwd = lambda j: pl.BlockSpec((1, tt, HG_DIM), lambda bi, i: (bi, i, j))
    bwd = lambda j: pl.BlockSpec((1, tt, HG_DIM), lambda bi, i: (bi, ns - 1 - i, j))
    sspec = pl.BlockSpec((1, 2, HG_HEADS, HG_DV, HG_DK), lambda bi, i: (bi, 0, 0, 0, 0))
    in_specs = [
        pl.BlockSpec(tri.shape, lambda bi, i: (0, 0, 0)),
        pl.BlockSpec(lmask.shape, lambda bi, i: (0, 0, 0, 0)),
        pl.BlockSpec(dmask.shape, lambda bi, i: (0, 0, 0)),
        fwd(HGP_Q), fwd(HGP_GF), fwd(HGP_KF), fwd(HGP_V), bwd(HGP_Q), bwd(HGP_GB), bwd(HGP_KB), bwd(HGP_V),
    ]
    args = [tri, lmask, dmask] + [hgp] * 8
    if has_init:
        in_specs.append(sspec)
        args.append(s0)
    c = tri.shape[1]
    return pl.pallas_call(
        functools.partial(_hgrn_kernel, has_init=has_init),
        grid=(b, ns),
        in_specs=in_specs,
        out_specs=[
            pl.BlockSpec((1, tt, HG_DIM), lambda bi, i: (bi, i, 0)),
            pl.BlockSpec((1, tt, HG_DIM), lambda bi, i: (bi, ns - 1 - i, 0)),
            sspec,
        ],
        out_shape=[
            jax.ShapeDtypeStruct((b, l, HG_DIM), F32),
            jax.ShapeDtypeStruct((b, l, HG_DIM), F32),
            jax.ShapeDtypeStruct((b, 2, HG_HEADS, HG_DV, HG_DK), F32),
        ],
        scratch_shapes=[pltpu.VMEM((2 * HG_HEADS * (tt // c), c, HG_DK), F32)],
        compiler_params=_cparams("parallel", "arbitrary"),
        name="hgrn_scan",
    )(*args)


def _hgrn_constants(c):
    t = lax.broadcasted_iota(jnp.int32, (c, c), 0)
    s = lax.broadcasted_iota(jnp.int32, (c, c), 1)
    tri = jnp.stack([s <= t, s >= t]).astype(BF16)
    masks, direct = [], []
    for backward in (False, True):
        per_level = []
        hs = c // 2
        while hs >= 1:
            blk = 2 * hs
            same = (t // blk) == (s // blk)
            t_late, s_late = (t % blk) >= hs, (s % blk) >= hs
            pair = (jnp.logical_not(t_late) & s_late) if backward else (t_late & jnp.logical_not(s_late))
            per_level.append(same & pair)
            hs //= 2
        masks.append(jnp.stack(per_level))
        direct.append(((t // HG_DIRECT) == (s // HG_DIRECT)) & ((s >= t) if backward else (s <= t)))
    return tri, jnp.stack(masks).astype(F32), jnp.stack(direct).astype(F32)


def _mixout_kernel(attn_ref, of_ref, ob_ref, gh_ref, ga_ref, gb_ref, x_ref, g1_ref, sc2_ref, sh2_ref,
                   ng_ref, wa_ref, wh_ref, wo_ref, lng_ref, lnb_ref, wrh_ref, wrl_ref,
                   xo_ref, u2_ref, aff_ref, *, alpha):
    for r0 in range(0, x_ref.shape[1], TOKEN_TILE):
        rows = slice(r0, r0 + TOKEN_TILE)
        o = of_ref[0, rows, :] + ob_ref[0, rows, :]
        parts = []
        for h in range(HG_HEADS):
            oh = o[:, h * HG_DV:(h + 1) * HG_DV]
            parts.append(oh * lax.rsqrt(jnp.mean(oh * oh, axis=-1, keepdims=True) + LN_EPS))
        hg = jnp.concatenate(parts, axis=1) * ng_ref[...] * gh_ref[0, rows, :]
        ya = _dot(attn_ref[0, rows, :], wa_ref[...])
        yh = _dot(hg.astype(BF16), wh_ref[...])
        mix = ga_ref[0, rows, :] * ya + gb_ref[0, rows, :] * yh
        y = _dot(mix.astype(BF16), wo_ref[...])
        xn = _layer_norm(alpha * x_ref[0, rows, :] + g1_ref[0] * y) * lng_ref[...] + lnb_ref[...]
        xo_ref[0, rows, :] = xn
        u2 = _layer_norm(xn) * (1.0 + sc2_ref[0]) + sh2_ref[0]
        u2_ref[0, rows, :] = u2.astype(BF16)
        uh, ul = _split2(u2)
        lg = _dot(uh, wrh_ref[...]) + _dot(ul, wrh_ref[...]) + _dot(uh, wrl_ref[...])
        lg = jnp.where(lax.broadcasted_iota(jnp.int32, lg.shape, 1) < N_EXPERTS, lg, -jnp.inf)
        ex = jnp.exp(lg - jnp.max(lg, axis=1, keepdims=True))
        aff = ex / jnp.sum(ex, axis=1, keepdims=True)
        aff_ref[0, :, rows] = aff.T[0:N_EXPERTS, :]


def _mixer_output(attn, o_f, o_b, hgp, gab, x, g1, sc2, sh2, norm_g, w_a, w_h, w_o, ln_g, ln_b,
                  wr_hi, wr_lo, layer, alpha):
    b, l, d = x.shape
    tm = min(l, MIXER_STEP_TILES * TOKEN_TILE)
    per_batch = g1.shape[0] > 1
    mod_map = (lambda bi, i: (bi, 0, 0)) if per_batch else (lambda bi, i: (0, 0, 0))
    tok = lambda w, j=0: pl.BlockSpec((1, tm, w), lambda bi, i: (bi, i, j))
    lay2 = lambda r, c: pl.BlockSpec((None, r, c), lambda bi, i: (layer, 0, 0))
    in_specs = [
        tok(ATTN_Q_DIM), tok(HG_DIM), tok(HG_DIM), tok(HG_DIM, HGP_OG), tok(d, 0), tok(d, 1), tok(d),
        pl.BlockSpec((1, 1, d), mod_map), pl.BlockSpec((1, 1, d), mod_map), pl.BlockSpec((1, 1, d), mod_map),
        lay2(1, HG_DIM), lay2(ATTN_Q_DIM, d), lay2(HG_DIM, d), lay2(d, d),
        pl.BlockSpec((None, None, 1, d), lambda bi, i: (layer, 0, 0, 0)),
        pl.BlockSpec((None, None, 1, d), lambda bi, i: (layer, 0, 0, 0)),
        lay2(d, ROUTER_LANES), lay2(d, ROUTER_LANES),
    ]
    return pl.pallas_call(
        functools.partial(_mixout_kernel, alpha=alpha),
        grid=(b, l // tm),
        in_specs=in_specs,
        out_specs=[tok(d), tok(d), pl.BlockSpec((1, N_EXPERTS, tm), lambda bi, i: (bi, 0, i))],
        out_shape=[
            jax.ShapeDtypeStruct((b, l, d), F32),
            jax.ShapeDtypeStruct((b, l, d), BF16),
            jax.ShapeDtypeStruct((b, N_EXPERTS, l), F32),
        ],
        compiler_params=_cparams("parallel", "arbitrary"),
        name="mixer_output",
    )(attn, o_f, o_b, hgp, gab, gab, x, g1, sc2, sh2, norm_g, w_a, w_h, w_o, ln_g, ln_b, wr_hi, wr_lo)


def _select_kernel(aff_ref, triu_ref, rk_ref, rkt_ref, st_ref, thr_ref, *, cap):
    a = aff_ref[0]
    ne, n = a.shape
    ts = triu_ref.shape[0]
    capf = jnp.float32(cap)

    def count_ge(v):
        return jnp.sum(jnp.where(a >= v, 1.0, 0.0), axis=1, keepdims=True)

    def body(_, carry):
        lo, hi = carry
        mid = 0.5 * (lo + hi)
        ok = count_ge(mid) >= capf
        return jnp.where(ok, mid, lo), jnp.where(ok, hi, mid)

    lo0 = jnp.zeros((ne, 1), F32)
    hi0 = jnp.full((ne, 1), 2.0, F32)
    _, hi = lax.fori_loop(0, SELECT_BISECTIONS, body, (lo0, hi0))
    thr = jnp.max(jnp.where(a < hi, a, -1.0), axis=1, keepdims=True)
    for _ in range(SELECT_REFINEMENTS):
        below = jnp.max(jnp.where(a < thr, a, -1.0), axis=1, keepdims=True)
        thr = jnp.where(count_ge(thr) < capf, below, thr)

    def count_gt(v):
        return jnp.sum(jnp.where(a > v, 1.0, 0.0), axis=1, keepdims=True)

    thr_ref[...] = jnp.broadcast_to(thr, thr_ref.shape)
    found = jnp.min(jnp.where((count_gt(thr) < capf) & (count_ge(thr) >= capf), 1.0, 0.0)) > 0.5

    @pl.when(jnp.logical_not(found))
    def _():
        bits = lax.bitcast_convert_type(a, jnp.int32)

        def bit_body(_, carry):
            lo, hi = carry
            mid = lo + ((hi - lo) >> 1)
            ok = jnp.sum(jnp.where(bits >= mid, 1.0, 0.0), axis=1, keepdims=True) >= capf
            return jnp.where(ok, mid, lo), jnp.where(ok, hi, mid)

        lo_bits, _ = lax.fori_loop(0, 31, bit_body, (jnp.zeros((ne, 1), jnp.int32),
                                                     jnp.full((ne, 1), 0x7F800000, jnp.int32)))
        thr_ref[...] = jnp.broadcast_to(lax.bitcast_convert_type(lo_bits, F32), thr_ref.shape)

    thr = thr_ref[:, 0:1]
    gt = a > thr
    eq = a == thr
    need = capf - count_gt(thr)
    triu = triu_ref[...]
    lane = lax.broadcasted_iota(jnp.int32, st_ref.shape[1:], 1)
    eq_carry = jnp.zeros((ne, 1), F32)
    sel_carry = jnp.zeros((ne, 1), F32)
    starts = jnp.zeros(st_ref.shape[1:], F32)
    for t in range(n // ts):
        sl = slice(t * ts, (t + 1) * ts)
        eq_t = eq[:, sl]
        eq_pref = _dot(jnp.where(eq_t, 1.0, 0.0).astype(BF16), triu) + eq_carry
        sel_t = gt[:, sl] | (eq_t & (eq_pref <= need))
        sel_f = jnp.where(sel_t, 1.0, 0.0)
        sel_pref = _dot(sel_f.astype(BF16), triu) + sel_carry
        rk_t = jnp.where(sel_t, sel_pref - 1.0, -1.0)
        rk_ref[0, :, sl] = rk_t
        rkt_ref[0, sl, :] = rk_t.T
        starts = jnp.where(lane == t, sel_carry, starts)
        eq_carry = eq_pref[:, ts - 1:ts]
        sel_carry = sel_pref[:, ts - 1:ts]
    st_ref[0] = starts


def _select(aff, triu, cap):
    b, ne, n = aff.shape
    return pl.pallas_call(
        functools.partial(_select_kernel, cap=cap),
        grid=(b,),
        in_specs=[
            pl.BlockSpec((1, ne, n), lambda bi: (bi, 0, 0)),
            pl.BlockSpec(triu.shape, lambda bi: (0, 0)),
        ],
        out_specs=[
            pl.BlockSpec((1, ne, n), lambda bi: (bi, 0, 0)),
            pl.BlockSpec((1, n, ne), lambda bi: (bi, 0, 0)),
            pl.BlockSpec((1, ne, 128), lambda bi: (bi, 0, 0)),
        ],
        out_shape=[
            jax.ShapeDtypeStruct((b, ne, n), F32),
            jax.ShapeDtypeStruct((b, n, ne), F32),
            jax.ShapeDtypeStruct((b, ne, 128), F32),
        ],
        scratch_shapes=[pltpu.VMEM((ne, 128), F32)],
        compiler_params=_cparams("parallel"),
        name="expert_choice_select",
    )(aff, triu)


def _slot_window(cap, tile):
    return min(cap, tile + SLOT_ALIGN)


def _aligned_start(s0, cap, w):
    return jnp.minimum((s0 // SLOT_ALIGN) * SLOT_ALIGN, cap - w)


def _window_start(s0, cap, tile):
    return _aligned_start(s0, cap, _slot_window(cap, tile))


def _ffn_kernel(st_ref, u_ref, rk_ref, aff_ref, wg_ref, wu_ref, wd_ref, y_ref, xs_ref,
                *, cap, ns, unroll, expert_major):
    b, e = pl.program_id(0), pl.program_id(1)
    if expert_major:
        b, e = e, b
    d = u_ref.shape[2]
    nt = rk_ref.shape[2]
    w_full = _slot_window(cap, TOKEN_TILE)
    widths = [w for w in DISPATCH_WINDOWS if w < w_full] + [w_full]
    sub = TOKEN_TILE // ROUTE_TILE
    base = (b * N_EXPERTS + e) * (ns + 1)

    def bounds(t):
        return st_ref[base + t * sub], st_ref[base + (t + 1) * sub]

    def dispatch(t, w):
        s0, s1 = bounds(t)
        a = pl.multiple_of(_aligned_start(s0, cap, w), SLOT_ALIGN)
        slot = a + lax.broadcasted_iota(jnp.int32, (w, 1), 0)
        hit = slot.astype(F32) == rk_ref[0, 0, pl.ds(t, 1), :]
        mine = (slot >= s0) & (slot < s1)
        tok0 = pl.multiple_of(t * TOKEN_TILE, TOKEN_TILE)
        rows = _dot(jnp.where(hit, 1.0, 0.0).astype(BF16), u_ref[0, pl.ds(tok0, TOKEN_TILE), :])
        wts = jnp.sum(jnp.where(hit, aff_ref[0, 0, pl.ds(t, 1), :], 0.0), axis=1, keepdims=True)
        vals = jnp.concatenate([rows, jnp.broadcast_to(wts, (w, 128))], axis=1)
        pltpu.store(xs_ref.at[pl.ds(a, w), :], vals, mask=jnp.broadcast_to(mine, vals.shape))

    def run(w):
        def group(i, carry):
            for j in range(unroll):
                dispatch(i * unroll + j, w)
            return carry

        lax.fori_loop(0, nt // unroll, group, 0)

    def fits(t, ok):
        s0, s1 = bounds(t)
        return tuple(o & (s1 - _aligned_start(s0, cap, w) <= w) for o, w in zip(ok, widths[:-1]))

    ok = lax.fori_loop(0, nt, fits, (True,) * (len(widths) - 1))
    wider_needed = True
    for i, w in enumerate(widths):
        fit = ok[i] if i < len(ok) else True
        pl.when(jnp.logical_and(wider_needed, fit))(functools.partial(run, w))
        wider_needed = jnp.logical_and(wider_needed, jnp.logical_not(fit))

    rc = min(cap, 256)
    for c in range(cap // rc):
        rows = slice(c * rc, (c + 1) * rc)
        xb = xs_ref[rows, 0:d].astype(BF16)
        h = _silu(_dot(xb, wg_ref[0])) * _dot(xb, wu_ref[0])
        y_ref[0, 0, rows, :] = (_dot(h.astype(BF16), wd_ref[0]) * xs_ref[rows, d:d + 1]).astype(BF16)


def _expert_ffn(starts, u2, rk, aff, w_gate, w_up, w_down, layer, cap):
    b, n, d = u2.shape
    ff = w_gate.shape[-1]
    ns = n // ROUTE_TILE
    nt = n // TOKEN_TILE
    rk4 = rk.reshape(b, N_EXPERTS, nt, TOKEN_TILE)
    aff4 = aff.reshape(b, N_EXPERTS, nt, TOKEN_TILE)
    expert_major = n * d < 3 * d * ff
    grid = (N_EXPERTS, b) if expert_major else (b, N_EXPERTS)
    be = (lambda i, j: (j, i)) if expert_major else (lambda i, j: (i, j))
    wspec = lambda r, c: pl.BlockSpec((None, 1, r, c), lambda i, j, st: (layer, be(i, j)[1], 0, 0))
    rspec = pl.BlockSpec((1, 1, nt, TOKEN_TILE), lambda i, j, st: (*be(i, j), 0, 0))
    grid_spec = pltpu.PrefetchScalarGridSpec(
        num_scalar_prefetch=1,
        grid=grid,
        in_specs=[
            pl.BlockSpec((1, n, d), lambda i, j, st: (be(i, j)[0], 0, 0),
                         pipeline_mode=None if expert_major else pl.Buffered(1)),
            rspec, rspec,
            wspec(d, ff), wspec(d, ff), wspec(ff, d),
        ],
        out_specs=pl.BlockSpec((1, 1, cap, d), lambda i, j, st: (*be(i, j), 0, 0)),
        scratch_shapes=[pltpu.VMEM((cap, d + 128), F32)],
    )
    return pl.pallas_call(
        functools.partial(_ffn_kernel, cap=cap, ns=ns, unroll=min(nt, 16), expert_major=expert_major),
        grid_spec=grid_spec,
        out_shape=jax.ShapeDtypeStruct((b, N_EXPERTS, cap, d), BF16),
        compiler_params=_cparams("parallel", "arbitrary"),
        name="expert_ffn",
    )(starts, u2, rk4, aff4, w_gate, w_up, w_down)


def _ffn_stacked_kernel(u_ref, rk_ref, aff_ref, wg_ref, wu_ref, wd_ref, y_ref, xs_ref, *, cap):
    nb, _, d = u_ref.shape
    slot = lax.broadcasted_iota(jnp.int32, (cap, 1), 0).astype(F32)
    for b in range(nb):
        hit = slot == rk_ref[b, 0]
        rows = _dot(jnp.where(hit, 1.0, 0.0).astype(BF16), u_ref[b])
        wts = jnp.sum(jnp.where(hit, aff_ref[b, 0], 0.0), axis=1, keepdims=True)
        xs_ref[b * cap:(b + 1) * cap, :] = jnp.concatenate([rows, jnp.broadcast_to(wts, (cap, 128))], axis=1)
    xb = xs_ref[:, 0:d].astype(BF16)
    h = _silu(_dot(xb, wg_ref[0])) * _dot(xb, wu_ref[0])
    y = (_dot(h.astype(BF16), wd_ref[0]) * xs_ref[:, d:d + 1]).astype(BF16)
    for b in range(nb):
        y_ref[b, 0] = y[b * cap:(b + 1) * cap]


def _expert_ffn_stacked(u2, rk, aff, w_gate, w_up, w_down, layer, cap):
    b, n, d = u2.shape
    ff = w_gate.shape[-1]
    rk4 = rk.reshape(b, N_EXPERTS, 1, n)
    aff4 = aff.reshape(b, N_EXPERTS, 1, n)
    wspec = lambda r, c: pl.BlockSpec((None, 1, r, c), lambda e: (layer, e, 0, 0))
    rspec = pl.BlockSpec((b, 1, 1, n), lambda e: (0, e, 0, 0))
    return pl.pallas_call(
        functools.partial(_ffn_stacked_kernel, cap=cap),
        grid=(N_EXPERTS,),
        in_specs=[pl.BlockSpec((b, n, d), lambda e: (0, 0, 0)), rspec, rspec,
                  wspec(d, ff), wspec(d, ff), wspec(ff, d)],
        out_specs=pl.BlockSpec((b, 1, cap, d), lambda e: (0, e, 0, 0)),
        out_shape=jax.ShapeDtypeStruct((b, N_EXPERTS, cap, d), BF16),
        scratch_shapes=[pltpu.VMEM((b * cap, d + 128), F32)],
        compiler_params=_cparams("arbitrary"),
        name="expert_ffn_stacked",
    )(u2, rk4, aff4, w_gate, w_up, w_down)


def _combine_kernel(*refs, cap, ns, alpha):
    st_ref, rkt_ref, x_ref, g2_ref, lng_ref, lnb_ref = refs[:6]
    y_refs = refs[6:6 + N_EXPERTS]
    o_ref = refs[6 + N_EXPERTS]
    b, t = pl.program_id(0), pl.program_id(1)
    sub = TOKEN_TILE // ROUTE_TILE
    w = _slot_window(cap, ROUTE_TILE)
    for half in range(sub):
        rows = slice(half * ROUTE_TILE, (half + 1) * ROUTE_TILE)
        acc = jnp.zeros((ROUTE_TILE, x_ref.shape[2]), F32)
        for e in range(N_EXPERTS):
            base = (b * N_EXPERTS + e) * (ns + 1) + t * sub
            a_tile = _window_start(st_ref[base], cap, TOKEN_TILE)
            a = _window_start(st_ref[base + half], cap, ROUTE_TILE)
            off = pl.multiple_of(a - a_tile, SLOT_ALIGN)
            slot = (a + lax.broadcasted_iota(jnp.int32, (1, w), 1)).astype(F32)
            hit = rkt_ref[0, rows, e:e + 1] == slot
            acc = acc + _dot(jnp.where(hit, 1.0, 0.0).astype(BF16), y_refs[e][pl.ds(off, w), :])
        xr = alpha * x_ref[0, rows, :] + g2_ref[0] * acc
        o_ref[0, rows, :] = _layer_norm(xr) * lng_ref[...] + lnb_ref[...]


def _combine(starts, rkt, x, g2, ln_g, ln_b, y, layer, cap, alpha):
    b, n, d = x.shape
    tt = TOKEN_TILE
    ns = n // ROUTE_TILE
    sub = TOKEN_TILE // ROUTE_TILE
    w = _slot_window(cap, tt)
    per_batch = g2.shape[0] > 1
    mod_map = (lambda bi, t, st: (bi, 0, 0)) if per_batch else (lambda bi, t, st: (0, 0, 0))

    def y_spec(e):
        def index_map(bi, t, st):
            be = bi * N_EXPERTS + e
            a = _window_start(st[be * (ns + 1) + t * sub], cap, tt)
            return pl.multiple_of(be * cap + a, SLOT_ALIGN), 0
        return pl.BlockSpec((pl.Element(w), pl.Element(d)), index_map)

    y = y.reshape(b * N_EXPERTS * cap, d)
    grid_spec = pltpu.PrefetchScalarGridSpec(
        num_scalar_prefetch=1,
        grid=(b, n // tt),
        in_specs=[
            pl.BlockSpec((1, tt, N_EXPERTS), lambda bi, t, st: (bi, t, 0)),
            pl.BlockSpec((1, tt, d), lambda bi, t, st: (bi, t, 0)),
            pl.BlockSpec((1, 1, d), mod_map),
            pl.BlockSpec((None, None, 1, d), lambda bi, t, st: (layer, 1, 0, 0)),
            pl.BlockSpec((None, None, 1, d), lambda bi, t, st: (layer, 1, 0, 0)),
        ] + [y_spec(e) for e in range(N_EXPERTS)],
        out_specs=pl.BlockSpec((1, tt, d), lambda bi, t, st: (bi, t, 0)),
    )
    return pl.pallas_call(
        functools.partial(_combine_kernel, cap=cap, ns=ns, alpha=alpha),
        grid_spec=grid_spec,
        out_shape=jax.ShapeDtypeStruct((b, n, d), F32),
        compiler_params=_cparams("parallel", "arbitrary"),
        name="moe_combine",
    )(starts, rkt, x, g2, ln_g, ln_b, *([y] * N_EXPERTS))


def _combine_packed_kernel(st_ref, rkt_ref, x_ref, g2_ref, lng_ref, lnb_ref, y_hbm, o_ref, ybuf, sem,
                           *, cap, ns, alpha):
    b, t = pl.program_id(0), pl.program_id(1)
    nt = pl.num_programs(1)
    step = b * nt + t
    sub = TOKEN_TILE // ROUTE_TILE
    w = COMBINE_WINDOW

    def window_start(bb, tt, half, e):
        return _aligned_start(st_ref[(bb * N_EXPERTS + e) * (ns + 1) + tt * sub + half], cap, w)

    def copies(bb, tt, slot):
        out = []
        for half in range(sub):
            for e in range(N_EXPERTS):
                row0 = pl.multiple_of((bb * N_EXPERTS + e) * cap + window_start(bb, tt, half, e), SLOT_ALIGN)
                out.append(pltpu.make_async_copy(y_hbm.at[pl.ds(row0, w), :],
                                                 ybuf.at[slot, half, pl.ds(e * w, w), :], sem.at[slot]))
        return out

    slot = step % 2

    @pl.when(step == 0)
    def _():
        for cp in copies(b, t, 0):
            cp.start()

    @pl.when(step + 1 < pl.num_programs(0) * nt)
    def _():
        wrap = t + 1 == nt
        for cp in copies(jnp.where(wrap, b + 1, b), jnp.where(wrap, 0, t + 1), 1 - slot):
            cp.start()

    for cp in copies(b, t, slot):
        cp.wait()

    lanes = 128
    per = lanes // w
    lane = lax.broadcasted_iota(jnp.int32, (1, lanes), 1)
    for half in range(sub):
        rows = slice(half * ROUTE_TILE, (half + 1) * ROUTE_TILE)
        hits = []
        for e0 in range(0, N_EXPERTS, per):
            slot_ids = window_start(b, t, half, e0) + lane
            rank = rkt_ref[0, rows, e0:e0 + 1]
            for j in range(1, per):
                own = lane >= j * w
                slot_ids = jnp.where(own, window_start(b, t, half, e0 + j) + lane - j * w, slot_ids)
                rank = jnp.where(own, rkt_ref[0, rows, e0 + j:e0 + j + 1], rank)
            hits.append(jnp.where(rank == slot_ids.astype(F32), 1.0, 0.0).astype(BF16))
        acc = _dot(jnp.concatenate(hits, axis=1), ybuf[slot, half])
        xr = alpha * x_ref[0, rows, :] + g2_ref[0] * acc
        o_ref[0, rows, :] = _layer_norm(xr) * lng_ref[...] + lnb_ref[...]


def _combine_packed(starts, rkt, x, g2, ln_g, ln_b, y, layer, cap, alpha):
    b, n, d = x.shape
    tt = TOKEN_TILE
    ns = n // ROUTE_TILE
    sub = TOKEN_TILE // ROUTE_TILE
    per_batch = g2.shape[0] > 1
    mod_map = (lambda bi, t, st: (bi, 0, 0)) if per_batch else (lambda bi, t, st: (0, 0, 0))
    grid_spec = pltpu.PrefetchScalarGridSpec(
        num_scalar_prefetch=1,
        grid=(b, n // tt),
        in_specs=[
            pl.BlockSpec((1, tt, N_EXPERTS), lambda bi, t, st: (bi, t, 0)),
            pl.BlockSpec((1, tt, d), lambda bi, t, st: (bi, t, 0)),
            pl.BlockSpec((1, 1, d), mod_map),
            pl.BlockSpec((None, None, 1, d), lambda bi, t, st: (layer, 1, 0, 0)),
            pl.BlockSpec((None, None, 1, d), lambda bi, t, st: (layer, 1, 0, 0)),
            pl.BlockSpec(memory_space=pl.ANY),
        ],
        out_specs=pl.BlockSpec((1, tt, d), lambda bi, t, st: (bi, t, 0)),
        scratch_shapes=[
            pltpu.VMEM((2, sub, N_EXPERTS * COMBINE_WINDOW, d), BF16),
            pltpu.SemaphoreType.DMA((2,)),
        ],
    )
    return pl.pallas_call(
        functools.partial(_combine_packed_kernel, cap=cap, ns=ns, alpha=alpha),
        grid_spec=grid_spec,
        out_shape=jax.ShapeDtypeStruct((b, n, d), F32),
        compiler_params=_cparams("arbitrary", "arbitrary"),
        name="moe_combine_packed",
    )(starts, rkt, x, g2, ln_g, ln_b, y.reshape(b * N_EXPERTS * cap, d))


def _moe(x, u2, aff, g2, ln_g, ln_b, triu, w_gate, w_up, w_down, layer, alpha):
    b, n, _ = x.shape
    cap = CAPACITY_FACTOR * n // N_EXPERTS
    ns = n // ROUTE_TILE
    rk, rkt, st = _select(aff, triu, cap)
    starts = jnp.concatenate([st[:, :, :ns], jnp.full((b, N_EXPERTS, 1), cap, F32)], axis=2)
    starts = starts.astype(jnp.int32).reshape(-1)
    if n == TOKEN_TILE and b * cap <= TOKEN_TILE:
        y = _expert_ffn_stacked(u2, rk, aff, w_gate, w_up, w_down, layer, cap)
    else:
        y = _expert_ffn(starts, u2, rk, aff, w_gate, w_up, w_down, layer, cap)
    args = (starts, rkt, x, g2, ln_g, ln_b, y)
    general = lambda *a: _combine(*a, layer, cap, alpha)
    if cap < COMBINE_WINDOW:
        return general(*args)
    s0 = starts.reshape(b, N_EXPERTS, ns + 1)
    fits = jnp.all(s0[:, :, 1:] - _aligned_start(s0[:, :, :-1], cap, COMBINE_WINDOW) <= COMBINE_WINDOW)
    return lax.cond(fits, lambda *a: _combine_packed(*a, layer, cap, alpha), general, *args)


def _rope_tables(l):
    pos = jnp.arange(l)
    nf = HEAD_DIM // 4
    inv = ROPE_THETA ** (-jnp.arange(nf, dtype=F32) / nf)
    ang_r = (pos // GRID_W).astype(F32)[:, None] * inv[None, :]
    ang_c = (pos % GRID_W).astype(F32)[:, None] * inv[None, :]
    cos = jnp.concatenate([jnp.cos(ang_r)] * 2 + [jnp.cos(ang_c)] * 2, axis=1)
    sin = jnp.concatenate([-jnp.sin(ang_r), jnp.sin(ang_r), -jnp.sin(ang_c), jnp.sin(ang_c)], axis=1)
    return jnp.tile(cos, (1, ATTN_HEADS)), jnp.tile(sin, (1, ATTN_HEADS))


def kernel(x, c, ctx, c_ctx, w_mod, b_mod, w_in, attn_sink, hgrn_lb_fw, hgrn_lb_bw, hgrn_norm_g,
           w_branch_attn, w_branch_hgrn, w_out, w_router, w_gate, w_up, w_down, ln_g, ln_b):
    depth, d = w_mod.shape[0], x.shape[-1]
    b = x.shape[0]
    assert x.shape[1] % (4 * TOKEN_TILE) == 0 and ctx.shape[1] == TOKEN_TILE
    alpha = (2 * depth) ** 0.25
    mod_rows = -(-(b + 1) // 8) * 8
    c_all = jnp.concatenate([c, c_ctx[None, :], jnp.zeros((mod_rows - b - 1, d), F32)], axis=0)
    b_mod3 = b_mod[:, None, :]
    norm_g3 = hgrn_norm_g[:, None, :]
    ln_g4, ln_b4 = ln_g[:, :, None, :], ln_b[:, :, None, :]
    w_in_b = _relayout_w_in(w_in)
    w_a_b, w_h_b, w_o_b = w_branch_attn.astype(BF16), w_branch_hgrn.astype(BF16), w_out.astype(BF16)
    w_g_b, w_u_b, w_d_b = w_gate.astype(BF16), w_up.astype(BF16), w_down.astype(BF16)
    wr_pad = jnp.pad(w_router, ((0, 0), (0, 0), (0, ROUTER_LANES - N_EXPERTS)))
    wr_hi = wr_pad.astype(BF16)
    wr_lo = (wr_pad - wr_hi.astype(F32)).astype(BF16)
    rope_tabs = _rope_tables(x.shape[1])
    hg_consts = _hgrn_constants(HG_CHUNK)
    r_i = lax.broadcasted_iota(jnp.int32, (ROUTE_TILE, ROUTE_TILE), 0)
    c_i = lax.broadcasted_iota(jnp.int32, (ROUTE_TILE, ROUTE_TILE), 1)
    triu = (r_i <= c_i).astype(BF16)

    xc = ctx
    for l in range(depth):
        need_ctx = l < depth - 1
        mod = _modulation(c_all, w_mod, b_mod3, l)
        sh1, sc1, g1, sh2, sc2, g2 = [mod[:b, i * d:(i + 1) * d][:, None, :] for i in range(N_MOD)]
        sh1c, sc1c, g1c, sh2c, sc2c, g2c = [mod[b:b + 1, i * d:(i + 1) * d][:, None, :] for i in range(N_MOD)]
        cq, ckv, chg, cgab = _in_projection(xc, sc1c, sh1c, w_in_b, hgrn_lb_fw, hgrn_lb_bw, l, None)
        oc_f, oc_b, s_ctx = _hgrn_scan(chg, hg_consts, None)
        q, kv, hgp, gab = _in_projection(x, sc1, sh1, w_in_b, hgrn_lb_fw, hgrn_lb_bw, l, rope_tabs)
        o_f, o_b, _ = _hgrn_scan(hgp, hg_consts, s_ctx)
        attn = _attention(q, kv, ckv, attn_sink, l)
        x1, u2, aff = _mixer_output(attn, o_f, o_b, hgp, gab, x, g1, sc2, sh2, norm_g3, w_a_b, w_h_b, w_o_b,
                                    ln_g4, ln_b4, wr_hi, wr_lo, l, alpha)
        x = _moe(x1, u2, aff, g2, ln_g4, ln_b4, triu, w_g_b, w_u_b, w_d_b, l, alpha)
        if need_ctx:
            attn_c = _attention(cq, None, ckv, attn_sink, l)
            xc1, uc2, affc = _mixer_output(attn_c, oc_f, oc_b, chg, cgab, xc, g1c, sc2c, sh2c, norm_g3,
                                           w_a_b, w_h_b, w_o_b, ln_g4, ln_b4, wr_hi, wr_lo, l, alpha)
            xc = _moe(xc1, uc2, affc, g2c, ln_g4, ln_b4, triu, w_g_b, w_u_b, w_d_b, l, alpha)
    return x
```
